```python
import jax, jax.numpy as jnp
from jax import lax
import numpy as np

D_MODEL = 1024
BATCH = 4
SEQ = 4096
DEPTH = 2
DEC_BATCH = 16
DEC_SEQ = 16
PAST_LEN = 4096

CHUNK = 64
Q_BLOCK = 128
EPS = 1e-6
A_CHUNK = 128
A_GROUPS = 4
A_GROUP_DIM = 128
A_HALF = A_GROUPS * A_GROUP_DIM
B_HEADS = 8
B_KV_HEADS = 2
B_HEAD_DIM = 64
B_TOPK_MAX = 256
IDX_HEADS = 8
IDX_DIM = 32
C_HEADS = 8
C_HEAD_DIM = 64
N_BRANCH = 3
D_FF = ((8 * D_MODEL // 3 + 255) // 256) * 256
PLE_DIM = 256
IN_COLS = (2 * A_HALF + (B_HEADS + 2 * B_KV_HEADS) * B_HEAD_DIM + IDX_HEADS * IDX_DIM + IDX_DIM
           + IDX_HEADS + 3 * C_HEADS * C_HEAD_DIM + N_BRANCH * D_MODEL)

kernel_name = 'hybrid_gmlp_dsa_stickbreak_stream_step'


def rms_norm(x, g):
    xf = x.astype(jnp.float32)
    y = xf * lax.rsqrt(jnp.mean(xf * xf, axis=-1, keepdims=True) + EPS)
    return (y * g.astype(jnp.float32)).astype(x.dtype)


def split_columns(z):
    sizes = (A_HALF, A_HALF, B_HEADS * B_HEAD_DIM, B_KV_HEADS * B_HEAD_DIM, B_KV_HEADS * B_HEAD_DIM,
             IDX_HEADS * IDX_DIM, IDX_DIM, IDX_HEADS, C_HEADS * C_HEAD_DIM, C_HEADS * C_HEAD_DIM,
             C_HEADS * C_HEAD_DIM, N_BRANCH * D_MODEL)
    offs, acc = [], 0
    for s in sizes[:-1]:
        acc += s
        offs.append(acc)
    return jnp.split(z, offs, axis=-1)


def to_blocks(a):
    b, s = a.shape[:2]
    return a.reshape(b, s // Q_BLOCK, Q_BLOCK, *a.shape[2:]).swapaxes(0, 1)


def from_blocks(o):
    nb, b, qb = o.shape[:3]
    return o.swapaxes(0, 1).reshape(b, nb * qb, *o.shape[3:])


def gmlp_spatial(u, v, ws, bias):
    b, t, _ = v.shape
    nc = -(-t // A_CHUNK)
    vp = jnp.pad(v, ((0, 0), (0, nc * A_CHUNK - t), (0, 0)))
    vp = vp.reshape(b, nc, A_CHUNK, A_GROUPS, A_GROUP_DIM)
    i = jnp.arange(A_CHUNK)
    mask = (i[None, :] // CHUNK) <= (i[:, None] // CHUNK)
    wm = jnp.where(mask[None], ws, 0)
    mixed = jnp.einsum('gij,bcjgd->bcigd', wm, vp) + bias.T[None, None, :, :, None]
    mixed = mixed.reshape(b, nc * A_CHUNK, A_HALF)[:, :t]
    return u * mixed


def dsa_attend(q, qi, wi, q_pos, k, v, ki, k_pos, topk):
    f32 = jnp.float32
    b, nq = q.shape[:2]
    dots = jnp.einsum('bqhe,ble->bqhl', qi.astype(f32), ki.astype(f32)) * (IDX_DIM ** -0.5)
    score = jnp.einsum('bqhl,bqh->bql', jax.nn.relu(dots), wi.astype(f32))
    admissible = (k_pos[None, :] // CHUNK) <= (q_pos[:, None] // CHUNK)
    score = jnp.where(admissible[None], score, -jnp.inf)
    _, idx = lax.top_k(score, topk)
    valid = (k_pos[idx] // CHUNK) <= (q_pos // CHUNK)[None, :, None]
    gather = jax.vmap(lambda a, ix: a[ix])
    k_sel = gather(k, idx).astype(f32)
    v_sel = gather(v, idx).astype(f32)
    qg = q.reshape(b, nq, B_KV_HEADS, B_HEADS // B_KV_HEADS, B_HEAD_DIM).astype(f32)
    logits = jnp.einsum('bqhgd,bqnhd->bqhgn', qg, k_sel) * (B_HEAD_DIM ** -0.5)
    logits = jnp.where(valid[:, :, None, None, :], logits, -jnp.inf)
    probs = jax.nn.softmax(logits, axis=-1)
    o = jnp.einsum('bqhgn,bqnhd->bqhgd', probs, v_sel)
    return o.reshape(b, nq, B_HEADS * B_HEAD_DIM).astype(q.dtype)


def dsa_prompt(q, qi, wi, k, v, ki):
    s = q.shape[1]
    topk = min(B_TOPK_MAX, s // 4)
    pos = jnp.arange(s)

    def blk(args):
        qb, qib, wib, pb = args
        return dsa_attend(qb, qib, wib, pb, k, v, ki, pos, topk)

    out = lax.map(blk, (to_blocks(q), to_blocks(qi), to_blocks(wi), pos.reshape(s // Q_BLOCK, Q_BLOCK)))
    return from_blocks(out)


def stick_breaking(q, q_pos, k, v, k_pos):
    f32 = jnp.float32
    b, nq = q.shape[:2]
    z = jnp.einsum('bqhd,blhd->bhql', q.astype(f32), k.astype(f32)) * (C_HEAD_DIM ** -0.5)
    mask = k_pos[None, :] < q_pos[:, None]
    log_stay = jnp.where(mask, jax.nn.log_sigmoid(-z), 0.0)
    after = lax.cumsum(log_stay, axis=3, reverse=True) - log_stay
    w = jnp.where(mask, jnp.exp(jax.nn.log_sigmoid(z) + after), 0.0)
    o = jnp.einsum('bhql,blhd->bqhd', w, v.astype(f32))
    return o.reshape(b, nq, C_HEADS * C_HEAD_DIM).astype(q.dtype)


def stick_prompt(q, k, v):
    s = q.shape[1]
    pos = jnp.arange(s)

    def blk(args):
        qb, pb = args
        return stick_breaking(qb, pb, k, v, pos)

    out = lax.map(blk, (to_blocks(q), pos.reshape(s // Q_BLOCK, Q_BLOCK)))
    return from_blocks(out)


def mix_projections(x, prm):
    bsz, t, _ = x.shape
    h = rms_norm(x, prm['norm_mix'])
    au, av, bq, bk, bv, iq, ik, iw, cq, ck, cv, gl = split_columns(h @ prm['w_in'])
    au = jax.nn.gelu(au)
    av = rms_norm(jax.nn.gelu(av), prm['a_vnorm'])
    bq = rms_norm(bq.reshape(bsz, t, B_HEADS, B_HEAD_DIM), prm['b_qnorm'])
    bk = rms_norm(bk.reshape(bsz, t, B_KV_HEADS, B_HEAD_DIM), prm['b_knorm'])
    bv = bv.reshape(bsz, t, B_KV_HEADS, B_HEAD_DIM)
    iq = iq.reshape(bsz, t, IDX_HEADS, IDX_DIM)
    iw = iw * (IDX_HEADS ** -0.5)
    cq = cq.reshape(bsz, t, C_HEADS, C_HEAD_DIM)
    ck = ck.reshape(bsz, t, C_HEADS, C_HEAD_DIM)
    cv = cv.reshape(bsz, t, C_HEADS, C_HEAD_DIM)
    gates = jax.nn.sigmoid(gl + prm['gate_bias']).reshape(bsz, t, N_BRANCH, D_MODEL)
    return au, av, bq, bk, bv, iq, ik, iw, cq, ck, cv, gates


def run_layer(x, p, prm, past):
    au, av, bq, bk, bv, iq, ik, iw, cq, ck, cv, gates = mix_projections(x, prm)
    oa = gmlp_spatial(au, av, prm['a_ws'], prm['a_bias'])
    if past is None:
        ob = dsa_prompt(bq, iq, iw, bk, bv, ik)
        oc = stick_prompt(cq, ck, cv)
    else:
        pbk, pbv, pik, pck, pcv = past
        t = x.shape[1]
        n_past = pbk.shape[1]
        n_keys = n_past + t
        k_pos = jnp.arange(n_keys)
        q_pos = n_past + jnp.arange(t)
        ob = dsa_attend(bq, iq, iw, q_pos,
                        jnp.concatenate([pbk, bk], axis=1), jnp.concatenate([pbv, bv], axis=1),
                        jnp.concatenate([pik, ik], axis=1), k_pos, min(B_TOPK_MAX, n_keys // 4))
        oc = stick_breaking(cq, q_pos, jnp.concatenate([pck, ck], axis=1),
                            jnp.concatenate([pcv, cv], axis=1), k_pos)
    merged = (gates[:, :, 0] * (oa @ prm['w_br_a']) + gates[:, :, 1] * (ob @ prm['w_br_b'])
              + gates[:, :, 2] * (oc @ prm['w_br_c']))
    x = x + merged @ prm['w_out']
    hf = rms_norm(x, prm['norm_ffn'])
    g, up = jnp.split(hf @ prm['w_ffn_in'], 2, axis=-1)
    x = x + (jax.nn.silu(g) * up) @ prm['w_ffn_out']
    ple_gate = jax.nn.sigmoid(rms_norm(x, prm['norm_ple']) @ prm['w_ple_gate'])
    x = x + ple_gate * (p @ prm['w_ple_proj'])
    return x, (bk, bv, ik, ck, cv, av)


def stack_layers(states, i):
    return jnp.stack([s[i] for s in states])


def setup_inputs(seed: int = 0) -> dict:
    key = jax.random.key(seed)
    ks = jax.random.split(key, 32)

    def nrm(k, shape, scale=1.0):
        return jax.random.normal(k, shape, jnp.float32) * scale

    def gain(k, shape):
        return 1.0 + nrm(k, shape, 0.01)

    return {
        'x_prompt': nrm(ks[0], (BATCH, SEQ, D_MODEL)),
        'x_sample': nrm(ks[1], (DEC_BATCH, DEC_SEQ, D_MODEL)),
        'cache_b_k': nrm(ks[2], (DEPTH, DEC_BATCH, PAST_LEN, B_KV_HEADS, B_HEAD_DIM)),
        'cache_b_v': nrm(ks[3], (DEPTH, DEC_BATCH, PAST_LEN, B_KV_HEADS, B_HEAD_DIM)),
        'cache_b_kidx': nrm(ks[4], (DEPTH, DEC_BATCH, PAST_LEN, IDX_DIM)),
        'cache_c_k': nrm(ks[5], (DEPTH, DEC_BATCH, PAST_LEN, C_HEADS, C_HEAD_DIM)),
        'cache_c_v': nrm(ks[6], (DEPTH, DEC_BATCH, PAST_LEN, C_HEADS, C_HEAD_DIM)),
        'p_prompt': nrm(ks[7], (DEPTH, BATCH, SEQ, PLE_DIM)),
        'p_sample': nrm(ks[8], (DEPTH, DEC_BATCH, DEC_SEQ, PLE_DIM)),
        'norm_mix': gain(ks[9], (DEPTH, D_MODEL)),
        'w_in': nrm(ks[10], (DEPTH, D_MODEL, IN_COLS), D_MODEL ** -0.5),
        'gate_bias': nrm(ks[11], (DEPTH, N_BRANCH * D_MODEL), 0.01),
        'a_vnorm': gain(ks[12], (DEPTH, A_HALF)),
        'a_ws': nrm(ks[13], (DEPTH, A_GROUPS, A_CHUNK, A_CHUNK), A_CHUNK ** -0.5),
        'a_bias': 1.0 + nrm(ks[14], (DEPTH, A_GROUPS, A_CHUNK), 0.1),
        'b_qnorm': gain(ks[15], (DEPTH, B_HEAD_DIM)),
        'b_knorm': gain(ks[16], (DEPTH, B_HEAD_DIM)),
        'w_br_a': nrm(ks[17], (DEPTH, A_HALF, D_MODEL), A_HALF ** -0.5),
        'w_br_b': nrm(ks[18], (DEPTH, B_HEADS * B_HEAD_DIM, D_MODEL), (B_HEADS * B_HEAD_DIM) ** -0.5),
        'w_br_c': nrm(ks[19], (DEPTH, C_HEADS * C_HEAD_DIM, D_MODEL), (C_HEADS * C_HEAD_DIM) ** -0.5),
        'w_out': nrm(ks[20], (DEPTH, D_MODEL, D_MODEL), D_MODEL ** -0.5),
        'norm_ffn': gain(ks[21], (DEPTH, D_MODEL)),
        'w_ffn_in': nrm(ks[22], (DEPTH, D_MODEL, 2 * D_FF), D_MODEL ** -0.5),
        'w_ffn_out': nrm(ks[23], (DEPTH, D_FF, D_MODEL), D_FF ** -0.5),
        'norm_ple': gain(ks[24], (DEPTH, D_MODEL)),
        'w_ple_gate': nrm(ks[25], (DEPTH, D_MODEL, D_MODEL), D_MODEL ** -0.5),
        'w_ple_proj': nrm(ks[26], (DEPTH, PLE_DIM, D_MODEL), PLE_DIM ** -0.5),
    }


def reference(x_prompt, x_sample, cache_b_k, cache_b_v, cache_b_kidx, cache_c_k, cache_c_v,
              p_prompt, p_sample, norm_mix, w_in, gate_bias, a_vnorm, a_ws, a_bias, b_qnorm, b_knorm,
              w_br_a, w_br_b, w_br_c, w_out, norm_ffn, w_ffn_in, w_ffn_out, norm_ple, w_ple_gate,
              w_ple_proj):
    yp, ys = x_prompt, x_sample
    new_p, new_s = [], []
    for l in range(DEPTH):
        prm = {
            'norm_mix': norm_mix[l], 'w_in': w_in[l], 'gate_bias': gate_bias[l],
            'a_vnorm': a_vnorm[l], 'a_ws': a_ws[l], 'a_bias': a_bias[l],
            'b_qnorm': b_qnorm[l], 'b_knorm': b_knorm[l],
            'w_br_a': w_br_a[l], 'w_br_b': w_br_b[l], 'w_br_c': w_br_c[l], 'w_out': w_out[l],
            'norm_ffn': norm_ffn[l], 'w_ffn_in': w_ffn_in[l], 'w_ffn_out': w_ffn_out[l],
            'norm_ple': norm_ple[l], 'w_ple_gate': w_ple_gate[l], 'w_ple_proj': w_ple_proj[l],
        }
        yp, st_p = run_layer(yp, p_prompt[l], prm, None)
        ys, st_s = run_layer(ys, p_sample[l], prm,
                             (cache_b_k[l], cache_b_v[l], cache_b_kidx[l], cache_c_k[l], cache_c_v[l]))
        new_p.append(st_p)
        new_s.append(st_s)
    return (yp, ys,
            stack_layers(new_p, 0), stack_layers(new_p, 1), stack_layers(new_p, 2),
            stack_layers(new_p, 3), stack_layers(new_p, 4),
            stack_layers(new_s, 0), stack_layers(new_s, 1), stack_layers(new_s, 2),
            stack_layers(new_s, 3), stack_layers(new_s, 4), stack_layers(new_s, 5))
```

```python
import functools

import jax
import jax.numpy as jnp
from jax import lax
from jax.experimental import pallas as pl
from jax.experimental.pallas import tpu as pltpu

F32 = jnp.float32
BF16 = jnp.bfloat16
I32 = jnp.int32

D_MODEL = 1024
CHUNK = 64
EPS = 1e-6
A_CHUNK = 128
A_GROUPS = 4
A_GROUP_DIM = 128
A_HALF = 512
B_HEADS = 8
B_KV_HEADS = 2
B_HEAD_DIM = 64
B_TOPK_MAX = 256
IDX_HEADS = 8
IDX_DIM = 32
C_HEADS = 8
C_HEAD_DIM = 64
N_BRANCH = 3
D_FF = 2816
PLE_DIM = 256

LANES = 128
VMEM_LIMIT = 56 * 1024 * 1024

_C_AU, _C_AV, _C_BQ, _C_BK, _C_BV, _C_IQ = 0, 512, 1024, 1536, 1664, 1792
_C_IKT, _C_KD, _C_VD, _C_SM, _C_CQ, _C_CK, _C_CV, _C_END = 2048, 2304, 2560, 2816, 2944, 3456, 3968, 4480

INT_MIN = -2147483648
NEG_INF_KEY = (-8388608) ^ 0x7FFFFFFF
SB_SKIP = -104.0


def _const_spec(shape):
    zeros = (0,) * len(shape)
    return pl.BlockSpec(shape, lambda *_: zeros, pipeline_mode=pl.Buffered(1))


def _rms(x, gain):
    return x * lax.rsqrt(jnp.mean(x * x, axis=-1, keepdims=True) + EPS) * gain


def _dot(a, b):
    return jnp.dot(a, b, preferred_element_type=F32)


def _dot_t(a, b):
    return lax.dot_general(a, b, (((1,), (1,)), ((), ())), preferred_element_type=F32)


def _split_dot(x, m_ref):
    hi = x.astype(BF16)
    lo = (x - hi.astype(F32)).astype(BF16)
    m = m_ref[...]
    return _dot(hi, m) + _dot(lo, m)


def _group_rms(z, g_ref, gain):
    return z * lax.rsqrt(_split_dot(z * z, g_ref) + EPS) * gain


def _proj_kernel(x_ref, nmix_ref, w_ref, avn_ref, qn_ref, kn_ref, kdn_ref, g512_ref, g128_ref, g256_ref,
                 ma_ref, abias_ref,
                 oa_ref, q2_ref, bk_ref, bv_ref, iq_ref, ikt_ref, kk_ref, vv_ref, ik_ref, iw_ref,
                 cq_ref, ck_ref, cv_ref, ckb_ref, cvb_ref, av_ref, *, tm, ca, sample):
    hb = _rms(x_ref[...], nmix_ref[...]).astype(BF16)

    def proj(c0, c1):
        return _dot(hb, w_ref[:, c0:c1])

    q2_ref[...] = (_group_rms(proj(_C_BQ, _C_BK), g512_ref, qn_ref[...]) * (B_HEAD_DIM ** -0.5)).astype(BF16)
    bk_ref[...] = _group_rms(proj(_C_BK, _C_BV), g128_ref, kn_ref[...])
    bv_ref[...] = proj(_C_BV, _C_IQ)
    iq_ref[...] = proj(_C_IQ, _C_IKT).astype(BF16)
    ikt_ref[...] = proj(_C_IKT, _C_KD).astype(BF16)
    kk_ref[...] = _group_rms(proj(_C_KD, _C_VD), g256_ref, kdn_ref[...]).astype(BF16)
    vv_ref[...] = proj(_C_VD, _C_SM).astype(BF16)
    small = proj(_C_SM, _C_CQ)
    ik_ref[...] = small[:, 0:IDX_DIM]
    iw_ref[...] = small[:, IDX_DIM:IDX_DIM + IDX_HEADS] * (1.0 / 16.0)
    cq_ref[...] = (proj(_C_CQ, _C_CK) * (C_HEAD_DIM ** -0.5)).astype(BF16)
    ck = proj(_C_CK, _C_CV)
    ck_ref[...] = ck
    ckb_ref[...] = ck.astype(BF16)
    cv = proj(_C_CV, _C_END)
    cv_ref[...] = cv
    cvb_ref[...] = cv.astype(BF16)

    au = jax.nn.gelu(proj(_C_AU, _C_AV))
    av = _rms(jax.nn.gelu(proj(_C_AV, _C_BQ)), avn_ref[...])
    av_ref[...] = av
    avb = av.astype(BF16)
    row = lax.broadcasted_iota(I32, (ca, ca), 0)
    col = lax.broadcasted_iota(I32, (ca, ca), 1)
    if sample:
        vis = (col // 16) == (row // 16)
    else:
        vis = (col // CHUNK) <= (row // CHUNK)
    for g in range(A_GROUPS):
        wm = jnp.where(vis, ma_ref[g], 0.0).astype(BF16)
        cs = slice(g * A_GROUP_DIM, (g + 1) * A_GROUP_DIM)
        for c in range(tm // ca):
            rs = slice(c * ca, (c + 1) * ca)
            mixed = _dot(wm, avb[rs, cs]) + abias_ref[:, cs]
            oa_ref[rs, cs] = (au[rs, cs] * mixed).astype(BF16)


def _proj_call(x, lw, *, tm, ca, sample):
    n = x.shape[0]
    grid = (n // tm,)

    def tile(c):
        return pl.BlockSpec((tm, c), lambda i: (i, 0))

    out_cols = [(A_HALF, BF16), (512, BF16), (128, F32), (128, F32), (256, BF16), (256, BF16), (256, BF16),
                (256, BF16), (IDX_DIM, F32), (IDX_HEADS, F32), (512, BF16), (512, F32), (512, F32),
                (512, BF16), (512, BF16), (A_HALF, F32)]
    ma = lw['ma_sample'] if sample else lw['ma_prompt']
    abias = lw['abias_sample'] if sample else lw['abias_prompt']
    consts = [lw['norm_mix'], lw['w1'], lw['a_vnorm'], lw['qn'], lw['kn'], lw['kdn'],
              lw['g512'], lw['g128'], lw['g256'], ma, abias]
    return pl.pallas_call(
        functools.partial(_proj_kernel, tm=tm, ca=ca, sample=sample),
        grid=grid,
        in_specs=[tile(D_MODEL)] + [_const_spec(c.shape) for c in consts],
        out_specs=[tile(c) for c, _ in out_cols],
        out_shape=[jax.ShapeDtypeStruct((n, c), dt) for c, dt in out_cols],
        compiler_params=pltpu.CompilerParams(dimension_semantics=("arbitrary",), vmem_limit_bytes=VMEM_LIMIT),
        name="proj_sample" if sample else "proj_prompt",
    )(x, *consts)


def _dsa_kernel(q_ref, iq_ref, iw_ref, ikt_ref, kk_ref, vv_ref, o_ref, keys_ref, j_ref,
                *, qb, kb_size, q_pos_static, n_valid, nkb_static, topk):
    if q_pos_static is None:
        q0 = pl.program_id(1) * qb
        nkb = (q0 + qb + kb_size - 1) // kb_size
    else:
        q0 = q_pos_static
        nkb = nkb_static

    rows = lax.broadcasted_iota(I32, (qb, kb_size), 0)
    lanes = lax.broadcasted_iota(I32, (qb, kb_size), 1)
    qchunk = (q0 + rows) // CHUNK

    iq = iq_ref[...]
    iw = iw_ref[...]
    lane_iq = lax.broadcasted_iota(I32, iq.shape, 1)
    iq_heads = [jnp.where(lane_iq // IDX_DIM == h, iq, jnp.zeros_like(iq)) for h in range(IDX_HEADS)]
    iw_heads = [iw[:, h:h + 1] for h in range(IDX_HEADS)]

    def score_body(kb, carry):
        off = pl.multiple_of(kb * kb_size, kb_size)
        ik_blk = ikt_ref[pl.ds(off, kb_size), :]
        acc = jnp.zeros((qb, kb_size), F32)
        for h in range(IDX_HEADS):
            acc = acc + jnp.maximum(_dot_t(iq_heads[h], ik_blk), 0.0) * iw_heads[h]
        colv = off + lanes
        adm = (colv // CHUNK <= qchunk) & (colv < n_valid)
        bits = pltpu.bitcast(jnp.where(adm, acc, -jnp.inf), I32)
        keys_ref[kb] = jnp.where(bits < 0, bits ^ 0x7FFFFFFF, bits)
        return carry

    lax.fori_loop(0, nkb, score_body, 0)

    def count(pred):
        def body(kb, cnt):
            ind = jnp.where(pred(keys_ref[kb], kb), 1.0, 0.0)
            part = ind[:, 0:LANES]
            for s in range(1, kb_size // LANES):
                part = part + ind[:, s * LANES:(s + 1) * LANES]
            return cnt + part
        cnt = lax.fori_loop(0, nkb, body, jnp.zeros((qb, LANES), F32))
        return jnp.sum(cnt, axis=1, keepdims=True)

    def bit_body(t, ans):
        cand_u = ans | lax.shift_left(jnp.int32(1), 31 - t)
        cand = cand_u ^ INT_MIN
        c = count(lambda blk, kb: blk >= cand)
        return jnp.where(c >= float(topk), cand_u, ans)

    ans = lax.fori_loop(0, 32, bit_body, jnp.zeros((qb, 1), I32))
    thr = ans ^ INT_MIN

    c_gt = count(lambda blk, kb: blk > thr)
    c_eq = count(lambda blk, kb: blk == thr)
    need = float(topk) - c_gt
    tie_rows = jnp.where((c_eq > need) & (thr > NEG_INF_KEY), 1.0, 0.0)
    j_ref[...] = jnp.full((qb, 1), 2 ** 30, I32)

    @pl.when(jnp.max(tie_rows) > 0.0)
    def _():
        def idx_bit_body(t, jans):
            cand = jans | lax.shift_left(jnp.int32(1), 13 - t)
            c = count(lambda blk, kb: (blk == thr) & (kb * kb_size + lanes < cand))
            return jnp.where(c < need, cand, jans)
        j_ref[...] = lax.fori_loop(0, 14, idx_bit_body, jnp.zeros((qb, 1), I32))

    jmax = j_ref[...]

    q = q_ref[...]
    lane_q = lax.broadcasted_iota(I32, (qb, LANES), 1)
    q_heads = []
    for h in range(B_HEADS):
        pair = q[:, (h // 2) * LANES:(h // 2 + 1) * LANES]
        q_heads.append(jnp.where(lane_q // B_HEAD_DIM == h % 2, pair, jnp.zeros_like(pair)))

    def att_body(kb, carry):
        ms, ls, accs = carry
        off = pl.multiple_of(kb * kb_size, kb_size)
        key_blk = keys_ref[kb]
        colv = off + lanes
        sel = ((key_blk > thr) | ((key_blk == thr) & (colv <= jmax))) & (key_blk > NEG_INF_KEY)
        bias = jnp.where(sel, 0.0, -jnp.inf)
        kk = kk_ref[pl.ds(off, kb_size), :]
        vv = vv_ref[pl.ds(off, kb_size), :]
        new_m, new_l, new_a = [], [], []
        for h in range(B_HEADS):
            g = h // (B_HEADS // B_KV_HEADS)
            s = _dot_t(q_heads[h], kk[:, g * LANES:(g + 1) * LANES]) + bias
            m_new = jnp.maximum(ms[h], jnp.max(s, axis=1, keepdims=True))
            m_safe = jnp.where(m_new == -jnp.inf, 0.0, m_new)
            p = jnp.exp(s - m_safe)
            alpha = jnp.exp(ms[h] - m_safe)
            new_m.append(m_new)
            new_l.append(alpha * ls[h] + jnp.sum(p, axis=1, keepdims=True))
            new_a.append(alpha * accs[h] + _dot(p.astype(BF16), vv[:, g * LANES:(g + 1) * LANES]))
        return tuple(new_m), tuple(new_l), tuple(new_a)

    init = (tuple(jnp.full((qb, 1), -jnp.inf, F32) for _ in range(B_HEADS)),
            tuple(jnp.zeros((qb, 1), F32) for _ in range(B_HEADS)),
            tuple(jnp.zeros((qb, LANES), F32) for _ in range(B_HEADS)))
    ms, ls, accs = lax.fori_loop(0, nkb, att_body, init)
    for j in range(B_HEADS // 2):
        lo = accs[2 * j] / ls[2 * j]
        hi = accs[2 * j + 1] / ls[2 * j + 1]
        o_ref[:, j * LANES:(j + 1) * LANES] = jnp.where(lane_q < B_HEAD_DIM, lo, hi).astype(BF16)


def _dsa_call(q2, iqb, iws, ikt, kk, vv, *, qb, kb_size, q_pos_static, n_valid, name):
    b, sq, _ = q2.shape
    lp = ikt.shape[1]
    topk = min(B_TOPK_MAX, n_valid // 4)

    def qspec(c):
        return pl.BlockSpec((None, qb, c), lambda bi, i: (bi, i, 0))

    kspec = pl.BlockSpec((None, lp, 256), lambda bi, i: (bi, 0, 0))
    return pl.pallas_call(
        functools.partial(_dsa_kernel, qb=qb, kb_size=kb_size, q_pos_static=q_pos_static, n_valid=n_valid,
                          nkb_static=lp // kb_size, topk=topk),
        grid=(b, sq // qb),
        in_specs=[qspec(512), qspec(256), qspec(IDX_HEADS), kspec, kspec, kspec],
        out_specs=qspec(512),
        out_shape=jax.ShapeDtypeStruct((b, sq, 512), BF16),
        scratch_shapes=[pltpu.VMEM((lp // kb_size, qb, kb_size), I32), pltpu.VMEM((qb, 1), I32)],
        compiler_params=pltpu.CompilerParams(dimension_semantics=("arbitrary", "arbitrary"),
                                             vmem_limit_bytes=VMEM_LIMIT),
        name=name,
    )(q2, iqb, iws, ikt, kk, vv)


def _sb_masked_q(q):
    qb = q.shape[0]
    lane = lax.broadcasted_iota(I32, (qb, LANES), 1)
    out = []
    for h in range(C_HEADS):
        pair = q[:, (h // 2) * LANES:(h // 2 + 1) * LANES]
        out.append(jnp.where(lane // C_HEAD_DIM == h % 2, pair, jnp.zeros_like(pair)))
    return out


def _sb_block(q_heads, k_blk, v_blk, tri, carries, accs, vis):
    new_c, new_a = [], []
    for h in range(C_HEADS):
        ps = slice((h // 2) * LANES, (h // 2 + 1) * LANES)
        z = _dot_t(q_heads[h], k_blk[:, ps])
        sp = jnp.maximum(z, 0.0) + jnp.log1p(jnp.exp(-jnp.abs(z)))
        log_stay = -sp
        if vis is not None:
            log_stay = jnp.where(vis, log_stay, 0.0)
        hi = log_stay.astype(BF16)
        lo = (log_stay - hi.astype(F32)).astype(BF16)
        after = _dot(hi, tri) + _dot(lo, tri) + carries[h]
        w = jnp.exp((z - sp) + after)
        if vis is not None:
            w = jnp.where(vis, w, 0.0)
        new_a.append(accs[h] + _dot(w.astype(BF16), v_blk[:, ps]))
        new_c.append(carries[h] + jnp.sum(log_stay, axis=1, keepdims=True))
    return tuple(new_c), tuple(new_a)


def _sb_max_carry(carries):
    m = carries[0]
    for c in carries[1:]:
        m = jnp.maximum(m, c)
    return jnp.max(m)


def _sb_finish(o_ref, accs):
    qb = accs[0].shape[0]
    lane = lax.broadcasted_iota(I32, (qb, LANES), 1)
    for j in range(C_HEADS // 2):
        o_ref[:, j * LANES:(j + 1) * LANES] = jnp.where(lane < C_HEAD_DIM, accs[2 * j], accs[2 * j + 1]).astype(BF16)


def _sb_tri():
    r = lax.broadcasted_iota(I32, (LANES, LANES), 0)
    c = lax.broadcasted_iota(I32, (LANES, LANES), 1)
    return jnp.where(r > c, 1.0, 0.0).astype(BF16)


def _sb_init(qb):
    return (tuple(jnp.zeros((qb, 1), F32) for _ in range(C_HEADS)),
            tuple(jnp.zeros((qb, LANES), F32) for _ in range(C_HEADS)))


def _sb_prompt_kernel(q_ref, k_ref, v_ref, o_ref, *, qb):
    i = pl.program_id(1)
    q_heads = _sb_masked_q(q_ref[...])
    tri = _sb_tri()
    rows = lax.broadcasted_iota(I32, (qb, LANES), 0)
    lane = lax.broadcasted_iota(I32, (qb, LANES), 1)
    off = pl.multiple_of(i * qb, qb)
    carries, accs = _sb_init(qb)
    carries, accs = _sb_block(q_heads, k_ref[pl.ds(off, qb), :], v_ref[pl.ds(off, qb), :], tri,
                              carries, accs, lane < rows)

    def cond(state):
        kb, mx, _, _ = state
        return (kb >= 0) & (mx > SB_SKIP)

    def body(state):
        kb, _, carries, accs = state
        o = pl.multiple_of(kb * qb, qb)
        carries, accs = _sb_block(q_heads, k_ref[pl.ds(o, qb), :], v_ref[pl.ds(o, qb), :], tri,
                                  carries, accs, None)
        return kb - 1, _sb_max_carry(carries), carries, accs

    _, _, _, accs = lax.while_loop(cond, body, (i - 1, _sb_max_carry(carries), carries, accs))
    _sb_finish(o_ref, accs)


def _sb_prompt_call(cq, ckb, cvb, *, qb):
    b, s, _ = cq.shape
    qspec = pl.BlockSpec((None, qb, 512), lambda bi, i: (bi, i, 0))
    kspec = pl.BlockSpec((None, s, 512), lambda bi, i: (bi, 0, 0))
    return pl.pallas_call(
        functools.partial(_sb_prompt_kernel, qb=qb),
        grid=(b, s // qb),
        in_specs=[qspec, kspec, kspec],
        out_specs=qspec,
        out_shape=jax.ShapeDtypeStruct((b, s, 512), BF16),
        compiler_params=pltpu.CompilerParams(dimension_semantics=("arbitrary", "arbitrary"),
                                             vmem_limit_bytes=VMEM_LIMIT),
        name="stick_prompt",
    )(cq, ckb, cvb)


def _sb_sample_kernel(q_ref, kn_ref, vn_ref, kp_ref, vp_ref, o_ref, *, qb, n_past):
    q_heads = _sb_masked_q(q_ref[...])
    tri = _sb_tri()
    rows = lax.broadcasted_iota(I32, (qb, LANES), 0)
    lane = lax.broadcasted_iota(I32, (qb, LANES), 1)
    carries, accs = _sb_init(qb)
    carries, accs = _sb_block(q_heads, kn_ref[...], vn_ref[...], tri, carries, accs, lane < rows)

    def cond(state):
        kb, mx, _, _ = state
        return (kb >= 0) & (mx > SB_SKIP)

    def body(state):
        kb, _, carries, accs = state
        o = pl.multiple_of(kb * LANES, LANES)
        carries, accs = _sb_block(q_heads, kp_ref[pl.ds(o, LANES), :].astype(BF16),
                                  vp_ref[pl.ds(o, LANES), :].astype(BF16), tri, carries, accs, None)
        return kb - 1, _sb_max_carry(carries), carries, accs

    _, _, _, accs = lax.while_loop(cond, body, (n_past // LANES - 1, _sb_max_carry(carries), carries, accs))
    _sb_finish(o_ref, accs)


def _sb_sample_call(cq, kn, vn, kp, vp):
    b, t, _ = cq.shape
    n_past = kp.shape[1]
    qspec = pl.BlockSpec((None, t, 512), lambda bi: (bi, 0, 0))
    nspec = pl.BlockSpec((None, LANES, 512), lambda bi: (bi, 0, 0))
    pspec = pl.BlockSpec((None, n_past, 512), lambda bi: (bi, 0, 0))
    return pl.pallas_call(
        functools.partial(_sb_sample_kernel, qb=t, n_past=n_past),
        grid=(b,),
        in_specs=[qspec, nspec, nspec, pspec, pspec],
        out_specs=qspec,
        out_shape=jax.ShapeDtypeStruct((b, t, 512), BF16),
        compiler_params=pltpu.CompilerParams(dimension_semantics=("arbitrary",), vmem_limit_bytes=VMEM_LIMIT),
        name="stick_sample",
    )(cq, kn, vn, kp, vp)


def _merge_kernel(x_ref, nmix_ref, wg_ref, gb_ref, oa_ref, ob_ref, oc_ref, wa_ref, wb_ref, wc_ref, wo_ref, o_ref):
    x = x_ref[...]
    hb = _rms(x, nmix_ref[...]).astype(BF16)
    merged = None
    for idx, (br_ref, w_ref) in enumerate(((oa_ref, wa_ref), (ob_ref, wb_ref), (oc_ref, wc_ref))):
        cs = slice(idx * D_MODEL, (idx + 1) * D_MODEL)
        gate = jax.nn.sigmoid(_dot(hb, wg_ref[:, cs]) + gb_ref[:, cs])
        term = gate * _dot(br_ref[...], w_ref[...])
        merged = term if merged is None else merged + term
    o_ref[...] = x + _dot(merged.astype(BF16), wo_ref[...])


def _merge_call(x, oa, ob, oc, lw, *, tm, name):
    n = x.shape[0]

    def tile(c):
        return pl.BlockSpec((tm, c), lambda i: (i, 0))

    consts_a = [lw['norm_mix'], lw['w_gate'], lw['gate_bias']]
    consts_b = [lw['w_br_a'], lw['w_br_b'], lw['w_br_c'], lw['w_out']]
    return pl.pallas_call(
        _merge_kernel,
        grid=(n // tm,),
        in_specs=([tile(D_MODEL)] + [_const_spec(c.shape) for c in consts_a] + [tile(512)] * 3
                  + [_const_spec(c.shape) for c in consts_b]),
        out_specs=tile(D_MODEL),
        out_shape=jax.ShapeDtypeStruct((n, D_MODEL), F32),
        compiler_params=pltpu.CompilerParams(dimension_semantics=("arbitrary",), vmem_limit_bytes=VMEM_LIMIT),
        name=name,
    )(x, *consts_a, oa, ob, oc, *consts_b)


def _ffn_kernel(x_ref, p_ref, nffn_ref, wfi_ref, wfo_ref, nple_ref, wpg_ref, wpp_ref, o_ref):
    x = x_ref[...]
    hf = _rms(x, nffn_ref[...]).astype(BF16)
    g = _dot(hf, wfi_ref[:, 0:D_FF])
    up = _dot(hf, wfi_ref[:, D_FF:2 * D_FF])
    x = x + _dot((jax.nn.silu(g) * up).astype(BF16), wfo_ref[...])
    hp = _rms(x, nple_ref[...]).astype(BF16)
    gate = jax.nn.sigmoid(_dot(hp, wpg_ref[...]))
    o_ref[...] = x + gate * _dot(p_ref[...].astype(BF16), wpp_ref[...])


def _ffn_call(x, p, lw, *, tm, name):
    n = x.shape[0]
    consts = [lw['norm_ffn'], lw['w_ffn_in'], lw['w_ffn_out'], lw['norm_ple'], lw['w_ple_gate'], lw['w_ple_proj']]
    return pl.pallas_call(
        _ffn_kernel,
        grid=(n // tm,),
        in_specs=[pl.BlockSpec((tm, D_MODEL), lambda i: (i, 0)), pl.BlockSpec((tm, PLE_DIM), lambda i: (i, 0))]
        + [_const_spec(c.shape) for c in consts],
        out_specs=pl.BlockSpec((tm, D_MODEL), lambda i: (i, 0)),
        out_shape=jax.ShapeDtypeStruct((n, D_MODEL), F32),
        compiler_params=pltpu.CompilerParams(dimension_semantics=("arbitrary",), vmem_limit_bytes=VMEM_LIMIT),
        name=name,
    )(x, p, *consts)


def _block_diag_mean(n):
    r = jnp.arange(n)
    return jnp.where((r[:, None] // B_HEAD_DIM) == (r[None, :] // B_HEAD_DIM), 1.0 / B_HEAD_DIM, 0.0).astype(BF16)


def _prep_layer(l, t_sample, w):
    w_in = w['w_in'][l]
    bk = w_in[:, 1536:1664]
    bv = w_in[:, 1664:1792]
    ik = w_in[:, 2048:2080]
    dup = lambda a: jnp.concatenate([a[:, :64], a[:, :64], a[:, 64:], a[:, 64:]], axis=1)
    small = jnp.pad(w_in[:, 2048:2088], ((0, 0), (0, LANES - IDX_DIM - IDX_HEADS)))
    w1 = jnp.concatenate([w_in[:, :2048], jnp.tile(ik, (1, IDX_HEADS)), dup(bk), dup(bv), small,
                          w_in[:, 2088:3624]], axis=1).astype(BF16)
    reps = A_CHUNK // t_sample
    a_ws = w['a_ws'][l]
    a_bias_full = jnp.repeat(w['a_bias'][l].T, A_GROUP_DIM, axis=1)
    row = lambda a: a.reshape(1, -1)
    return {
        'norm_mix': row(w['norm_mix'][l]), 'w1': w1, 'a_vnorm': row(w['a_vnorm'][l]),
        'qn': row(jnp.tile(w['b_qnorm'][l], B_HEADS)), 'kn': row(jnp.tile(w['b_knorm'][l], B_KV_HEADS)),
        'kdn': row(jnp.tile(w['b_knorm'][l], 2 * B_KV_HEADS)),
        'g512': _block_diag_mean(512), 'g128': _block_diag_mean(128), 'g256': _block_diag_mean(256),
        'ma_prompt': a_ws, 'abias_prompt': a_bias_full,
        'ma_sample': jnp.tile(a_ws[:, :t_sample, :t_sample], (1, 16, 16)),
        'abias_sample': jnp.tile(a_bias_full[:t_sample], (16, 1)),
        'w_gate': w_in[:, 3624:].astype(BF16), 'gate_bias': row(w['gate_bias'][l]),
        'w_br_a': w['w_br_a'][l].astype(BF16), 'w_br_b': w['w_br_b'][l].astype(BF16),
        'w_br_c': w['w_br_c'][l].astype(BF16), 'w_out': w['w_out'][l].astype(BF16),
        'norm_ffn': row(w['norm_ffn'][l]), 'w_ffn_in': w['w_ffn_in'][l].astype(BF16),
        'w_ffn_out': w['w_ffn_out'][l].astype(BF16), 'norm_ple': row(w['norm_ple'][l]),
        'w_ple_gate': w['w_ple_gate'][l].astype(BF16), 'w_ple_proj': w['w_ple_proj'][l].astype(BF16),
    }


_PROJ_NAMES = ('oa', 'q2', 'bk', 'bv', 'iq', 'ikt', 'kk', 'vv', 'ik', 'iw', 'cq', 'ck', 'cv', 'ckb', 'cvb', 'av')


def _prompt_layer(x, p, lw, b, s):
    pr = dict(zip(_PROJ_NAMES, _proj_call(x, lw, tm=512, ca=A_CHUNK, sample=False)))
    r3 = lambda a: a.reshape(b, s, a.shape[-1])
    ob = _dsa_call(r3(pr['q2']), r3(pr['iq']), r3(pr['iw']), r3(pr['ikt']), r3(pr['kk']), r3(pr['vv']),
                   qb=128, kb_size=512, q_pos_static=None, n_valid=s, name="dsa_prompt")
    oc = _sb_prompt_call(r3(pr['cq']), r3(pr['ckb']), r3(pr['cvb']), qb=128)
    x = _merge_call(x, pr['oa'], ob.reshape(b * s, 512), oc.reshape(b * s, 512), lw, tm=256, name="merge_prompt")
    x = _ffn_call(x, p, lw, tm=256, name="ffn_prompt")
    return x, pr


def _sample_layer(x, p, lw, b, t, past):
    pbk, pbv, pik, pck, pcv = past
    n_past = pbk.shape[1]
    pr = dict(zip(_PROJ_NAMES, _proj_call(x, lw, tm=b * t, ca=b * t, sample=True)))
    r3 = lambda a: a.reshape(b, t, a.shape[-1])
    n_keys = n_past + t
    kb_size = 512
    pad = (-n_keys) % kb_size

    def cat(past_part, new_part):
        return jnp.pad(jnp.concatenate([past_part, new_part], axis=1), ((0, 0), (0, pad), (0, 0)))

    dup = lambda a: jnp.repeat(a, 2, axis=2).reshape(b, n_past, 256).astype(BF16)
    ikt = cat(jnp.tile(pik.astype(BF16), (1, 1, IDX_HEADS)), r3(pr['ikt']))
    kk = cat(dup(pbk), r3(pr['kk']))
    vv = cat(dup(pbv), r3(pr['vv']))
    ob = _dsa_call(r3(pr['q2']), r3(pr['iq']), r3(pr['iw']), ikt, kk, vv, qb=t, kb_size=kb_size,
                   q_pos_static=n_past, n_valid=n_keys, name="dsa_sample")
    padn = lambda a: jnp.pad(r3(a), ((0, 0), (0, LANES - t), (0, 0)))
    oc = _sb_sample_call(r3(pr['cq']), padn(pr['ckb']), padn(pr['cvb']),
                         pck.reshape(b, n_past, 512), pcv.reshape(b, n_past, 512))
    x = _merge_call(x, pr['oa'], ob.reshape(b * t, 512), oc.reshape(b * t, 512), lw, tm=b * t, name="merge_sample")
    x = _ffn_call(x, p, lw, tm=b * t, name="ffn_sample")
    return x, pr


def kernel(x_prompt, x_sample, cache_b_k, cache_b_v, cache_b_kidx, cache_c_k, cache_c_v, p_prompt, p_sample,
           norm_mix, w_in, gate_bias, a_vnorm, a_ws, a_bias, b_qnorm, b_knorm, w_br_a, w_br_b, w_br_c, w_out,
           norm_ffn, w_ffn_in, w_ffn_out, norm_ple, w_ple_gate, w_ple_proj):
    weights = dict(norm_mix=norm_mix, w_in=w_in, gate_bias=gate_bias, a_vnorm=a_vnorm, a_ws=a_ws, a_bias=a_bias,
                   b_qnorm=b_qnorm, b_knorm=b_knorm, w_br_a=w_br_a, w_br_b=w_br_b, w_br_c=w_br_c, w_out=w_out,
                   norm_ffn=norm_ffn, w_ffn_in=w_ffn_in, w_ffn_out=w_ffn_out, norm_ple=norm_ple,
                   w_ple_gate=w_ple_gate, w_ple_proj=w_ple_proj)
    depth = w_in.shape[0]
    bp, sp, _ = x_prompt.shape
    bs, ts, _ = x_sample.shape
    yp = x_prompt.reshape(bp * sp, D_MODEL)
    ys = x_sample.reshape(bs * ts, D_MODEL)
    new_p, new_s = [], []
    for l in range(depth):
        lw = _prep_layer(l, ts, weights)
        yp, pr_p = _prompt_layer(yp, p_prompt[l].reshape(bp * sp, PLE_DIM), lw, bp, sp)
        ys, pr_s = _sample_layer(ys, p_sample[l].reshape(bs * ts, PLE_DIM), lw, bs, ts,
                                 (cache_b_k[l], cache_b_v[l], cache_b_kidx[l], cache_c_k[l], cache_c_v[l]))
        new_p.append(pr_p)
        new_s.append(pr_s)

    def stack(states, name, lead, tail):
        return jnp.stack([s[name].reshape(*lead, *tail) for s in states])

    lp, ls = (bp, sp), (bs, ts)
    return (yp.reshape(bp, sp, D_MODEL), ys.reshape(bs, ts, D_MODEL),
            stack(new_p, 'bk', lp, (B_KV_HEADS, B_HEAD_DIM)), stack(new_p, 'bv', lp, (B_KV_HEADS, B_HEAD_DIM)),
            stack(new_p, 'ik', lp, (IDX_DIM,)),
            stack(new_p, 'ck', lp, (C_HEADS, C_HEAD_DIM)), stack(new_p, 'cv', lp, (C_HEADS, C_HEAD_DIM)),
            stack(new_s, 'bk', ls, (B_KV_HEADS, B_HEAD_DIM)), stack(new_s, 'bv', ls, (B_KV_HEADS, B_HEAD_DIM)),
            stack(new_s, 'ik', ls, (IDX_DIM,)),
            stack(new_s, 'ck', ls, (C_HEADS, C_HEAD_DIM)), stack(new_s, 'cv', ls, (C_HEADS, C_HEAD_DIM)),
            stack(new_s, 'av', ls, (A_HALF,)))
```

```python
import functools

import jax
import jax.numpy as jnp
from jax import lax
from jax.experimental import pallas as pl
from jax.experimental.pallas import tpu as pltpu

F32 = jnp.float32
BF16 = jnp.bfloat16
I32 = jnp.int32

D_MODEL = 1024
CHUNK = 64
EPS = 1e-6
A_CHUNK = 128
A_GROUPS = 4
A_GROUP_DIM = 128
A_HALF = 512
B_HEADS = 8
B_KV_HEADS = 2
B_HEAD_DIM = 64
B_TOPK_MAX = 256
IDX_HEADS = 8
IDX_DIM = 32
C_HEADS = 8
C_HEAD_DIM = 64
N_BRANCH = 3
D_FF = 2816
PLE_DIM = 256

LANES = 128
VMEM_LIMIT = 56 * 1024 * 1024

_C_AU, _C_AV, _C_BQ, _C_BK, _C_BV, _C_IQ = 0, 512, 1024, 1536, 1664, 1792
_C_SM, _C_CQ, _C_CK, _C_CV, _C_END = 2048, 2176, 2688, 3200, 3712
DSA_KB = 256

INT_MIN = -2147483648
NEG_INF_KEY = (-8388608) ^ 0x7FFFFFFF
SB_SKIP = -104.0


def _const_spec(shape):
    zeros = (0,) * len(shape)
    return pl.BlockSpec(shape, lambda *_: zeros, pipeline_mode=pl.Buffered(1))


def _rms(x, gain):
    return x * lax.rsqrt(jnp.mean(x * x, axis=-1, keepdims=True) + EPS) * gain


def _dot(a, b):
    return jnp.dot(a, b, preferred_element_type=F32)


def _dot_t(a, b):
    return lax.dot_general(a, b, (((1,), (1,)), ((), ())), preferred_element_type=F32)


def _split_dot(x, m_ref):
    hi = x.astype(BF16)
    lo = (x - hi.astype(F32)).astype(BF16)
    m = m_ref[...]
    return _dot(hi, m) + _dot(lo, m)


def _group_rms(z, g_ref, gain):
    return z * lax.rsqrt(_split_dot(z * z, g_ref) + EPS) * gain


def _proj_kernel(x_ref, nmix_ref, w_ref, wvt_ref, wiwt_ref, avn_ref, qn_ref, kn_ref, g512_ref, g128_ref,
                 ma_ref, abias_ref,
                 oa_ref, qg_ref, bk_ref, bkb_ref, bv_ref, vt_ref, iq_ref, ik_ref, ikb_ref, iwt_ref,
                 cq_ref, ck_ref, cv_ref, ckb_ref, cvb_ref, av_ref, *, tm, ca, sample):
    hb = _rms(x_ref[...], nmix_ref[...]).astype(BF16)

    def proj(c0, c1):
        return _dot(hb, w_ref[:, c0:c1])

    qn = _group_rms(proj(_C_BQ, _C_BK), g512_ref, qn_ref[...]) * (B_HEAD_DIM ** -0.5)
    lane = lax.broadcasted_iota(I32, (tm, LANES), 1)
    for h in range(B_HEADS):
        g = h // (B_HEADS // B_KV_HEADS)
        pair = qn[:, (h // 2) * LANES:(h // 2 + 1) * LANES]
        if h % 2 != g:
            pair = pltpu.roll(pair, B_HEAD_DIM, axis=1)
        qg_ref[:, h * LANES:(h + 1) * LANES] = jnp.where(lane // B_HEAD_DIM == g, pair, 0.0).astype(BF16)
    bk = _group_rms(proj(_C_BK, _C_BV), g128_ref, kn_ref[...])
    bk_ref[...] = bk
    bkb_ref[...] = bk.astype(BF16)
    bv_ref[...] = proj(_C_BV, _C_IQ)
    vt = _dot_t(wvt_ref[...], hb).astype(BF16)
    for c in range(tm // DSA_KB):
        vt_ref[c] = vt[:, c * DSA_KB:(c + 1) * DSA_KB]
    iq_ref[...] = proj(_C_IQ, _C_SM).astype(BF16)
    ik = proj(_C_SM, _C_CQ)[:, 0:IDX_DIM]
    ik_ref[...] = ik
    ikb_ref[...] = ik.astype(BF16)
    iwt_ref[...] = _dot_t(wiwt_ref[...], hb) * (1.0 / 16.0)
    cq_ref[...] = (proj(_C_CQ, _C_CK) * (C_HEAD_DIM ** -0.5)).astype(BF16)
    ck = proj(_C_CK, _C_CV)
    ck_ref[...] = ck
    ckb_ref[...] = ck.astype(BF16)
    cv = proj(_C_CV, _C_END)
    cv_ref[...] = cv
    cvb_ref[...] = cv.astype(BF16)

    au = jax.nn.gelu(proj(_C_AU, _C_AV))
    av = _rms(jax.nn.gelu(proj(_C_AV, _C_BQ)), avn_ref[...])
    av_ref[...] = av
    avb = av.astype(BF16)
    row = lax.broadcasted_iota(I32, (ca, ca), 0)
    col = lax.broadcasted_iota(I32, (ca, ca), 1)
    if sample:
        vis = (col // 16) == (row // 16)
    else:
        vis = (col // CHUNK) <= (row // CHUNK)
    for g in range(A_GROUPS):
        wm = jnp.where(vis, ma_ref[g], 0.0).astype(BF16)
        cs = slice(g * A_GROUP_DIM, (g + 1) * A_GROUP_DIM)
        for c in range(tm // ca):
            rs = slice(c * ca, (c + 1) * ca)
            mixed = _dot(wm, avb[rs, cs]) + abias_ref[:, cs]
            oa_ref[rs, cs] = (au[rs, cs] * mixed).astype(BF16)


_PROJ_NAMES = ('oa', 'qg', 'bk', 'bkb', 'bv', 'vt', 'iq', 'ik', 'ikb', 'iwt', 'cq', 'ck', 'cv', 'ckb', 'cvb', 'av')


def _proj_call(x, lw, *, tm, ca, sample):
    n = x.shape[0]
    grid = (n // tm,)

    def tile(c):
        return pl.BlockSpec((tm, c), lambda i: (i, 0))

    row_out = {'oa': (A_HALF, BF16), 'qg': (B_HEADS * LANES, BF16), 'bk': (128, F32), 'bkb': (128, BF16),
               'bv': (128, F32), 'iq': (256, BF16), 'ik': (IDX_DIM, F32), 'ikb': (IDX_DIM, BF16),
               'cq': (512, BF16), 'ck': (512, F32), 'cv': (512, F32), 'ckb': (512, BF16), 'cvb': (512, BF16),
               'av': (A_HALF, F32)}
    out_specs, out_shape = [], []
    for name in _PROJ_NAMES:
        if name == 'vt':
            out_specs.append(pl.BlockSpec((tm // DSA_KB, 128, DSA_KB), lambda i: (i, 0, 0)))
            out_shape.append(jax.ShapeDtypeStruct((n // DSA_KB, 128, DSA_KB), BF16))
        elif name == 'iwt':
            out_specs.append(pl.BlockSpec((IDX_HEADS, tm), lambda i: (0, i)))
            out_shape.append(jax.ShapeDtypeStruct((IDX_HEADS, n), F32))
        else:
            c, dt = row_out[name]
            out_specs.append(tile(c))
            out_shape.append(jax.ShapeDtypeStruct((n, c), dt))
    ma = lw['ma_sample'] if sample else lw['ma_prompt']
    abias = lw['abias_sample'] if sample else lw['abias_prompt']
    consts = [lw['norm_mix'], lw['w1'], lw['w_vt'], lw['w_iwt'], lw['a_vnorm'], lw['qn'], lw['kn'],
              lw['g512'], lw['g128'], ma, abias]
    outs = pl.pallas_call(
        functools.partial(_proj_kernel, tm=tm, ca=ca, sample=sample),
        grid=grid,
        in_specs=[tile(D_MODEL)] + [_const_spec(c.shape) for c in consts],
        out_specs=out_specs,
        out_shape=out_shape,
        compiler_params=pltpu.CompilerParams(dimension_semantics=("arbitrary",), vmem_limit_bytes=VMEM_LIMIT),
        name="proj_sample" if sample else "proj_prompt",
    )(x, *consts)
    return dict(zip(_PROJ_NAMES, outs))


def _dsa_kernel(qg_ref, iq_ref, iwt_ref, ik_ref, k_ref, vt_ref, o_ref, keys_ref,
                *, qb, kb_size, q_pos_static, n_valid, nkb_static, topk, transpose_out):
    if q_pos_static is None:
        q0 = pl.program_id(1) * qb
        nkb = (q0 + qb + kb_size - 1) // kb_size
    else:
        q0 = q_pos_static
        nkb = nkb_static

    cb_size = 2 * kb_size
    ncb = (nkb + 1) // 2
    kpos0 = lax.broadcasted_iota(I32, (kb_size, qb), 0)
    cpos0 = lax.broadcasted_iota(I32, (cb_size, qb), 0)
    qpos_row = q0 + lax.broadcasted_iota(I32, (1, qb), 1)
    qchunk_row = qpos_row // CHUNK
    n_adm = jnp.minimum((qchunk_row + 1) * CHUNK, n_valid)

    def fold(x, op):
        return op(op(x.reshape(kb_size // 8, 8, qb), axis=0), axis=0, keepdims=True)

    iq = iq_ref[...]
    iq_rows = jnp.concatenate([iq[:, h * IDX_DIM:(h + 1) * IDX_DIM] for h in range(IDX_HEADS)], axis=0)
    iwt = iwt_ref[...]

    def score_body(cb, carry):
        off = pl.multiple_of(cb * cb_size, cb_size)
        d = _dot_t(ik_ref[pl.ds(off, cb_size), :], iq_rows)
        acc = jnp.zeros((cb_size, qb), F32)
        for h in range(IDX_HEADS):
            acc = acc + jnp.maximum(d[:, h * qb:(h + 1) * qb], 0.0) * iwt[h:h + 1, :]
        kpos = off + cpos0
        adm = (kpos // CHUNK <= qchunk_row) & (kpos < n_valid)
        bits = pltpu.bitcast(jnp.where(adm, acc, -jnp.inf), I32)
        keys_ref[pl.ds(off, cb_size), :] = jnp.where(bits < 0, bits ^ 0x7FFFFFFF, bits)
        return carry

    lax.fori_loop(0, ncb, score_body, 0)

    acc_rows = 32

    def count(pred):
        def body(cb, cnt):
            off = pl.multiple_of(cb * cb_size, cb_size)
            ind = jnp.where(pred(keys_ref[pl.ds(off, cb_size), :], off), 1.0, 0.0)
            return cnt + jnp.sum(ind.reshape(cb_size // acc_rows, acc_rows, qb), axis=0)
        cnt = lax.fori_loop(0, ncb, body, jnp.zeros((acc_rows, qb), F32))
        return jnp.sum(jnp.sum(cnt.reshape(acc_rows // 8, 8, qb), axis=0), axis=0, keepdims=True)

    k_f = float(topk)
    few = n_adm <= topk
    bits_per_trip = 4

    def sel_cond(state):
        t, _, _, pending = state
        return (t < 32) & (pending > 0.0)

    def sel_body(state):
        t, ans, c_ans, _ = state
        for e in range(bits_per_trip):
            cand_u = ans | lax.shift_left(jnp.int32(1), 31 - e - t)
            cand = cand_u ^ INT_MIN
            c = count(lambda blk, off: blk >= cand)
            take = c >= k_f
            ans = jnp.where(take, cand_u, ans)
            c_ans = jnp.where(take, c, c_ans)
        pending = jnp.max(jnp.where(few | (c_ans == k_f), 0.0, 1.0))
        return t + bits_per_trip, ans, c_ans, pending

    _, ans, _, _ = lax.while_loop(
        sel_cond, sel_body,
        (jnp.int32(0), jnp.zeros((1, qb), I32), jnp.full((1, qb), 2.0 * k_f + 1.0, F32), jnp.float32(1.0)))
    thr = jnp.where(few, NEG_INF_KEY, ans ^ INT_MIN)

    c_gt = count(lambda blk, off: blk > thr)
    c_eq = count(lambda blk, off: blk == thr)
    need = k_f - c_gt
    tie = jnp.max(jnp.where((c_eq > need) & (thr > NEG_INF_KEY), 1.0, 0.0))

    def tie_break(_):
        def idx_bit_body(t, jans):
            cand = jans | lax.shift_left(jnp.int32(1), 13 - t)
            c = count(lambda blk, off: (blk == thr) & (off + cpos0 < cand))
            return jnp.where(c < need, cand, jans)
        return lax.fori_loop(0, 14, idx_bit_body, jnp.zeros((1, qb), I32))

    jmax = lax.cond(tie > 0.0, tie_break, lambda _: jnp.full((1, qb), 2 ** 30, I32), 0)

    qg = qg_ref[...]
    q_pairs = [jnp.concatenate([qg[:, (2 * p) * LANES:(2 * p + 1) * LANES],
                                qg[:, (2 * p + 1) * LANES:(2 * p + 2) * LANES]], axis=0)
               for p in range(B_HEADS // 2)]
    hpg = B_HEADS // B_KV_HEADS

    def att_body(kb, carry):
        ms, ls, accs = carry
        off = pl.multiple_of(kb * kb_size, kb_size)
        key_blk = keys_ref[pl.ds(off, kb_size), :]
        kpos = off + kpos0
        sel = ((key_blk > thr) | ((key_blk == thr) & (kpos <= jmax))) & (key_blk > NEG_INF_KEY)
        bias = jnp.where(sel, 0.0, -jnp.inf)
        k_blk = k_ref[pl.ds(off, kb_size), :]
        vt_blk = vt_ref[kb]
        new_m, new_l, new_a = list(ms), list(ls), list(accs)
        sts = [_dot_t(k_blk, q_pairs[p]) for p in range(B_HEADS // 2)]
        pts, alphas = [], []
        for h in range(B_HEADS):
            s = sts[h // 2][:, (h % 2) * qb:(h % 2 + 1) * qb] + bias
            m_new = jnp.maximum(ms[h], fold(s, jnp.max))
            m_safe = jnp.where(m_new == -jnp.inf, 0.0, m_new)
            pt = jnp.exp(s - m_safe)
            alpha = jnp.exp(ms[h] - m_safe)
            new_m[h] = m_new
            new_l[h] = alpha * ls[h] + fold(pt, jnp.sum)
            pts.append(pt.astype(BF16))
            alphas.append(alpha)
        for h in range(B_HEADS):
            g = h // hpg
            new_a[h] = alphas[h] * accs[h] + _dot(vt_blk[g * B_HEAD_DIM:(g + 1) * B_HEAD_DIM, :], pts[h])
        return tuple(new_m), tuple(new_l), tuple(new_a)

    init = (tuple(jnp.full((1, qb), -jnp.inf, F32) for _ in range(B_HEADS)),
            tuple(jnp.zeros((1, qb), F32) for _ in range(B_HEADS)),
            tuple(jnp.zeros((B_HEAD_DIM, qb), F32) for _ in range(B_HEADS)))
    ms, ls, accs = lax.fori_loop(0, nkb, att_body, init)
    out_t = jnp.concatenate([accs[h] / ls[h] for h in range(B_HEADS)], axis=0)
    if transpose_out:
        o_ref[...] = out_t.T.astype(BF16)
    else:
        o_ref[...] = out_t.astype(BF16)


def _dsa_call(qg, iq, iwt, ik, k, vt, *, qb, kb_size, q_pos_static, n_valid, transpose_out, name):
    b, sq, _ = qg.shape
    lp = ik.shape[1]
    nq = sq // qb
    nkb = lp // kb_size
    topk = min(B_TOPK_MAX, n_valid // 4)

    def qspec(c):
        return pl.BlockSpec((None, qb, c), lambda bi, i: (bi, i, 0))

    def kspec(c):
        return pl.BlockSpec((None, lp, c), lambda bi, i: (bi, 0, 0))

    if transpose_out:
        out_spec, out_shape = qspec(512), jax.ShapeDtypeStruct((b, sq, 512), BF16)
    else:
        out_spec = pl.BlockSpec((None, 512, qb), lambda bi, i: (bi * nq + i, 0, 0))
        out_shape = jax.ShapeDtypeStruct((b * nq, 512, qb), BF16)
    return pl.pallas_call(
        functools.partial(_dsa_kernel, qb=qb, kb_size=kb_size, q_pos_static=q_pos_static, n_valid=n_valid,
                          nkb_static=nkb, topk=topk, transpose_out=transpose_out),
        grid=(b, nq),
        in_specs=[qspec(B_HEADS * LANES), qspec(256),
                  pl.BlockSpec((None, IDX_HEADS, qb), lambda bi, i: (bi * nq + i, 0, 0)),
                  kspec(IDX_DIM), kspec(128),
                  pl.BlockSpec((nkb, 128, kb_size), lambda bi, i: (bi, 0, 0))],
        out_specs=out_spec,
        out_shape=out_shape,
        scratch_shapes=[pltpu.VMEM((lp, qb), I32)],
        compiler_params=pltpu.CompilerParams(dimension_semantics=("arbitrary", "arbitrary"),
                                             vmem_limit_bytes=VMEM_LIMIT),
        name=name,
    )(qg, iq, iwt, ik, k, vt)


def _sb_masked_q(q):
    qb = q.shape[0]
    lane = lax.broadcasted_iota(I32, (qb, LANES), 1)
    out = []
    for h in range(C_HEADS):
        pair = q[:, (h // 2) * LANES:(h // 2 + 1) * LANES]
        out.append(jnp.where(lane // C_HEAD_DIM == h % 2, pair, jnp.zeros_like(pair)))
    return out


def _sb_block(q_heads, k_blk, v_blk, tri, carries, accs, vis):
    new_c, new_a = [], []
    for h in range(C_HEADS):
        ps = slice((h // 2) * LANES, (h // 2 + 1) * LANES)
        z = _dot_t(q_heads[h], k_blk[:, ps])
        sp = jnp.maximum(z, 0.0) + jnp.log1p(jnp.exp(-jnp.abs(z)))
        log_stay = -sp
        if vis is not None:
            log_stay = jnp.where(vis, log_stay, 0.0)
        hi = log_stay.astype(BF16)
        lo = (log_stay - hi.astype(F32)).astype(BF16)
        after = _dot(hi, tri) + _dot(lo, tri) + carries[h]
        w = jnp.exp((z - sp) + after)
        if vis is not None:
            w = jnp.where(vis, w, 0.0)
        new_a.append(accs[h] + _dot(w.astype(BF16), v_blk[:, ps]))
        new_c.append(carries[h] + jnp.sum(log_stay, axis=1, keepdims=True))
    return tuple(new_c), tuple(new_a)


def _sb_max_carry(carries):
    m = carries[0]
    for c in carries[1:]:
        m = jnp.maximum(m, c)
    return jnp.max(m)


def _sb_finish(o_ref, accs):
    qb = accs[0].shape[0]
    lane = lax.broadcasted_iota(I32, (qb, LANES), 1)
    for j in range(C_HEADS // 2):
        o_ref[:, j * LANES:(j + 1) * LANES] = jnp.where(lane < C_HEAD_DIM, accs[2 * j], accs[2 * j + 1]).astype(BF16)


def _sb_tri():
    r = lax.broadcasted_iota(I32, (LANES, LANES), 0)
    c = lax.broadcasted_iota(I32, (LANES, LANES), 1)
    return jnp.where(r > c, 1.0, 0.0).astype(BF16)


def _sb_init(qb):
    return (tuple(jnp.zeros((qb, 1), F32) for _ in range(C_HEADS)),
            tuple(jnp.zeros((qb, LANES), F32) for _ in range(C_HEADS)))


def _sb_prompt_kernel(q_ref, k_ref, v_ref, o_ref, *, qb):
    i = pl.program_id(1)
    q_heads = _sb_masked_q(q_ref[...])
    tri = _sb_tri()
    rows = lax.broadcasted_iota(I32, (qb, LANES), 0)
    lane = lax.broadcasted_iota(I32, (qb, LANES), 1)
    off = pl.multiple_of(i * qb, qb)
    carries, accs = _sb_init(qb)
    carries, accs = _sb_block(q_heads, k_ref[pl.ds(off, qb), :], v_ref[pl.ds(off, qb), :], tri,
                              carries, accs, lane < rows)

    def cond(state):
        kb, mx, _, _ = state
        return (kb >= 0) & (mx > SB_SKIP)

    def body(state):
        kb, _, carries, accs = state
        o = pl.multiple_of(kb * qb, qb)
        carries, accs = _sb_block(q_heads, k_ref[pl.ds(o, qb), :], v_ref[pl.ds(o, qb), :], tri,
                                  carries, accs, None)
        return kb - 1, _sb_max_carry(carries), carries, accs

    _, _, _, accs = lax.while_loop(cond, body, (i - 1, _sb_max_carry(carries), carries, accs))
    _sb_finish(o_ref, accs)


def _sb_prompt_call(cq, ckb, cvb, *, qb):
    b, s, _ = cq.shape
    qspec = pl.BlockSpec((None, qb, 512), lambda bi, i: (bi, i, 0))
    kspec = pl.BlockSpec((None, s, 512), lambda bi, i: (bi, 0, 0))
    return pl.pallas_call(
        functools.partial(_sb_prompt_kernel, qb=qb),
        grid=(b, s // qb),
        in_specs=[qspec, kspec, kspec],
        out_specs=qspec,
        out_shape=jax.ShapeDtypeStruct((b, s, 512), BF16),
        compiler_params=pltpu.CompilerParams(dimension_semantics=("arbitrary", "arbitrary"),
                                             vmem_limit_bytes=VMEM_LIMIT),
        name="stick_prompt",
    )(cq, ckb, cvb)


def _sb_sample_kernel(q_ref, kn_ref, vn_ref, kp_ref, vp_ref, o_ref, *, qb, n_past):
    q_heads = _sb_masked_q(q_ref[...])
    tri = _sb_tri()
    rows = lax.broadcasted_iota(I32, (qb, LANES), 0)
    lane = lax.broadcasted_iota(I32, (qb, LANES), 1)
    carries, accs = _sb_init(qb)
    carries, accs = _sb_block(q_heads, kn_ref[...], vn_ref[...], tri, carries, accs, lane < rows)

    def cond(state):
        kb, mx, _, _ = state
        return (kb >= 0) & (mx > SB_SKIP)

    def body(state):
        kb, _, carries, accs = state
        o = pl.multiple_of(kb * LANES, LANES)
        carries, accs = _sb_block(q_heads, kp_ref[pl.ds(o, LANES), :].astype(BF16),
                                  vp_ref[pl.ds(o, LANES), :].astype(BF16), tri, carries, accs, None)
        return kb - 1, _sb_max_carry(carries), carries, accs

    _, _, _, accs = lax.while_loop(cond, body, (n_past // LANES - 1, _sb_max_carry(carries), carries, accs))
    _sb_finish(o_ref, accs)


def _sb_sample_call(cq, kn, vn, kp, vp):
    b, t, _ = cq.shape
    n_past = kp.shape[1]
    qspec = pl.BlockSpec((None, t, 512), lambda bi: (bi, 0, 0))
    nspec = pl.BlockSpec((None, LANES, 512), lambda bi: (bi, 0, 0))
    pspec = pl.BlockSpec((None, n_past, 512), lambda bi: (bi, 0, 0))
    return pl.pallas_call(
        functools.partial(_sb_sample_kernel, qb=t, n_past=n_past),
        grid=(b,),
        in_specs=[qspec, nspec, nspec, pspec, pspec],
        out_specs=qspec,
        out_shape=jax.ShapeDtypeStruct((b, t, 512), BF16),
        compiler_params=pltpu.CompilerParams(dimension_semantics=("arbitrary",), vmem_limit_bytes=VMEM_LIMIT),
        name="stick_sample",
    )(cq, kn, vn, kp, vp)


def _merge_kernel(x_ref, nmix_ref, wg_ref, gb_ref, oa_ref, ob_ref, oc_ref, wa_ref, wb_ref, wc_ref, wo_ref, o_ref):
    x = x_ref[...]
    hb = _rms(x, nmix_ref[...]).astype(BF16)
    merged = None
    for idx, (br_ref, w_ref) in enumerate(((oa_ref, wa_ref), (ob_ref, wb_ref), (oc_ref, wc_ref))):
        cs = slice(idx * D_MODEL, (idx + 1) * D_MODEL)
        gate = jax.nn.sigmoid(_dot(hb, wg_ref[:, cs]) + gb_ref[:, cs])
        term = gate * _dot(br_ref[...], w_ref[...])
        merged = term if merged is None else merged + term
    o_ref[...] = x + _dot(merged.astype(BF16), wo_ref[...])


def _merge_call(x, oa, ob, oc, lw, *, tm, name):
    n = x.shape[0]

    def tile(c):
        return pl.BlockSpec((tm, c), lambda i: (i, 0))

    consts_a = [lw['norm_mix'], lw['w_gate'], lw['gate_bias']]
    consts_b = [lw['w_br_a'], lw['w_br_b'], lw['w_br_c'], lw['w_out']]
    return pl.pallas_call(
        _merge_kernel,
        grid=(n // tm,),
        in_specs=([tile(D_MODEL)] + [_const_spec(c.shape) for c in consts_a] + [tile(512)] * 3
                  + [_const_spec(c.shape) for c in consts_b]),
        out_specs=tile(D_MODEL),
        out_shape=jax.ShapeDtypeStruct((n, D_MODEL), F32),
        compiler_params=pltpu.CompilerParams(dimension_semantics=("arbitrary",), vmem_limit_bytes=VMEM_LIMIT),
        name=name,
    )(x, *consts_a, oa, ob, oc, *consts_b)


def _ffn_kernel(x_ref, p_ref, nffn_ref, wfi_ref, wfo_ref, nple_ref, wpg_ref, wpp_ref, o_ref):
    x = x_ref[...]
    hf = _rms(x, nffn_ref[...]).astype(BF16)
    g = _dot(hf, wfi_ref[:, 0:D_FF])
    up = _dot(hf, wfi_ref[:, D_FF:2 * D_FF])
    x = x + _dot((jax.nn.silu(g) * up).astype(BF16), wfo_ref[...])
    hp = _rms(x, nple_ref[...]).astype(BF16)
    gate = jax.nn.sigmoid(_dot(hp, wpg_ref[...]))
    o_ref[...] = x + gate * _dot(p_ref[...].astype(BF16), wpp_ref[...])


def _ffn_call(x, p, lw, *, tm, name):
    n = x.shape[0]
    consts = [lw['norm_ffn'], lw['w_ffn_in'], lw['w_ffn_out'], lw['norm_ple'], lw['w_ple_gate'], lw['w_ple_proj']]
    return pl.pallas_call(
        _ffn_kernel,
        grid=(n // tm,),
        in_specs=[pl.BlockSpec((tm, D_MODEL), lambda i: (i, 0)), pl.BlockSpec((tm, PLE_DIM), lambda i: (i, 0))]
        + [_const_spec(c.shape) for c in consts],
        out_specs=pl.BlockSpec((tm, D_MODEL), lambda i: (i, 0)),
        out_shape=jax.ShapeDtypeStruct((n, D_MODEL), F32),
        compiler_params=pltpu.CompilerParams(dimension_semantics=("arbitrary",), vmem_limit_bytes=VMEM_LIMIT),
        name=name,
    )(x, p, *consts)


def _block_diag_mean(n):
    r = jnp.arange(n)
    return jnp.where((r[:, None] // B_HEAD_DIM) == (r[None, :] // B_HEAD_DIM), 1.0 / B_HEAD_DIM, 0.0).astype(BF16)


def _prep_layer(l, t_sample, w):
    w_in = w['w_in'][l]
    small = jnp.pad(w_in[:, 2048:2088], ((0, 0), (0, LANES - IDX_DIM - IDX_HEADS)))
    w1 = jnp.concatenate([w_in[:, :2048], small, w_in[:, 2088:3624]], axis=1).astype(BF16)
    a_ws = w['a_ws'][l]
    a_bias_full = jnp.repeat(w['a_bias'][l].T, A_GROUP_DIM, axis=1)
    row = lambda a: a.reshape(1, -1)
    return {
        'norm_mix': row(w['norm_mix'][l]), 'w1': w1,
        'w_vt': w_in[:, 1664:1792].T.astype(BF16), 'w_iwt': w_in[:, 2080:2088].T.astype(BF16),
        'a_vnorm': row(w['a_vnorm'][l]),
        'qn': row(jnp.tile(w['b_qnorm'][l], B_HEADS)), 'kn': row(jnp.tile(w['b_knorm'][l], B_KV_HEADS)),
        'g512': _block_diag_mean(512), 'g128': _block_diag_mean(128),
        'ma_prompt': a_ws, 'abias_prompt': a_bias_full,
        'ma_sample': jnp.tile(a_ws[:, :t_sample, :t_sample], (1, 16, 16)),
        'abias_sample': jnp.tile(a_bias_full[:t_sample], (16, 1)),
        'w_gate': w_in[:, 3624:].astype(BF16), 'gate_bias': row(w['gate_bias'][l]),
        'w_br_a': w['w_br_a'][l].astype(BF16), 'w_br_b': w['w_br_b'][l].astype(BF16),
        'w_br_c': w['w_br_c'][l].astype(BF16), 'w_out': w['w_out'][l].astype(BF16),
        'norm_ffn': row(w['norm_ffn'][l]), 'w_ffn_in': w['w_ffn_in'][l].astype(BF16),
        'w_ffn_out': w['w_ffn_out'][l].astype(BF16), 'norm_ple': row(w['norm_ple'][l]),
        'w_ple_gate': w['w_ple_gate'][l].astype(BF16), 'w_ple_proj': w['w_ple_proj'][l].astype(BF16),
    }


def _iwt_blocks(iwt, qb):
    return iwt.reshape(IDX_HEADS, -1, qb).transpose(1, 0, 2)


def _prompt_layer(x, p, lw, b, s):
    pr = _proj_call(x, lw, tm=512, ca=A_CHUNK, sample=False)
    r3 = lambda a: a.reshape(b, s, a.shape[-1])
    ob = _dsa_call(r3(pr['qg']), r3(pr['iq']), _iwt_blocks(pr['iwt'], 128), r3(pr['ikb']), r3(pr['bkb']), pr['vt'],
                   qb=128, kb_size=DSA_KB, q_pos_static=None, n_valid=s, transpose_out=True, name="dsa_prompt")
    oc = _sb_prompt_call(r3(pr['cq']), r3(pr['ckb']), r3(pr['cvb']), qb=128)
    x = _merge_call(x, pr['oa'], ob.reshape(b * s, 512), oc.reshape(b * s, 512), lw, tm=256, name="merge_prompt")
    x = _ffn_call(x, p, lw, tm=256, name="ffn_prompt")
    return x, pr


def _sample_layer(x, p, lw, b, t, past):
    pbk, pbv, pik, pck, pcv = past
    n_past = pbk.shape[1]
    pr = _proj_call(x, lw, tm=b * t, ca=b * t, sample=True)
    r3 = lambda a: a.reshape(b, t, a.shape[-1])
    n_keys = n_past + t
    pad = (-n_keys) % (2 * DSA_KB)

    def cat(past_part, new_part):
        both = jnp.concatenate([past_part.reshape(b, n_past, -1).astype(BF16), r3(new_part).astype(BF16)], axis=1)
        return jnp.pad(both, ((0, 0), (0, pad), (0, 0)))

    lp = n_keys + pad
    vt = cat(pbv, pr['bv']).reshape(b, lp // DSA_KB, DSA_KB, 128).transpose(0, 1, 3, 2)
    ob_t = _dsa_call(r3(pr['qg']), r3(pr['iq']), _iwt_blocks(pr['iwt'], t), cat(pik, pr['ik']), cat(pbk, pr['bk']),
                     vt.reshape(b * (lp // DSA_KB), 128, DSA_KB),
                     qb=t, kb_size=DSA_KB, q_pos_static=n_past, n_valid=n_keys, transpose_out=False,
                     name="dsa_sample")
    ob = ob_t.transpose(0, 2, 1)
    padn = lambda a: jnp.pad(r3(a), ((0, 0), (0, LANES - t), (0, 0)))
    oc = _sb_sample_call(r3(pr['cq']), padn(pr['ckb']), padn(pr['cvb']),
                         pck.reshape(b, n_past, 512), pcv.reshape(b, n_past, 512))
    x = _merge_call(x, pr['oa'], ob.reshape(b * t, 512), oc.reshape(b * t, 512), lw, tm=b * t, name="merge_sample")
    x = _ffn_call(x, p, lw, tm=b * t, name="ffn_sample")
    return x, pr


def kernel(x_prompt, x_sample, cache_b_k, cache_b_v, cache_b_kidx, cache_c_k, cache_c_v, p_prompt, p_sample,
           norm_mix, w_in, gate_bias, a_vnorm, a_ws, a_bias, b_qnorm, b_knorm, w_br_a, w_br_b, w_br_c, w_out,
           norm_ffn, w_ffn_in, w_ffn_out, norm_ple, w_ple_gate, w_ple_proj):
    weights = dict(norm_mix=norm_mix, w_in=w_in, gate_bias=gate_bias, a_vnorm=a_vnorm, a_ws=a_ws, a_bias=a_bias,
                   b_qnorm=b_qnorm, b_knorm=b_knorm, w_br_a=w_br_a, w_br_b=w_br_b, w_br_c=w_br_c, w_out=w_out,
                   norm_ffn=norm_ffn, w_ffn_in=w_ffn_in, w_ffn_out=w_ffn_out, norm_ple=norm_ple,
                   w_ple_gate=w_ple_gate, w_ple_proj=w_ple_proj)
    depth = w_in.shape[0]
    bp, sp, _ = x_prompt.shape
    bs, ts, _ = x_sample.shape
    yp = x_prompt.reshape(bp * sp, D_MODEL)
    ys = x_sample.reshape(bs * ts, D_MODEL)
    new_p, new_s = [], []
    for l in range(depth):
        lw = _prep_layer(l, ts, weights)
        yp, pr_p = _prompt_layer(yp, p_prompt[l].reshape(bp * sp, PLE_DIM), lw, bp, sp)
        ys, pr_s = _sample_layer(ys, p_sample[l].reshape(bs * ts, PLE_DIM), lw, bs, ts,
                                 (cache_b_k[l], cache_b_v[l], cache_b_kidx[l], cache_c_k[l], cache_c_v[l]))
        new_p.append(pr_p)
        new_s.append(pr_s)

    def stack(states, name, lead, tail):
        return jnp.stack([s[name].reshape(*lead, *tail) for s in states])

    lp, ls = (bp, sp), (bs, ts)
    return (yp.reshape(bp, sp, D_MODEL), ys.reshape(bs, ts, D_MODEL),
            stack(new_p, 'bk', lp, (B_KV_HEADS, B_HEAD_DIM)), stack(new_p, 'bv', lp, (B_KV_HEADS, B_HEAD_DIM)),
            stack(new_p, 'ik', lp, (IDX_DIM,)),
            stack(new_p, 'ck', lp, (C_HEADS, C_HEAD_DIM)), stack(new_p, 'cv', lp, (C_HEADS, C_HEAD_DIM)),
            stack(new_s, 'bk', ls, (B_KV_HEADS, B_HEAD_DIM)), stack(new_s, 'bv', ls, (B_KV_HEADS, B_HEAD_DIM)),
            stack(new_s, 'ik', ls, (IDX_DIM,)),
            stack(new_s, 'ck', ls, (C_HEADS, C_HEAD_DIM)), stack(new_s, 'cv', ls, (C_HEADS, C_HEAD_DIM)),
            stack(new_s, 'av', ls, (A_HALF,)))
```

```python
import functools

import jax
import jax.numpy as jnp
from jax import lax
from jax.experimental import pallas as pl
from jax.experimental.pallas import tpu as pltpu

F32 = jnp.float32
BF16 = jnp.bfloat16
I32 = jnp.int32

D_MODEL = 1024
CHUNK = 64
EPS = 1e-6
A_CHUNK = 128
A_GROUPS = 4
A_GROUP_DIM = 128
A_HALF = 512
B_HEADS = 8
B_KV_HEADS = 2
B_HEAD_DIM = 64
B_TOPK_MAX = 256
IDX_HEADS = 8
IDX_DIM = 32
C_HEADS = 8
C_HEAD_DIM = 64
N_BRANCH = 3
D_FF = 2816
PLE_DIM = 256

LANES = 128
VMEM_LIMIT = 56 * 1024 * 1024

_C_AU, _C_AV, _C_BQ, _C_BK, _C_BV, _C_IQ = 0, 512, 1024, 1536, 1664, 1792
_C_SM, _C_CQ, _C_CK, _C_CV, _C_END = 2048, 2176, 2688, 3200, 3712
DSA_KB = 256

INT_MIN = -2147483648
NEG_INF_KEY = (-8388608) ^ 0x7FFFFFFF
SB_SKIP = -104.0


def _const_spec(shape):
    zeros = (0,) * len(shape)
    return pl.BlockSpec(shape, lambda *_: zeros, pipeline_mode=pl.Buffered(1))


def _rms(x, gain):
    return x * lax.rsqrt(jnp.mean(x * x, axis=-1, keepdims=True) + EPS) * gain


def _dot(a, b):
    return jnp.dot(a, b, preferred_element_type=F32)


def _dot_t(a, b):
    return lax.dot_general(a, b, (((1,), (1,)), ((), ())), preferred_element_type=F32)


def _split_dot(x, m_ref):
    hi = x.astype(BF16)
    lo = (x - hi.astype(F32)).astype(BF16)
    m = m_ref[...]
    return _dot(hi, m) + _dot(lo, m)


def _group_rms(z, g_ref, gain):
    return z * lax.rsqrt(_split_dot(z * z, g_ref) + EPS) * gain


def _proj_kernel(x_ref, nmix_ref, w_ref, wvt_ref, wiwt_ref, avn_ref, qn_ref, kn_ref, g512_ref, g128_ref,
                 ma_ref, abias_ref,
                 oa_ref, qg_ref, qh_ref, bk_ref, bkb_ref, bv_ref, vt_ref, iq_ref, ik_ref, ikb_ref, iwt_ref,
                 cq_ref, ck_ref, cv_ref, ckb_ref, cvb_ref, av_ref, *, tm, ca, sample):
    hb = _rms(x_ref[...], nmix_ref[...]).astype(BF16)

    def proj(c0, c1):
        return _dot(hb, w_ref[:, c0:c1])

    qn = _group_rms(proj(_C_BQ, _C_BK), g512_ref, qn_ref[...]) * (B_HEAD_DIM ** -0.5)
    qh_ref[...] = qn.astype(BF16)
    lane = lax.broadcasted_iota(I32, (tm, LANES), 1)
    for h in range(B_HEADS):
        g = h // (B_HEADS // B_KV_HEADS)
        pair = qn[:, (h // 2) * LANES:(h // 2 + 1) * LANES]
        if h % 2 != g:
            pair = pltpu.roll(pair, B_HEAD_DIM, axis=1)
        qg_ref[:, h * LANES:(h + 1) * LANES] = jnp.where(lane // B_HEAD_DIM == g, pair, 0.0).astype(BF16)
    bk = _group_rms(proj(_C_BK, _C_BV), g128_ref, kn_ref[...])
    bk_ref[...] = bk
    bkb_ref[...] = bk.astype(BF16)
    bv_ref[...] = proj(_C_BV, _C_IQ)
    vt = _dot_t(wvt_ref[...], hb).astype(BF16)
    for c in range(tm // DSA_KB):
        vt_ref[c] = vt[:, c * DSA_KB:(c + 1) * DSA_KB]
    iq_ref[...] = proj(_C_IQ, _C_SM).astype(BF16)
    ik = proj(_C_SM, _C_CQ)[:, 0:IDX_DIM]
    ik_ref[...] = ik
    ikb_ref[...] = ik.astype(BF16)
    iwt_ref[...] = _dot_t(wiwt_ref[...], hb) * (1.0 / 16.0)
    cq_ref[...] = (proj(_C_CQ, _C_CK) * (C_HEAD_DIM ** -0.5)).astype(BF16)
    ck = proj(_C_CK, _C_CV)
    ck_ref[...] = ck
    ckb_ref[...] = ck.astype(BF16)
    cv = proj(_C_CV, _C_END)
    cv_ref[...] = cv
    cvb_ref[...] = cv.astype(BF16)

    au = jax.nn.gelu(proj(_C_AU, _C_AV))
    av = _rms(jax.nn.gelu(proj(_C_AV, _C_BQ)), avn_ref[...])
    av_ref[...] = av
    avb = av.astype(BF16)
    row = lax.broadcasted_iota(I32, (ca, ca), 0)
    col = lax.broadcasted_iota(I32, (ca, ca), 1)
    if sample:
        vis = (col // 16) == (row // 16)
    else:
        vis = (col // CHUNK) <= (row // CHUNK)
    for g in range(A_GROUPS):
        wm = jnp.where(vis, ma_ref[g], 0.0).astype(BF16)
        cs = slice(g * A_GROUP_DIM, (g + 1) * A_GROUP_DIM)
        for c in range(tm // ca):
            rs = slice(c * ca, (c + 1) * ca)
            mixed = _dot(wm, avb[rs, cs]) + abias_ref[:, cs]
            oa_ref[rs, cs] = (au[rs, cs] * mixed).astype(BF16)


_PROJ_NAMES = ('oa', 'qg', 'qh', 'bk', 'bkb', 'bv', 'vt', 'iq', 'ik', 'ikb', 'iwt', 'cq', 'ck', 'cv', 'ckb', 'cvb', 'av')


def _proj_call(x, lw, *, tm, ca, sample):
    n = x.shape[0]
    grid = (n // tm,)

    def tile(c):
        return pl.BlockSpec((tm, c), lambda i: (i, 0))

    row_out = {'oa': (A_HALF, BF16), 'qg': (B_HEADS * LANES, BF16), 'qh': (512, BF16), 'bk': (128, F32), 'bkb': (128, BF16),
               'bv': (128, F32), 'iq': (256, BF16), 'ik': (IDX_DIM, F32), 'ikb': (IDX_DIM, BF16),
               'cq': (512, BF16), 'ck': (512, F32), 'cv': (512, F32), 'ckb': (512, BF16), 'cvb': (512, BF16),
               'av': (A_HALF, F32)}
    out_specs, out_shape = [], []
    for name in _PROJ_NAMES:
        if name == 'vt':
            out_specs.append(pl.BlockSpec((tm // DSA_KB, 128, DSA_KB), lambda i: (i, 0, 0)))
            out_shape.append(jax.ShapeDtypeStruct((n // DSA_KB, 128, DSA_KB), BF16))
        elif name == 'iwt':
            out_specs.append(pl.BlockSpec((IDX_HEADS, tm), lambda i: (0, i)))
            out_shape.append(jax.ShapeDtypeStruct((IDX_HEADS, n), F32))
        else:
            c, dt = row_out[name]
            out_specs.append(tile(c))
            out_shape.append(jax.ShapeDtypeStruct((n, c), dt))
    ma = lw['ma_sample'] if sample else lw['ma_prompt']
    abias = lw['abias_sample'] if sample else lw['abias_prompt']
    consts = [lw['norm_mix'], lw['w1'], lw['w_vt'], lw['w_iwt'], lw['a_vnorm'], lw['qn'], lw['kn'],
              lw['g512'], lw['g128'], ma, abias]
    outs = pl.pallas_call(
        functools.partial(_proj_kernel, tm=tm, ca=ca, sample=sample),
        grid=grid,
        in_specs=[tile(D_MODEL)] + [_const_spec(c.shape) for c in consts],
        out_specs=out_specs,
        out_shape=out_shape,
        compiler_params=pltpu.CompilerParams(dimension_semantics=("arbitrary",), vmem_limit_bytes=VMEM_LIMIT),
        name="proj_sample" if sample else "proj_prompt",
    )(x, *consts)
    return dict(zip(_PROJ_NAMES, outs))


def _dsa_kernel(qg_ref, iq_ref, iwt_ref, ik_ref, k_ref, vt_ref, o_ref, keys_ref,
                *, qb, kb_size, n_valid, topk):
    q0 = pl.program_id(1) * qb
    nkb = (q0 + qb + kb_size - 1) // kb_size

    cb_size = 2 * kb_size
    ncb = (nkb + 1) // 2
    kpos0 = lax.broadcasted_iota(I32, (kb_size, qb), 0)
    cpos0 = lax.broadcasted_iota(I32, (cb_size, qb), 0)
    qpos_row = q0 + lax.broadcasted_iota(I32, (1, qb), 1)
    qchunk_row = qpos_row // CHUNK
    n_adm = jnp.minimum((qchunk_row + 1) * CHUNK, n_valid)

    def fold(x, op):
        return op(op(x.reshape(kb_size // 8, 8, qb), axis=0), axis=0, keepdims=True)

    iq = iq_ref[...]
    iq_rows = jnp.concatenate([iq[:, h * IDX_DIM:(h + 1) * IDX_DIM] for h in range(IDX_HEADS)], axis=0)
    iwt = iwt_ref[...]

    def score_body(cb, carry):
        off = pl.multiple_of(cb * cb_size, cb_size)
        d = _dot_t(ik_ref[pl.ds(off, cb_size), :], iq_rows)
        acc = jnp.zeros((cb_size, qb), F32)
        for h in range(IDX_HEADS):
            acc = acc + jnp.maximum(d[:, h * qb:(h + 1) * qb], 0.0) * iwt[h:h + 1, :]
        kpos = off + cpos0
        adm = (kpos // CHUNK <= qchunk_row) & (kpos < n_valid)
        bits = pltpu.bitcast(jnp.where(adm, acc, -jnp.inf), I32)
        keys_ref[pl.ds(off, cb_size), :] = jnp.where(bits < 0, bits ^ 0x7FFFFFFF, bits)
        return carry

    lax.fori_loop(0, ncb, score_body, 0)

    acc_rows = 32

    def count(pred):
        def body(cb, cnt):
            off = pl.multiple_of(cb * cb_size, cb_size)
            ind = jnp.where(pred(keys_ref[pl.ds(off, cb_size), :], off), 1.0, 0.0)
            return cnt + jnp.sum(ind.reshape(cb_size // acc_rows, acc_rows, qb), axis=0)
        cnt = lax.fori_loop(0, ncb, body, jnp.zeros((acc_rows, qb), F32))
        return jnp.sum(jnp.sum(cnt.reshape(acc_rows // 8, 8, qb), axis=0), axis=0, keepdims=True)

    k_f = float(topk)
    few = n_adm <= topk
    bits_per_trip = 4

    def sel_cond(state):
        t, _, _, pending = state
        return (t < 32) & (pending > 0.0)

    def sel_body(state):
        t, ans, c_ans, _ = state
        for e in range(bits_per_trip):
            cand_u = ans | lax.shift_left(jnp.int32(1), 31 - e - t)
            cand = cand_u ^ INT_MIN
            c = count(lambda blk, off: blk >= cand)
            take = c >= k_f
            ans = jnp.where(take, cand_u, ans)
            c_ans = jnp.where(take, c, c_ans)
        pending = jnp.max(jnp.where(few | (c_ans == k_f), 0.0, 1.0))
        return t + bits_per_trip, ans, c_ans, pending

    _, ans, _, _ = lax.while_loop(
        sel_cond, sel_body,
        (jnp.int32(0), jnp.zeros((1, qb), I32), jnp.full((1, qb), 2.0 * k_f + 1.0, F32), jnp.float32(1.0)))
    thr = jnp.where(few, NEG_INF_KEY, ans ^ INT_MIN)

    c_gt = count(lambda blk, off: blk > thr)
    c_eq = count(lambda blk, off: blk == thr)
    need = k_f - c_gt
    tie = jnp.max(jnp.where((c_eq > need) & (thr > NEG_INF_KEY), 1.0, 0.0))

    def tie_break(_):
        def idx_bit_body(t, jans):
            cand = jans | lax.shift_left(jnp.int32(1), 13 - t)
            c = count(lambda blk, off: (blk == thr) & (off + cpos0 < cand))
            return jnp.where(c < need, cand, jans)
        return lax.fori_loop(0, 14, idx_bit_body, jnp.zeros((1, qb), I32))

    jmax = lax.cond(tie > 0.0, tie_break, lambda _: jnp.full((1, qb), 2 ** 30, I32), 0)

    qg = qg_ref[...]
    q_pairs = [jnp.concatenate([qg[:, (2 * p) * LANES:(2 * p + 1) * LANES],
                                qg[:, (2 * p + 1) * LANES:(2 * p + 2) * LANES]], axis=0)
               for p in range(B_HEADS // 2)]
    hpg = B_HEADS // B_KV_HEADS

    def att_body(kb, carry):
        ms, ls, accs = carry
        off = pl.multiple_of(kb * kb_size, kb_size)
        key_blk = keys_ref[pl.ds(off, kb_size), :]
        kpos = off + kpos0
        sel = ((key_blk > thr) | ((key_blk == thr) & (kpos <= jmax))) & (key_blk > NEG_INF_KEY)
        bias = jnp.where(sel, 0.0, -jnp.inf)
        k_blk = k_ref[pl.ds(off, kb_size), :]
        vt_blk = vt_ref[kb]
        new_m, new_l, new_a = list(ms), list(ls), list(accs)
        sts = [_dot_t(k_blk, q_pairs[p]) for p in range(B_HEADS // 2)]
        pts, alphas = [], []
        for h in range(B_HEADS):
            s = sts[h // 2][:, (h % 2) * qb:(h % 2 + 1) * qb] + bias
            m_new = jnp.maximum(ms[h], fold(s, jnp.max))
            m_safe = jnp.where(m_new == -jnp.inf, 0.0, m_new)
            pt = jnp.exp(s - m_safe)
            alpha = jnp.exp(ms[h] - m_safe)
            new_m[h] = m_new
            new_l[h] = alpha * ls[h] + fold(pt, jnp.sum)
            pts.append(pt.astype(BF16))
            alphas.append(alpha)
        for h in range(B_HEADS):
            g = h // hpg
            new_a[h] = alphas[h] * accs[h] + _dot(vt_blk[g * B_HEAD_DIM:(g + 1) * B_HEAD_DIM, :], pts[h])
        return tuple(new_m), tuple(new_l), tuple(new_a)

    init = (tuple(jnp.full((1, qb), -jnp.inf, F32) for _ in range(B_HEADS)),
            tuple(jnp.zeros((1, qb), F32) for _ in range(B_HEADS)),
            tuple(jnp.zeros((B_HEAD_DIM, qb), F32) for _ in range(B_HEADS)))
    ms, ls, accs = lax.fori_loop(0, nkb, att_body, init)
    out_t = jnp.concatenate([accs[h] / ls[h] for h in range(B_HEADS)], axis=0)
    o_ref[...] = out_t.T.astype(BF16)


def _dsa_call(qg, iq, iwt, ik, k, vt, *, qb, kb_size):
    b, sq, _ = qg.shape
    lp = ik.shape[1]
    nq = sq // qb
    nkb = lp // kb_size
    topk = min(B_TOPK_MAX, lp // 4)

    def qspec(c):
        return pl.BlockSpec((None, qb, c), lambda bi, i: (bi, i, 0))

    def kspec(c):
        return pl.BlockSpec((None, lp, c), lambda bi, i: (bi, 0, 0))

    return pl.pallas_call(
        functools.partial(_dsa_kernel, qb=qb, kb_size=kb_size, n_valid=lp, topk=topk),
        grid=(b, nq),
        in_specs=[qspec(B_HEADS * LANES), qspec(256),
                  pl.BlockSpec((None, IDX_HEADS, qb), lambda bi, i: (bi * nq + i, 0, 0)),
                  kspec(IDX_DIM), kspec(128),
                  pl.BlockSpec((nkb, 128, kb_size), lambda bi, i: (bi, 0, 0))],
        out_specs=qspec(512),
        out_shape=jax.ShapeDtypeStruct((b, sq, 512), BF16),
        scratch_shapes=[pltpu.VMEM((lp, qb), I32)],
        compiler_params=pltpu.CompilerParams(dimension_semantics=("arbitrary", "arbitrary"),
                                             vmem_limit_bytes=VMEM_LIMIT),
        name="dsa_prompt",
    )(qg, iq, iwt, ik, k, vt)


def _dsa_sample_kernel(qn_ref, iq_ref, iwc_ref, ikn_ref, ktn_ref, vtn_ref, ikp_ref, ktp_ref, vtp_ref, o_ref,
                       keys_ref, *, t, n_past, cb, topk):
    nb = n_past // cb
    hpg = B_HEADS // B_KV_HEADS
    lanes = lax.broadcasted_iota(I32, (t, cb), 1)
    iq = iq_ref[...]
    iq_rows = jnp.concatenate([iq[:, h * IDX_DIM:(h + 1) * IDX_DIM] for h in range(IDX_HEADS)], axis=0)
    iwc = iwc_ref[...]

    def to_keys(ik_blk, valid):
        w = jnp.maximum(_dot_t(iq_rows, ik_blk), 0.0) * iwc
        s = w[0:t]
        for h in range(1, IDX_HEADS):
            s = s + w[h * t:(h + 1) * t]
        if valid is not None:
            s = jnp.where(valid, s, -jnp.inf)
        bits = pltpu.bitcast(s, I32)
        return jnp.where(bits < 0, bits ^ 0x7FFFFFFF, bits)

    def past_scores(kb, carry):
        off = pl.multiple_of(kb * cb, cb)
        keys_ref[kb] = to_keys(ikp_ref[pl.ds(off, cb), :].astype(BF16), None)
        return carry

    lax.fori_loop(0, nb, past_scores, 0)
    keys_ref[nb] = to_keys(ikn_ref[...], lanes < t)

    def count(pred):
        def body(kb, cnt):
            ind = jnp.where(pred(keys_ref[kb], kb * cb + lanes), 1.0, 0.0)
            part = ind[:, 0:LANES]
            for c in range(1, cb // LANES):
                part = part + ind[:, c * LANES:(c + 1) * LANES]
            return cnt + part
        cnt = lax.fori_loop(0, nb + 1, body, jnp.zeros((t, LANES), F32))
        return jnp.sum(cnt, axis=1, keepdims=True)

    k_f = float(topk)
    bits_per_trip = 4

    def sel_cond(state):
        tt, _, _, pending = state
        return (tt < 32) & (pending > 0.0)

    def sel_body(state):
        tt, ans, c_ans, _ = state
        for e in range(bits_per_trip):
            cand_u = ans | lax.shift_left(jnp.int32(1), 31 - e - tt)
            cand = cand_u ^ INT_MIN
            c = count(lambda blk, pos: blk >= cand)
            take = c >= k_f
            ans = jnp.where(take, cand_u, ans)
            c_ans = jnp.where(take, c, c_ans)
        pending = jnp.max(jnp.where(c_ans == k_f, 0.0, 1.0))
        return tt + bits_per_trip, ans, c_ans, pending

    _, ans, _, _ = lax.while_loop(
        sel_cond, sel_body,
        (jnp.int32(0), jnp.zeros((t, 1), I32), jnp.full((t, 1), 2.0 * k_f + 1.0, F32), jnp.float32(1.0)))
    thr = ans ^ INT_MIN

    c_gt = count(lambda blk, pos: blk > thr)
    c_eq = count(lambda blk, pos: blk == thr)
    need = k_f - c_gt
    tie = jnp.max(jnp.where((c_eq > need) & (thr > NEG_INF_KEY), 1.0, 0.0))

    def tie_break(_):
        def idx_bit_body(tt, jans):
            cand = jans | lax.shift_left(jnp.int32(1), 13 - tt)
            c = count(lambda blk, pos: (blk == thr) & (pos < cand))
            return jnp.where(c < need, cand, jans)
        return lax.fori_loop(0, 14, idx_bit_body, jnp.zeros((t, 1), I32))

    jmax = lax.cond(tie > 0.0, tie_break, lambda _: jnp.full((t, 1), 2 ** 30, I32), 0)

    qn = qn_ref[...]
    q_groups = [jnp.concatenate([qn[:, h * B_HEAD_DIM:(h + 1) * B_HEAD_DIM]
                                 for h in range(g * hpg, (g + 1) * hpg)], axis=0)
                for g in range(B_KV_HEADS)]

    def attend(carry, kb, kt, vt):
        ms, ls, accs = carry
        key_blk = keys_ref[kb]
        pos = kb * cb + lanes
        sel = ((key_blk > thr) | ((key_blk == thr) & (pos <= jmax))) & (key_blk > NEG_INF_KEY)
        bias = jnp.where(sel, 0.0, -jnp.inf)
        bias_rows = jnp.concatenate([bias] * hpg, axis=0)
        new_m, new_l, new_a = [], [], []
        for g in range(B_KV_HEADS):
            s = _dot(q_groups[g], kt(g)) + bias_rows
            m_new = jnp.maximum(ms[g], jnp.max(s, axis=1, keepdims=True))
            m_safe = jnp.where(m_new == -jnp.inf, 0.0, m_new)
            p = jnp.exp(s - m_safe)
            alpha = jnp.exp(ms[g] - m_safe)
            new_m.append(m_new)
            new_l.append(alpha * ls[g] + jnp.sum(p, axis=1, keepdims=True))
            new_a.append(alpha * accs[g] + _dot_t(p.astype(BF16), vt(g)))
        return tuple(new_m), tuple(new_l), tuple(new_a)

    def past_attend(kb, carry):
        off = pl.multiple_of(kb * cb, cb)
        return attend(carry, kb, lambda g: ktp_ref[g, :, pl.ds(off, cb)].astype(BF16),
                      lambda g: vtp_ref[g, :, pl.ds(off, cb)].astype(BF16))

    init = (tuple(jnp.full((hpg * t, 1), -jnp.inf, F32) for _ in range(B_KV_HEADS)),
            tuple(jnp.zeros((hpg * t, 1), F32) for _ in range(B_KV_HEADS)),
            tuple(jnp.zeros((hpg * t, B_HEAD_DIM), F32) for _ in range(B_KV_HEADS)))
    carry = lax.fori_loop(0, nb, past_attend, init)
    _, ls, accs = attend(carry, nb, lambda g: ktn_ref[g], lambda g: vtn_ref[g])
    for g in range(B_KV_HEADS):
        out = accs[g] / ls[g]
        for hh in range(hpg):
            h = g * hpg + hh
            o_ref[:, h * B_HEAD_DIM:(h + 1) * B_HEAD_DIM] = out[hh * t:(hh + 1) * t].astype(BF16)


def _dsa_sample_call(qn, iq, iwc, ikn, ktn, vtn, ikp, ktp, vtp, layer, *, cb):
    b, t, _ = qn.shape
    n_past = ikp.shape[2]
    topk = min(B_TOPK_MAX, (n_past + t) // 4)
    assert n_past % cb == 0 and n_past + t > topk

    def bspec(*shape):
        zeros = (0,) * len(shape)
        return pl.BlockSpec((None,) + shape, lambda bi: (bi,) + zeros)

    def cspec(*shape):
        zeros = (0,) * len(shape)
        return pl.BlockSpec((None, None) + shape, lambda bi: (layer, bi) + zeros)

    return pl.pallas_call(
        functools.partial(_dsa_sample_kernel, t=t, n_past=n_past, cb=cb, topk=topk),
        grid=(b,),
        in_specs=[bspec(t, 512), bspec(t, 256), bspec(IDX_HEADS * t, 1),
                  bspec(cb, IDX_DIM), bspec(B_KV_HEADS, B_HEAD_DIM, cb), bspec(B_KV_HEADS, B_HEAD_DIM, cb),
                  cspec(n_past, IDX_DIM), cspec(B_KV_HEADS, B_HEAD_DIM, n_past),
                  cspec(B_KV_HEADS, B_HEAD_DIM, n_past)],
        out_specs=bspec(t, 512),
        out_shape=jax.ShapeDtypeStruct((b, t, 512), BF16),
        scratch_shapes=[pltpu.VMEM((n_past // cb + 1, t, cb), I32)],
        compiler_params=pltpu.CompilerParams(dimension_semantics=("arbitrary",), vmem_limit_bytes=VMEM_LIMIT),
        name="dsa_sample",
    )(qn, iq, iwc, ikn, ktn, vtn, ikp, ktp, vtp)


SB_WIDE = 256


def _sb_tri2(n):
    r = jnp.arange(2 * n)[:, None] % n
    c = jnp.arange(n)[None, :]
    return jnp.where(r > c, 1.0, 0.0).astype(BF16)


def _sb_stage(zs, pv, tri2_ref, carries, accs, vis):
    sps = [jnp.maximum(z, 0.0) + jnp.log1p(jnp.exp(-jnp.abs(z))) for z in zs]
    stays = [-sp if vis is None else jnp.where(vis, -sp, 0.0) for sp in sps]
    tri2 = tri2_ref[...]
    afters = []
    for ls in stays:
        hi = ls.astype(BF16)
        lo = (ls - hi.astype(F32)).astype(BF16)
        afters.append(_dot(jnp.concatenate([hi, lo], axis=1), tri2))
    new_c, new_a = [], []
    for h in range(C_HEADS):
        w = jnp.exp((zs[h] - sps[h]) + (afters[h] + carries[h]))
        if vis is not None:
            w = jnp.where(vis, w, 0.0)
        new_a.append(accs[h] + pv(h, w.astype(BF16)))
        new_c.append(carries[h] + jnp.sum(stays[h], axis=1, keepdims=True))
    return tuple(new_c), tuple(new_a)


def _sb_max_carry(carries):
    m = carries[0]
    for c in carries[1:]:
        m = jnp.maximum(m, c)
    return jnp.max(m)


def _sb_init(qb, d):
    return (tuple(jnp.zeros((qb, 1), F32) for _ in range(C_HEADS)),
            tuple(jnp.zeros((qb, d), F32) for _ in range(C_HEADS)))


def _sb_prompt_kernel(q_ref, k_ref, v_ref, tri1_ref, tri2_ref, o_ref, *, qb):
    i = pl.program_id(1)
    q = q_ref[...]
    lane = lax.broadcasted_iota(I32, (qb, LANES), 1)
    rows = lax.broadcasted_iota(I32, (qb, LANES), 0)
    q_heads = []
    for h in range(C_HEADS):
        pair = q[:, (h // 2) * LANES:(h // 2 + 1) * LANES]
        q_heads.append(jnp.where(lane // C_HEAD_DIM == h % 2, pair, jnp.zeros_like(pair)))

    def block(off, n, tri_ref, carries, accs, vis):
        k_blk = k_ref[pl.ds(off, n), :]
        v_blk = v_ref[pl.ds(off, n), :]
        zs = [_dot_t(q_heads[h], k_blk[:, (h // 2) * LANES:(h // 2 + 1) * LANES]) for h in range(C_HEADS)]
        pv = lambda h, w: _dot(w, v_blk[:, (h // 2) * LANES:(h // 2 + 1) * LANES])
        return _sb_stage(zs, pv, tri_ref, carries, accs, vis)

    carries, accs = _sb_init(qb, LANES)
    carries, accs = block(pl.multiple_of(i * qb, qb), qb, tri1_ref, carries, accs, lane < rows)

    def cond(state):
        j, mx, _, _ = state
        return (2 * j + 2 <= i) & (mx > SB_SKIP)

    def body(state):
        j, _, carries, accs = state
        off = pl.multiple_of((i - 2 * j - 2) * qb, qb)
        carries, accs = block(off, SB_WIDE, tri2_ref, carries, accs, None)
        return j + 1, _sb_max_carry(carries), carries, accs

    j, mx, carries, accs = lax.while_loop(cond, body, (jnp.int32(0), _sb_max_carry(carries), carries, accs))

    def tail(_):
        return block(0, qb, tri1_ref, carries, accs, None)[1]

    accs = lax.cond((2 * j + 1 == i) & (mx > SB_SKIP), tail, lambda _: accs, 0)
    for p in range(C_HEADS // 2):
        o_ref[:, p * LANES:(p + 1) * LANES] = jnp.where(lane < C_HEAD_DIM, accs[2 * p], accs[2 * p + 1]).astype(BF16)


def _sb_prompt_call(cq, ckb, cvb, lw, *, qb):
    b, s, _ = cq.shape
    qspec = pl.BlockSpec((None, qb, 512), lambda bi, i: (bi, i, 0))
    kspec = pl.BlockSpec((None, s, 512), lambda bi, i: (bi, 0, 0))
    return pl.pallas_call(
        functools.partial(_sb_prompt_kernel, qb=qb),
        grid=(b, s // qb),
        in_specs=[qspec, kspec, kspec, _const_spec(lw['tri1'].shape), _const_spec(lw['tri2'].shape)],
        out_specs=qspec,
        out_shape=jax.ShapeDtypeStruct((b, s, 512), BF16),
        compiler_params=pltpu.CompilerParams(dimension_semantics=("arbitrary", "arbitrary"),
                                             vmem_limit_bytes=VMEM_LIMIT),
        name="stick_prompt",
    )(cq, ckb, cvb, lw['tri1'], lw['tri2'])


def _sb_sample_kernel(q_ref, knt_ref, vnt_ref, kpt_ref, vpt_ref, tri1_ref, tri2_ref, o_ref, *, qb, n_past):
    q = q_ref[...]
    q_heads = [q[:, h * C_HEAD_DIM:(h + 1) * C_HEAD_DIM] for h in range(C_HEADS)]
    lane = lax.broadcasted_iota(I32, (qb, LANES), 1)
    rows = lax.broadcasted_iota(I32, (qb, LANES), 0)

    def block(kt, vt, tri_ref, carries, accs, vis):
        zs = [_dot(q_heads[h], kt(h)) for h in range(C_HEADS)]
        pv = lambda h, w: _dot_t(w, vt(h))
        return _sb_stage(zs, pv, tri_ref, carries, accs, vis)

    carries, accs = _sb_init(qb, C_HEAD_DIM)
    carries, accs = block(lambda h: knt_ref[h], lambda h: vnt_ref[h], tri1_ref, carries, accs, lane < rows)

    def cond(state):
        j, mx, _, _ = state
        return (j < n_past // SB_WIDE) & (mx > SB_SKIP)

    def body(state):
        j, _, carries, accs = state
        off = pl.multiple_of(n_past - (j + 1) * SB_WIDE, SB_WIDE)
        carries, accs = block(lambda h: kpt_ref[h, :, pl.ds(off, SB_WIDE)].astype(BF16),
                              lambda h: vpt_ref[h, :, pl.ds(off, SB_WIDE)].astype(BF16),
                              tri2_ref, carries, accs, None)
        return j + 1, _sb_max_carry(carries), carries, accs

    _, _, _, accs = lax.while_loop(cond, body, (jnp.int32(0), _sb_max_carry(carries), carries, accs))
    for h in range(C_HEADS):
        o_ref[:, h * C_HEAD_DIM:(h + 1) * C_HEAD_DIM] = accs[h].astype(BF16)


def _sb_sample_call(cq, knt, vnt, kpt, vpt, lw, layer):
    b, t, _ = cq.shape
    n_past = kpt.shape[-1]
    qspec = pl.BlockSpec((None, t, 512), lambda bi: (bi, 0, 0))
    nspec = pl.BlockSpec((None, C_HEADS, C_HEAD_DIM, LANES), lambda bi: (bi, 0, 0, 0))
    pspec = pl.BlockSpec((None, None, C_HEADS, C_HEAD_DIM, n_past), lambda bi: (layer, bi, 0, 0, 0))
    return pl.pallas_call(
        functools.partial(_sb_sample_kernel, qb=t, n_past=n_past),
        grid=(b,),
        in_specs=[qspec, nspec, nspec, pspec, pspec, _const_spec(lw['tri1'].shape), _const_spec(lw['tri2'].shape)],
        out_specs=qspec,
        out_shape=jax.ShapeDtypeStruct((b, t, 512), BF16),
        compiler_params=pltpu.CompilerParams(dimension_semantics=("arbitrary",), vmem_limit_bytes=VMEM_LIMIT),
        name="stick_sample",
    )(cq, knt, vnt, kpt, vpt, lw['tri1'], lw['tri2'])


def _merge_kernel(x_ref, nmix_ref, wg_ref, gb_ref, oa_ref, ob_ref, oc_ref, wa_ref, wb_ref, wc_ref, wo_ref, o_ref):
    x = x_ref[...]
    hb = _rms(x, nmix_ref[...]).astype(BF16)
    merged = None
    for idx, (br_ref, w_ref) in enumerate(((oa_ref, wa_ref), (ob_ref, wb_ref), (oc_ref, wc_ref))):
        cs = slice(idx * D_MODEL, (idx + 1) * D_MODEL)
        gate = jax.nn.sigmoid(_dot(hb, wg_ref[:, cs]) + gb_ref[:, cs])
        term = gate * _dot(br_ref[...], w_ref[...])
        merged = term if merged is None else merged + term
    o_ref[...] = x + _dot(merged.astype(BF16), wo_ref[...])


def _merge_call(x, oa, ob, oc, lw, *, tm, name):
    n = x.shape[0]

    def tile(c):
        return pl.BlockSpec((tm, c), lambda i: (i, 0))

    consts_a = [lw['norm_mix'], lw['w_gate'], lw['gate_bias']]
    consts_b = [lw['w_br_a'], lw['w_br_b'], lw['w_br_c'], lw['w_out']]
    return pl.pallas_call(
        _merge_kernel,
        grid=(n // tm,),
        in_specs=([tile(D_MODEL)] + [_const_spec(c.shape) for c in consts_a] + [tile(512)] * 3
                  + [_const_spec(c.shape) for c in consts_b]),
        out_specs=tile(D_MODEL),
        out_shape=jax.ShapeDtypeStruct((n, D_MODEL), F32),
        compiler_params=pltpu.CompilerParams(dimension_semantics=("arbitrary",), vmem_limit_bytes=VMEM_LIMIT),
        name=name,
    )(x, *consts_a, oa, ob, oc, *consts_b)


def _ffn_kernel(x_ref, p_ref, nffn_ref, wfi_ref, wfo_ref, nple_ref, wpg_ref, wpp_ref, o_ref):
    x = x_ref[...]
    hf = _rms(x, nffn_ref[...]).astype(BF16)
    g = _dot(hf, wfi_ref[:, 0:D_FF])
    up = _dot(hf, wfi_ref[:, D_FF:2 * D_FF])
    x = x + _dot((jax.nn.silu(g) * up).astype(BF16), wfo_ref[...])
    hp = _rms(x, nple_ref[...]).astype(BF16)
    gate = jax.nn.sigmoid(_dot(hp, wpg_ref[...]))
    o_ref[...] = x + gate * _dot(p_ref[...].astype(BF16), wpp_ref[...])


def _ffn_call(x, p, lw, *, tm, name):
    n = x.shape[0]
    consts = [lw['norm_ffn'], lw['w_ffn_in'], lw['w_ffn_out'], lw['norm_ple'], lw['w_ple_gate'], lw['w_ple_proj']]
    return pl.pallas_call(
        _ffn_kernel,
        grid=(n // tm,),
        in_specs=[pl.BlockSpec((tm, D_MODEL), lambda i: (i, 0)), pl.BlockSpec((tm, PLE_DIM), lambda i: (i, 0))]
        + [_const_spec(c.shape) for c in consts],
        out_specs=pl.BlockSpec((tm, D_MODEL), lambda i: (i, 0)),
        out_shape=jax.ShapeDtypeStruct((n, D_MODEL), F32),
        compiler_params=pltpu.CompilerParams(dimension_semantics=("arbitrary",), vmem_limit_bytes=VMEM_LIMIT),
        name=name,
    )(x, p, *consts)


def _block_diag_mean(n):
    r = jnp.arange(n)
    return jnp.where((r[:, None] // B_HEAD_DIM) == (r[None, :] // B_HEAD_DIM), 1.0 / B_HEAD_DIM, 0.0).astype(BF16)


def _prep_layer(l, t_sample, w):
    w_in = w['w_in'][l]
    small = jnp.pad(w_in[:, 2048:2088], ((0, 0), (0, LANES - IDX_DIM - IDX_HEADS)))
    w1 = jnp.concatenate([w_in[:, :2048], small, w_in[:, 2088:3624]], axis=1).astype(BF16)
    a_ws = w['a_ws'][l]
    a_bias_full = jnp.repeat(w['a_bias'][l].T, A_GROUP_DIM, axis=1)
    row = lambda a: a.reshape(1, -1)
    return {
        'norm_mix': row(w['norm_mix'][l]), 'w1': w1,
        'w_vt': w_in[:, 1664:1792].T.astype(BF16), 'w_iwt': w_in[:, 2080:2088].T.astype(BF16),
        'a_vnorm': row(w['a_vnorm'][l]),
        'qn': row(jnp.tile(w['b_qnorm'][l], B_HEADS)), 'kn': row(jnp.tile(w['b_knorm'][l], B_KV_HEADS)),
        'g512': _block_diag_mean(512), 'g128': _block_diag_mean(128),
        'tri1': _sb_tri2(LANES), 'tri2': _sb_tri2(SB_WIDE),
        'ma_prompt': a_ws, 'abias_prompt': a_bias_full,
        'ma_sample': jnp.tile(a_ws[:, :t_sample, :t_sample], (1, 16, 16)),
        'abias_sample': jnp.tile(a_bias_full[:t_sample], (16, 1)),
        'w_gate': w_in[:, 3624:].astype(BF16), 'gate_bias': row(w['gate_bias'][l]),
        'w_br_a': w['w_br_a'][l].astype(BF16), 'w_br_b': w['w_br_b'][l].astype(BF16),
        'w_br_c': w['w_br_c'][l].astype(BF16), 'w_out': w['w_out'][l].astype(BF16),
        'norm_ffn': row(w['norm_ffn'][l]), 'w_ffn_in': w['w_ffn_in'][l].astype(BF16),
        'w_ffn_out': w['w_ffn_out'][l].astype(BF16), 'norm_ple': row(w['norm_ple'][l]),
        'w_ple_gate': w['w_ple_gate'][l].astype(BF16), 'w_ple_proj': w['w_ple_proj'][l].astype(BF16),
    }


def _iwt_blocks(iwt, qb):
    return iwt.reshape(IDX_HEADS, -1, qb).transpose(1, 0, 2)


def _prompt_layer(x, p, lw, b, s):
    pr = _proj_call(x, lw, tm=512, ca=A_CHUNK, sample=False)
    r3 = lambda a: a.reshape(b, s, a.shape[-1])
    ob = _dsa_call(r3(pr['qg']), r3(pr['iq']), _iwt_blocks(pr['iwt'], 128), r3(pr['ikb']), r3(pr['bkb']), pr['vt'],
                   qb=128, kb_size=DSA_KB)
    oc = _sb_prompt_call(r3(pr['cq']), r3(pr['ckb']), r3(pr['cvb']), lw, qb=128)
    x = _merge_call(x, pr['oa'], ob.reshape(b * s, 512), oc.reshape(b * s, 512), lw, tm=256, name="merge_prompt")
    x = _ffn_call(x, p, lw, tm=256, name="ffn_prompt")
    return x, pr


def _sample_layer(x, p, lw, b, t, caches, layer):
    pik, pbk_t, pbv_t, pck_t, pcv_t = caches
    pr = _proj_call(x, lw, tm=b * t, ca=b * t, sample=True)
    r3 = lambda a: a.reshape(b, t, a.shape[-1])
    cb = 2 * DSA_KB

    def new_t(a, heads, width):
        at = a.reshape(b, t, heads, -1).transpose(0, 2, 3, 1).astype(BF16)
        return jnp.pad(at, ((0, 0), (0, 0), (0, 0), (0, width - t)))

    ikn = jnp.pad(r3(pr['ikb']), ((0, 0), (0, cb - t), (0, 0)))
    iwc = _iwt_blocks(pr['iwt'], t).reshape(b, IDX_HEADS * t, 1)
    ob = _dsa_sample_call(r3(pr['qh']), r3(pr['iq']), iwc, ikn, new_t(pr['bk'], B_KV_HEADS, cb),
                          new_t(pr['bv'], B_KV_HEADS, cb), pik, pbk_t, pbv_t, layer, cb=cb)
    oc = _sb_sample_call(r3(pr['cq']), new_t(pr['ck'], C_HEADS, LANES), new_t(pr['cv'], C_HEADS, LANES),
                         pck_t, pcv_t, lw, layer)
    x = _merge_call(x, pr['oa'], ob.reshape(b * t, 512), oc.reshape(b * t, 512), lw, tm=b * t, name="merge_sample")
    x = _ffn_call(x, p, lw, tm=b * t, name="ffn_sample")
    return x, pr


def kernel(x_prompt, x_sample, cache_b_k, cache_b_v, cache_b_kidx, cache_c_k, cache_c_v, p_prompt, p_sample,
           norm_mix, w_in, gate_bias, a_vnorm, a_ws, a_bias, b_qnorm, b_knorm, w_br_a, w_br_b, w_br_c, w_out,
           norm_ffn, w_ffn_in, w_ffn_out, norm_ple, w_ple_gate, w_ple_proj):
    weights = dict(norm_mix=norm_mix, w_in=w_in, gate_bias=gate_bias, a_vnorm=a_vnorm, a_ws=a_ws, a_bias=a_bias,
                   b_qnorm=b_qnorm, b_knorm=b_knorm, w_br_a=w_br_a, w_br_b=w_br_b, w_br_c=w_br_c, w_out=w_out,
                   norm_ffn=norm_ffn, w_ffn_in=w_ffn_in, w_ffn_out=w_ffn_out, norm_ple=norm_ple,
                   w_ple_gate=w_ple_gate, w_ple_proj=w_ple_proj)
    depth = w_in.shape[0]
    bp, sp, _ = x_prompt.shape
    bs, ts, _ = x_sample.shape
    yp = x_prompt.reshape(bp * sp, D_MODEL)
    ys = x_sample.reshape(bs * ts, D_MODEL)
    caches = (cache_b_kidx,) + tuple(c.transpose(0, 1, 3, 4, 2) for c in (cache_b_k, cache_b_v, cache_c_k, cache_c_v))
    new_p, new_s = [], []
    for l in range(depth):
        lw = _prep_layer(l, ts, weights)
        yp, pr_p = _prompt_layer(yp, p_prompt[l].reshape(bp * sp, PLE_DIM), lw, bp, sp)
        ys, pr_s = _sample_layer(ys, p_sample[l].reshape(bs * ts, PLE_DIM), lw, bs, ts, caches, l)
        new_p.append(pr_p)
        new_s.append(pr_s)

    def stack(states, name, lead, tail):
        return jnp.stack([s[name].reshape(*lead, *tail) for s in states])

    lp, ls = (bp, sp), (bs, ts)
    return (yp.reshape(bp, sp, D_MODEL), ys.reshape(bs, ts, D_MODEL),
            stack(new_p, 'bk', lp, (B_KV_HEADS, B_HEAD_DIM)), stack(new_p, 'bv', lp, (B_KV_HEADS, B_HEAD_DIM)),
            stack(new_p, 'ik', lp, (IDX_DIM,)),
            stack(new_p, 'ck', lp, (C_HEADS, C_HEAD_DIM)), stack(new_p, 'cv', lp, (C_HEADS, C_HEAD_DIM)),
            stack(new_s, 'bk', ls, (B_KV_HEADS, B_HEAD_DIM)), stack(new_s, 'bv', ls, (B_KV_HEADS, B_HEAD_DIM)),
            stack(new_s, 'ik', ls, (IDX_DIM,)),
            stack(new_s, 'ck', ls, (C_HEADS, C_HEAD_DIM)), stack(new_s, 'cv', ls, (C_HEADS, C_HEAD_DIM)),
            stack(new_s, 'av', ls, (A_HALF,)))
```

```python
import functools

import jax
import jax.numpy as jnp
from jax import lax
from jax.experimental import pallas as pl
from jax.experimental.pallas import tpu as pltpu

F32 = jnp.float32
BF16 = jnp.bfloat16
I32 = jnp.int32

D_MODEL = 1024
CHUNK = 64
EPS = 1e-6
A_CHUNK = 128
A_GROUPS = 4
A_GROUP_DIM = 128
A_HALF = 512
B_HEADS = 8
B_KV_HEADS = 2
B_HEAD_DIM = 64
B_TOPK_MAX = 256
IDX_HEADS = 8
IDX_DIM = 32
C_HEADS = 8
C_HEAD_DIM = 64
N_BRANCH = 3
D_FF = 2816
PLE_DIM = 256

LANES = 128
VMEM_LIMIT = 56 * 1024 * 1024

_C_AU, _C_AV, _C_BQ, _C_BK, _C_BV, _C_IQ = 0, 512, 1024, 1536, 1664, 1792
_C_SM, _C_CQ, _C_CK, _C_CV, _C_END = 2048, 2176, 2688, 3200, 3712
DSA_KB = 512

INT_MIN = -2147483648
NEG_INF_KEY = (-8388608) ^ 0x7FFFFFFF
SB_SKIP = -104.0
LOG2E = 1.4426950408889634


def _const_spec(shape):
    zeros = (0,) * len(shape)
    return pl.BlockSpec(shape, lambda *_: zeros, pipeline_mode=pl.Buffered(1))


def _rms(x, gain):
    return x * lax.rsqrt(jnp.mean(x * x, axis=-1, keepdims=True) + EPS) * gain


def _dot(a, b):
    return jnp.dot(a, b, preferred_element_type=F32)


def _dot_t(a, b):
    return lax.dot_general(a, b, (((1,), (1,)), ((), ())), preferred_element_type=F32)


def _split_dot(x, m_ref):
    hi = x.astype(BF16)
    lo = (x - hi.astype(F32)).astype(BF16)
    m = m_ref[...]
    return _dot(hi, m) + _dot(lo, m)


def _group_rms(z, g_ref, gain):
    return z * lax.rsqrt(_split_dot(z * z, g_ref) + EPS) * gain


def _proj_kernel(x_ref, nmix_ref, w_ref, wvt_ref, wiwt_ref, avn_ref, qn_ref, kn_ref, g512_ref, g128_ref,
                 ma_ref, abias_ref,
                 oa_ref, qg_ref, qh_ref, bk_ref, bkb_ref, bv_ref, vt_ref, iq_ref, ik_ref, ikb_ref, iwt_ref,
                 cq_ref, ck_ref, cv_ref, ckb_ref, cvb_ref, av_ref, *, tm, ca, sample):
    hb = _rms(x_ref[...], nmix_ref[...]).astype(BF16)

    def proj(c0, c1):
        return _dot(hb, w_ref[:, c0:c1])

    qn = _group_rms(proj(_C_BQ, _C_BK), g512_ref, qn_ref[...]) * (B_HEAD_DIM ** -0.5 * LOG2E)
    qh_ref[...] = qn.astype(BF16)
    lane = lax.broadcasted_iota(I32, (tm, LANES), 1)
    for h in range(B_HEADS):
        g = h // (B_HEADS // B_KV_HEADS)
        pair = qn[:, (h // 2) * LANES:(h // 2 + 1) * LANES]
        if h % 2 != g:
            pair = pltpu.roll(pair, B_HEAD_DIM, axis=1)
        qg_ref[:, h * LANES:(h + 1) * LANES] = jnp.where(lane // B_HEAD_DIM == g, pair, 0.0).astype(BF16)
    bk = _group_rms(proj(_C_BK, _C_BV), g128_ref, kn_ref[...])
    bk_ref[...] = bk
    bkb_ref[...] = bk.astype(BF16)
    bv_ref[...] = proj(_C_BV, _C_IQ)
    vt = _dot_t(wvt_ref[...], hb).astype(BF16)
    vw = min(DSA_KB, tm)
    for c in range(tm // vw):
        vt_ref[c] = vt[:, c * vw:(c + 1) * vw]
    iq_ref[...] = proj(_C_IQ, _C_SM).astype(BF16)
    ik = proj(_C_SM, _C_CQ)[:, 0:IDX_DIM]
    ik_ref[...] = ik
    ikb_ref[...] = ik.astype(BF16)
    iwt_ref[...] = _dot_t(wiwt_ref[...], hb) * (1.0 / 16.0)
    cq_ref[...] = (proj(_C_CQ, _C_CK) * (C_HEAD_DIM ** -0.5)).astype(BF16)
    ck = proj(_C_CK, _C_CV)
    ck_ref[...] = ck
    ckb_ref[...] = ck.astype(BF16)
    cv = proj(_C_CV, _C_END)
    cv_ref[...] = cv
    cvb_ref[...] = cv.astype(BF16)

    au = jax.nn.gelu(proj(_C_AU, _C_AV))
    av = _rms(jax.nn.gelu(proj(_C_AV, _C_BQ)), avn_ref[...])
    av_ref[...] = av
    avb = av.astype(BF16)
    row = lax.broadcasted_iota(I32, (ca, ca), 0)
    col = lax.broadcasted_iota(I32, (ca, ca), 1)
    if sample:
        vis = (col // 16) == (row // 16)
    else:
        vis = (col // CHUNK) <= (row // CHUNK)
    for g in range(A_GROUPS):
        wm = jnp.where(vis, ma_ref[g], 0.0).astype(BF16)
        cs = slice(g * A_GROUP_DIM, (g + 1) * A_GROUP_DIM)
        for c in range(tm // ca):
            rs = slice(c * ca, (c + 1) * ca)
            mixed = _dot(wm, avb[rs, cs]) + abias_ref[:, cs]
            oa_ref[rs, cs] = (au[rs, cs] * mixed).astype(BF16)


_PROJ_NAMES = ('oa', 'qg', 'qh', 'bk', 'bkb', 'bv', 'vt', 'iq', 'ik', 'ikb', 'iwt', 'cq', 'ck', 'cv', 'ckb', 'cvb', 'av')


def _proj_call(x, lw, *, tm, ca, sample):
    n = x.shape[0]
    grid = (n // tm,)

    def tile(c):
        return pl.BlockSpec((tm, c), lambda i: (i, 0))

    row_out = {'oa': (A_HALF, BF16), 'qg': (B_HEADS * LANES, BF16), 'qh': (512, BF16), 'bk': (128, F32), 'bkb': (128, BF16),
               'bv': (128, F32), 'iq': (256, BF16), 'ik': (IDX_DIM, F32), 'ikb': (IDX_DIM, BF16),
               'cq': (512, BF16), 'ck': (512, F32), 'cv': (512, F32), 'ckb': (512, BF16), 'cvb': (512, BF16),
               'av': (A_HALF, F32)}
    out_specs, out_shape = [], []
    for name in _PROJ_NAMES:
        if name == 'vt':
            vw = min(DSA_KB, tm)
            out_specs.append(pl.BlockSpec((tm // vw, 128, vw), lambda i: (i, 0, 0)))
            out_shape.append(jax.ShapeDtypeStruct((n // vw, 128, vw), BF16))
        elif name == 'iwt':
            out_specs.append(pl.BlockSpec((IDX_HEADS, tm), lambda i: (0, i)))
            out_shape.append(jax.ShapeDtypeStruct((IDX_HEADS, n), F32))
        else:
            c, dt = row_out[name]
            out_specs.append(tile(c))
            out_shape.append(jax.ShapeDtypeStruct((n, c), dt))
    ma = lw['ma_sample'] if sample else lw['ma_prompt']
    abias = lw['abias_sample'] if sample else lw['abias_prompt']
    consts = [lw['norm_mix'], lw['w1'], lw['w_vt'], lw['w_iwt'], lw['a_vnorm'], lw['qn'], lw['kn'],
              lw['g512'], lw['g128'], ma, abias]
    outs = pl.pallas_call(
        functools.partial(_proj_kernel, tm=tm, ca=ca, sample=sample),
        grid=grid,
        in_specs=[tile(D_MODEL)] + [_const_spec(c.shape) for c in consts],
        out_specs=out_specs,
        out_shape=out_shape,
        compiler_params=pltpu.CompilerParams(dimension_semantics=("arbitrary",), vmem_limit_bytes=VMEM_LIMIT),
        name="proj_sample" if sample else "proj_prompt",
    )(x, *consts)
    return dict(zip(_PROJ_NAMES, outs))


def _dsa_kernel(qg_ref, iq_ref, iwt_ref, ik_ref, k_ref, vt_ref, o_ref, keys_ref,
                *, qb, kb_size, n_valid, topk):
    q0 = pl.program_id(1) * qb
    nkb = (q0 + qb + kb_size - 1) // kb_size

    cb_size = kb_size
    ncb = nkb
    kpos0 = lax.broadcasted_iota(I32, (kb_size, qb), 0)
    cpos0 = lax.broadcasted_iota(I32, (cb_size, qb), 0)
    qpos_row = q0 + lax.broadcasted_iota(I32, (1, qb), 1)
    qchunk_row = qpos_row // CHUNK
    n_adm = jnp.minimum((qchunk_row + 1) * CHUNK, n_valid)

    def fold(x, op):
        return op(op(x.reshape(kb_size // 8, 8, qb), axis=0), axis=0, keepdims=True)

    iq = iq_ref[...]
    iq_rows = jnp.concatenate([iq[:, h * IDX_DIM:(h + 1) * IDX_DIM] for h in range(IDX_HEADS)], axis=0)
    iwt = iwt_ref[...]

    def score_body(cb, carry):
        off = pl.multiple_of(cb * cb_size, cb_size)
        d = _dot_t(ik_ref[pl.ds(off, cb_size), :], iq_rows)
        acc = jnp.zeros((cb_size, qb), F32)
        for h in range(IDX_HEADS):
            acc = acc + jnp.maximum(d[:, h * qb:(h + 1) * qb], 0.0) * iwt[h:h + 1, :]
        kpos = off + cpos0
        adm = (kpos // CHUNK <= qchunk_row) & (kpos < n_valid)
        bits = pltpu.bitcast(jnp.where(adm, acc, -jnp.inf), I32)
        keys_ref[pl.ds(off, cb_size), :] = jnp.where(bits < 0, bits ^ 0x7FFFFFFF, bits)
        return carry

    lax.fori_loop(0, ncb, score_body, 0)

    acc_rows = 32

    def count(pred):
        def body(cb, cnt):
            off = pl.multiple_of(cb * cb_size, cb_size)
            ind = jnp.where(pred(keys_ref[pl.ds(off, cb_size), :], off), 1.0, 0.0)
            return cnt + jnp.sum(ind.reshape(cb_size // acc_rows, acc_rows, qb), axis=0)
        cnt = lax.fori_loop(0, ncb, body, jnp.zeros((acc_rows, qb), F32))
        return jnp.sum(jnp.sum(cnt.reshape(acc_rows // 8, 8, qb), axis=0), axis=0, keepdims=True)

    k_f = float(topk)
    few = n_adm <= topk
    bits_per_trip = 4

    def sel_cond(state):
        t, _, _, pending = state
        return (t < 32) & (pending > 0.0)

    def sel_body(state):
        t, ans, c_ans, _ = state
        for e in range(bits_per_trip):
            cand_u = ans | lax.shift_left(jnp.int32(1), 31 - e - t)
            cand = cand_u ^ INT_MIN
            c = count(lambda blk, off: blk >= cand)
            take = c >= k_f
            ans = jnp.where(take, cand_u, ans)
            c_ans = jnp.where(take, c, c_ans)
        pending = jnp.max(jnp.where(few | (c_ans == k_f), 0.0, 1.0))
        return t + bits_per_trip, ans, c_ans, pending

    _, ans, _, pending = lax.while_loop(
        sel_cond, sel_body,
        (jnp.int32(0), jnp.zeros((1, qb), I32), jnp.full((1, qb), 2.0 * k_f + 1.0, F32), jnp.float32(1.0)))
    thr = jnp.where(few, NEG_INF_KEY, ans ^ INT_MIN)

    def tie_break(_):
        need = k_f - count(lambda blk, off: blk > thr)

        def idx_bit_body(t, jans):
            cand = jans | lax.shift_left(jnp.int32(1), 13 - t)
            c = count(lambda blk, off: (blk == thr) & (off + cpos0 < cand))
            return jnp.where(c < need, cand, jans)
        return lax.fori_loop(0, 14, idx_bit_body, jnp.zeros((1, qb), I32))

    jmax = lax.cond(pending > 0.0, tie_break, lambda _: jnp.full((1, qb), 2 ** 30, I32), 0)

    qg = qg_ref[...]
    q_pairs = [jnp.concatenate([qg[:, (2 * p) * LANES:(2 * p + 1) * LANES],
                                qg[:, (2 * p + 1) * LANES:(2 * p + 2) * LANES]], axis=0)
               for p in range(B_HEADS // 2)]
    hpg = B_HEADS // B_KV_HEADS

    def att_body(kb, carry):
        ms, ls, accs = carry
        off = pl.multiple_of(kb * kb_size, kb_size)
        key_blk = keys_ref[pl.ds(off, kb_size), :]
        kpos = off + kpos0
        sel = ((key_blk > thr) | ((key_blk == thr) & (kpos <= jmax))) & (key_blk > NEG_INF_KEY)
        bias = jnp.where(sel, 0.0, -jnp.inf)
        k_blk = k_ref[pl.ds(off, kb_size), :]
        vt_blk = vt_ref[kb]
        new_m, new_l, new_a = list(ms), list(ls), list(accs)
        sts = [_dot_t(k_blk, q_pairs[p]) for p in range(B_HEADS // 2)]
        pts, alphas = [], []
        for h in range(B_HEADS):
            s = sts[h // 2][:, (h % 2) * qb:(h % 2 + 1) * qb] + bias
            m_new = jnp.maximum(ms[h], fold(s, jnp.max))
            m_safe = jnp.where(m_new == -jnp.inf, 0.0, m_new)
            pt = jnp.exp2(s - m_safe)
            alpha = jnp.exp2(ms[h] - m_safe)
            new_m[h] = m_new
            new_l[h] = alpha * ls[h] + fold(pt, jnp.sum)
            pts.append(pt.astype(BF16))
            alphas.append(alpha)
        for h in range(B_HEADS):
            g = h // hpg
            new_a[h] = alphas[h] * accs[h] + _dot(vt_blk[g * B_HEAD_DIM:(g + 1) * B_HEAD_DIM, :], pts[h])
        return tuple(new_m), tuple(new_l), tuple(new_a)

    init = (tuple(jnp.full((1, qb), -jnp.inf, F32) for _ in range(B_HEADS)),
            tuple(jnp.zeros((1, qb), F32) for _ in range(B_HEADS)),
            tuple(jnp.zeros((B_HEAD_DIM, qb), F32) for _ in range(B_HEADS)))
    ms, ls, accs = lax.fori_loop(0, nkb, att_body, init)
    out_t = jnp.concatenate([accs[h] / ls[h] for h in range(B_HEADS)], axis=0)
    o_ref[...] = out_t.T.astype(BF16)


def _dsa_call(qg, iq, iwt, ik, k, vt, *, qb, kb_size):
    b, sq, _ = qg.shape
    lp = ik.shape[1]
    nq = sq // qb
    nkb = lp // kb_size
    topk = min(B_TOPK_MAX, lp // 4)

    def qspec(c):
        return pl.BlockSpec((None, qb, c), lambda bi, i: (bi, i, 0))

    def kspec(c):
        return pl.BlockSpec((None, lp, c), lambda bi, i: (bi, 0, 0))

    return pl.pallas_call(
        functools.partial(_dsa_kernel, qb=qb, kb_size=kb_size, n_valid=lp, topk=topk),
        grid=(b, nq),
        in_specs=[qspec(B_HEADS * LANES), qspec(256),
                  pl.BlockSpec((None, IDX_HEADS, qb), lambda bi, i: (bi * nq + i, 0, 0)),
                  kspec(IDX_DIM), kspec(128),
                  pl.BlockSpec((nkb, 128, kb_size), lambda bi, i: (bi, 0, 0))],
        out_specs=qspec(512),
        out_shape=jax.ShapeDtypeStruct((b, sq, 512), BF16),
        scratch_shapes=[pltpu.VMEM((lp, qb), I32)],
        compiler_params=pltpu.CompilerParams(dimension_semantics=("arbitrary", "arbitrary"),
                                             vmem_limit_bytes=VMEM_LIMIT),
        name="dsa_prompt",
    )(qg, iq, iwt, ik, k, vt)


def _dsa_sample_kernel(qn_ref, iq_ref, iwc_ref, ikn_ref, ktn_ref, vtn_ref, ikp_ref, ktp_ref, vtp_ref, o_ref,
                       keys_ref, *, t, n_past, cb, topk):
    nb = n_past // cb
    hpg = B_HEADS // B_KV_HEADS
    lanes = lax.broadcasted_iota(I32, (t, cb), 1)
    iq = iq_ref[...]
    iq_rows = jnp.concatenate([iq[:, h * IDX_DIM:(h + 1) * IDX_DIM] for h in range(IDX_HEADS)], axis=0)
    iwc = iwc_ref[...]

    def to_keys(ik_blk, valid):
        w = jnp.maximum(_dot_t(iq_rows, ik_blk), 0.0) * iwc
        s = w[0:t]
        for h in range(1, IDX_HEADS):
            s = s + w[h * t:(h + 1) * t]
        if valid is not None:
            s = jnp.where(valid, s, -jnp.inf)
        bits = pltpu.bitcast(s, I32)
        return jnp.where(bits < 0, bits ^ 0x7FFFFFFF, bits)

    def past_scores(kb, carry):
        off = pl.multiple_of(kb * cb, cb)
        keys_ref[kb] = to_keys(ikp_ref[pl.ds(off, cb), :].astype(BF16), None)
        return carry

    lax.fori_loop(0, nb, past_scores, 0)
    keys_ref[nb] = to_keys(ikn_ref[...], lanes < t)

    def count(pred):
        def body(kb, cnt):
            ind = jnp.where(pred(keys_ref[kb], kb * cb + lanes), 1.0, 0.0)
            part = ind[:, 0:LANES]
            for c in range(1, cb // LANES):
                part = part + ind[:, c * LANES:(c + 1) * LANES]
            return cnt + part
        cnt = lax.fori_loop(0, nb + 1, body, jnp.zeros((t, LANES), F32))
        return jnp.sum(cnt, axis=1, keepdims=True)

    k_f = float(topk)
    bits_per_trip = 4

    def sel_cond(state):
        tt, _, _, pending = state
        return (tt < 32) & (pending > 0.0)

    def sel_body(state):
        tt, ans, c_ans, _ = state
        for e in range(bits_per_trip):
            cand_u = ans | lax.shift_left(jnp.int32(1), 31 - e - tt)
            cand = cand_u ^ INT_MIN
            c = count(lambda blk, pos: blk >= cand)
            take = c >= k_f
            ans = jnp.where(take, cand_u, ans)
            c_ans = jnp.where(take, c, c_ans)
        pending = jnp.max(jnp.where(c_ans == k_f, 0.0, 1.0))
        return tt + bits_per_trip, ans, c_ans, pending

    _, ans, _, pending = lax.while_loop(
        sel_cond, sel_body,
        (jnp.int32(0), jnp.zeros((t, 1), I32), jnp.full((t, 1), 2.0 * k_f + 1.0, F32), jnp.float32(1.0)))
    thr = ans ^ INT_MIN

    def tie_break(_):
        need = k_f - count(lambda blk, pos: blk > thr)

        def idx_bit_body(tt, jans):
            cand = jans | lax.shift_left(jnp.int32(1), 13 - tt)
            c = count(lambda blk, pos: (blk == thr) & (pos < cand))
            return jnp.where(c < need, cand, jans)
        return lax.fori_loop(0, 14, idx_bit_body, jnp.zeros((t, 1), I32))

    jmax = lax.cond(pending > 0.0, tie_break, lambda _: jnp.full((t, 1), 2 ** 30, I32), 0)

    qn = qn_ref[...]
    q_groups = [jnp.concatenate([qn[:, h * B_HEAD_DIM:(h + 1) * B_HEAD_DIM]
                                 for h in range(g * hpg, (g + 1) * hpg)], axis=0)
                for g in range(B_KV_HEADS)]

    def attend(carry, kb, kt, vt):
        ms, ls, accs = carry
        key_blk = keys_ref[kb]
        pos = kb * cb + lanes
        sel = ((key_blk > thr) | ((key_blk == thr) & (pos <= jmax))) & (key_blk > NEG_INF_KEY)
        bias = jnp.where(sel, 0.0, -jnp.inf)
        bias_rows = jnp.concatenate([bias] * hpg, axis=0)
        new_m, new_l, new_a = [], [], []
        for g in range(B_KV_HEADS):
            s = _dot(q_groups[g], kt(g)) + bias_rows
            m_new = jnp.maximum(ms[g], jnp.max(s, axis=1, keepdims=True))
            m_safe = jnp.where(m_new == -jnp.inf, 0.0, m_new)
            p = jnp.exp2(s - m_safe)
            alpha = jnp.exp2(ms[g] - m_safe)
            new_m.append(m_new)
            new_l.append(alpha * ls[g] + jnp.sum(p, axis=1, keepdims=True))
            new_a.append(alpha * accs[g] + _dot_t(p.astype(BF16), vt(g)))
        return tuple(new_m), tuple(new_l), tuple(new_a)

    def past_attend(kb, carry):
        off = pl.multiple_of(kb * cb, cb)
        return attend(carry, kb, lambda g: ktp_ref[g, :, pl.ds(off, cb)].astype(BF16),
                      lambda g: vtp_ref[g, :, pl.ds(off, cb)].astype(BF16))

    init = (tuple(jnp.full((hpg * t, 1), -jnp.inf, F32) for _ in range(B_KV_HEADS)),
            tuple(jnp.zeros((hpg * t, 1), F32) for _ in range(B_KV_HEADS)),
            tuple(jnp.zeros((hpg * t, B_HEAD_DIM), F32) for _ in range(B_KV_HEADS)))
    carry = lax.fori_loop(0, nb, past_attend, init)
    _, ls, accs = attend(carry, nb, lambda g: ktn_ref[g], lambda g: vtn_ref[g])
    for g in range(B_KV_HEADS):
        out = accs[g] / ls[g]
        for hh in range(hpg):
            h = g * hpg + hh
            o_ref[:, h * B_HEAD_DIM:(h + 1) * B_HEAD_DIM] = out[hh * t:(hh + 1) * t].astype(BF16)


def _dsa_sample_call(qn, iq, iwc, ikn, ktn, vtn, ikp, ktp, vtp, layer, *, cb):
    b, t, _ = qn.shape
    n_past = ikp.shape[2]
    topk = min(B_TOPK_MAX, (n_past + t) // 4)
    assert n_past % cb == 0 and n_past + t > topk

    def bspec(*shape):
        zeros = (0,) * len(shape)
        return pl.BlockSpec((None,) + shape, lambda bi: (bi,) + zeros)

    def cspec(*shape):
        zeros = (0,) * len(shape)
        return pl.BlockSpec((None, None) + shape, lambda bi: (layer, bi) + zeros)

    return pl.pallas_call(
        functools.partial(_dsa_sample_kernel, t=t, n_past=n_past, cb=cb, topk=topk),
        grid=(b,),
        in_specs=[bspec(t, 512), bspec(t, 256), bspec(IDX_HEADS * t, 1),
                  bspec(cb, IDX_DIM), bspec(B_KV_HEADS, B_HEAD_DIM, cb), bspec(B_KV_HEADS, B_HEAD_DIM, cb),
                  cspec(n_past, IDX_DIM), cspec(B_KV_HEADS, B_HEAD_DIM, n_past),
                  cspec(B_KV_HEADS, B_HEAD_DIM, n_past)],
        out_specs=bspec(t, 512),
        out_shape=jax.ShapeDtypeStruct((b, t, 512), BF16),
        scratch_shapes=[pltpu.VMEM((n_past // cb + 1, t, cb), I32)],
        compiler_params=pltpu.CompilerParams(dimension_semantics=("arbitrary",), vmem_limit_bytes=VMEM_LIMIT),
        name="dsa_sample",
    )(qn, iq, iwc, ikn, ktn, vtn, ikp, ktp, vtp)


SB_WIDE = 256
SB_GROUP = 4


def _sb_tri2(n):
    r = jnp.arange(2 * n)[:, None] % n
    c = jnp.arange(n)[None, :]
    return jnp.where(r > c, 1.0, 0.0).astype(BF16)


def _sb_stage(z_fn, pv, tri2_ref, carries, accs, vis):
    tri2 = tri2_ref[...]
    new_c, new_a = list(carries), list(accs)
    for h0 in range(0, C_HEADS, SB_GROUP):
        heads = range(h0, h0 + SB_GROUP)
        zs = [z_fn(h) for h in heads]
        sps = [jnp.maximum(z, 0.0) + jnp.log(1.0 + jnp.exp(-jnp.abs(z))) for z in zs]
        stays = [-sp if vis is None else jnp.where(vis, -sp, 0.0) for sp in sps]
        afters = []
        for ls in stays:
            hi = ls.astype(BF16)
            lo = (ls - hi.astype(F32)).astype(BF16)
            afters.append(_dot(jnp.concatenate([hi, lo], axis=1), tri2))
        for i, h in enumerate(heads):
            w = jnp.exp((zs[i] - sps[i]) + (afters[i] + carries[h]))
            if vis is not None:
                w = jnp.where(vis, w, 0.0)
            new_a[h] = accs[h] + pv(h, w.astype(BF16))
            new_c[h] = carries[h] + jnp.sum(stays[i], axis=1, keepdims=True)
    return tuple(new_c), tuple(new_a)


def _sb_max_carry(carries):
    m = carries[0]
    for c in carries[1:]:
        m = jnp.maximum(m, c)
    return jnp.max(m)


def _sb_init(qb, d):
    return (tuple(jnp.zeros((qb, 1), F32) for _ in range(C_HEADS)),
            tuple(jnp.zeros((qb, d), F32) for _ in range(C_HEADS)))


def _sb_prompt_kernel(q_ref, k_ref, v_ref, tri1_ref, tri2_ref, o_ref, *, qb):
    i = pl.program_id(1)
    q = q_ref[...]
    lane = lax.broadcasted_iota(I32, (qb, LANES), 1)
    rows = lax.broadcasted_iota(I32, (qb, LANES), 0)
    q_heads = []
    for h in range(C_HEADS):
        pair = q[:, (h // 2) * LANES:(h // 2 + 1) * LANES]
        q_heads.append(jnp.where(lane // C_HEAD_DIM == h % 2, pair, jnp.zeros_like(pair)))

    def block(off, n, tri_ref, carries, accs, vis):
        k_blk = k_ref[pl.ds(off, n), :]
        v_blk = v_ref[pl.ds(off, n), :]
        z_fn = lambda h: _dot_t(q_heads[h], k_blk[:, (h // 2) * LANES:(h // 2 + 1) * LANES])
        pv = lambda h, w: _dot(w, v_blk[:, (h // 2) * LANES:(h // 2 + 1) * LANES])
        return _sb_stage(z_fn, pv, tri_ref, carries, accs, vis)

    carries, accs = _sb_init(qb, LANES)
    carries, accs = block(pl.multiple_of(i * qb, qb), qb, tri1_ref, carries, accs, lane < rows)

    def cond(state):
        j, mx, _, _ = state
        return (2 * j + 2 <= i) & (mx > SB_SKIP)

    def body(state):
        j, _, carries, accs = state
        off = pl.multiple_of((i - 2 * j - 2) * qb, qb)
        carries, accs = block(off, SB_WIDE, tri2_ref, carries, accs, None)
        return j + 1, _sb_max_carry(carries), carries, accs

    j, mx, carries, accs = lax.while_loop(cond, body, (jnp.int32(0), _sb_max_carry(carries), carries, accs))

    def tail(_):
        return block(0, qb, tri1_ref, carries, accs, None)[1]

    accs = lax.cond((2 * j + 1 == i) & (mx > SB_SKIP), tail, lambda _: accs, 0)
    for p in range(C_HEADS // 2):
        o_ref[:, p * LANES:(p + 1) * LANES] = jnp.where(lane < C_HEAD_DIM, accs[2 * p], accs[2 * p + 1]).astype(BF16)


def _sb_prompt_call(cq, ckb, cvb, lw, *, qb):
    b, s, _ = cq.shape
    qspec = pl.BlockSpec((None, qb, 512), lambda bi, i: (bi, i, 0))
    kspec = pl.BlockSpec((None, s, 512), lambda bi, i: (bi, 0, 0))
    return pl.pallas_call(
        functools.partial(_sb_prompt_kernel, qb=qb),
        grid=(b, s // qb),
        in_specs=[qspec, kspec, kspec, _const_spec(lw['tri1'].shape), _const_spec(lw['tri2'].shape)],
        out_specs=qspec,
        out_shape=jax.ShapeDtypeStruct((b, s, 512), BF16),
        compiler_params=pltpu.CompilerParams(dimension_semantics=("arbitrary", "arbitrary"),
                                             vmem_limit_bytes=VMEM_LIMIT),
        name="stick_prompt",
    )(cq, ckb, cvb, lw['tri1'], lw['tri2'])


def _sb_sample_kernel(q_ref, knt_ref, vnt_ref, kpt_ref, vpt_ref, tri1_ref, tri2_ref, o_ref, *, qb, n_past):
    q = q_ref[...]
    q_heads = [q[:, h * C_HEAD_DIM:(h + 1) * C_HEAD_DIM] for h in range(C_HEADS)]
    lane = lax.broadcasted_iota(I32, (qb, LANES), 1)
    rows = lax.broadcasted_iota(I32, (qb, LANES), 0)

    def block(kt, vt, tri_ref, carries, accs, vis):
        z_fn = lambda h: _dot(q_heads[h], kt(h))
        pv = lambda h, w: _dot_t(w, vt(h))
        return _sb_stage(z_fn, pv, tri_ref, carries, accs, vis)

    carries, accs = _sb_init(qb, C_HEAD_DIM)
    carries, accs = block(lambda h: knt_ref[h], lambda h: vnt_ref[h], tri1_ref, carries, accs, lane < rows)

    def cond(state):
        j, mx, _, _ = state
        return (j < n_past // SB_WIDE) & (mx > SB_SKIP)

    def body(state):
        j, _, carries, accs = state
        off = pl.multiple_of(n_past - (j + 1) * SB_WIDE, SB_WIDE)
        carries, accs = block(lambda h: kpt_ref[h, :, pl.ds(off, SB_WIDE)].astype(BF16),
                              lambda h: vpt_ref[h, :, pl.ds(off, SB_WIDE)].astype(BF16),
                              tri2_ref, carries, accs, None)
        return j + 1, _sb_max_carry(carries), carries, accs

    _, _, _, accs = lax.while_loop(cond, body, (jnp.int32(0), _sb_max_carry(carries), carries, accs))
    for h in range(C_HEADS):
        o_ref[:, h * C_HEAD_DIM:(h + 1) * C_HEAD_DIM] = accs[h].astype(BF16)


def _sb_sample_call(cq, knt, vnt, kpt, vpt, lw, layer):
    b, t, _ = cq.shape
    n_past = kpt.shape[-1]
    qspec = pl.BlockSpec((None, t, 512), lambda bi: (bi, 0, 0))
    nspec = pl.BlockSpec((None, C_HEADS, C_HEAD_DIM, LANES), lambda bi: (bi, 0, 0, 0))
    pspec = pl.BlockSpec((None, None, C_HEADS, C_HEAD_DIM, n_past), lambda bi: (layer, bi, 0, 0, 0))
    return pl.pallas_call(
        functools.partial(_sb_sample_kernel, qb=t, n_past=n_past),
        grid=(b,),
        in_specs=[qspec, nspec, nspec, pspec, pspec, _const_spec(lw['tri1'].shape), _const_spec(lw['tri2'].shape)],
        out_specs=qspec,
        out_shape=jax.ShapeDtypeStruct((b, t, 512), BF16),
        compiler_params=pltpu.CompilerParams(dimension_semantics=("arbitrary",), vmem_limit_bytes=VMEM_LIMIT),
        name="stick_sample",
    )(cq, knt, vnt, kpt, vpt, lw['tri1'], lw['tri2'])


def _merge_kernel(x_ref, nmix_ref, wg_ref, gb_ref, oa_ref, ob_ref, oc_ref, wa_ref, wb_ref, wc_ref, wo_ref, o_ref):
    x = x_ref[...]
    hb = _rms(x, nmix_ref[...]).astype(BF16)
    merged = None
    for idx, (br_ref, w_ref) in enumerate(((oa_ref, wa_ref), (ob_ref, wb_ref), (oc_ref, wc_ref))):
        cs = slice(idx * D_MODEL, (idx + 1) * D_MODEL)
        gate = jax.nn.sigmoid(_dot(hb, wg_ref[:, cs]) + gb_ref[:, cs])
        term = gate * _dot(br_ref[...], w_ref[...])
        merged = term if merged is None else merged + term
    o_ref[...] = x + _dot(merged.astype(BF16), wo_ref[...])


def _merge_call(x, oa, ob, oc, lw, *, tm, name):
    n = x.shape[0]

    def tile(c):
        return pl.BlockSpec((tm, c), lambda i: (i, 0))

    consts_a = [lw['norm_mix'], lw['w_gate'], lw['gate_bias']]
    consts_b = [lw['w_br_a'], lw['w_br_b'], lw['w_br_c'], lw['w_out']]
    return pl.pallas_call(
        _merge_kernel,
        grid=(n // tm,),
        in_specs=([tile(D_MODEL)] + [_const_spec(c.shape) for c in consts_a] + [tile(512)] * 3
                  + [_const_spec(c.shape) for c in consts_b]),
        out_specs=tile(D_MODEL),
        out_shape=jax.ShapeDtypeStruct((n, D_MODEL), F32),
        compiler_params=pltpu.CompilerParams(dimension_semantics=("arbitrary",), vmem_limit_bytes=VMEM_LIMIT),
        name=name,
    )(x, *consts_a, oa, ob, oc, *consts_b)


def _ffn_kernel(x_ref, p_ref, nffn_ref, wfi_ref, wfo_ref, nple_ref, wpg_ref, wpp_ref, o_ref):
    x = x_ref[...]
    hf = _rms(x, nffn_ref[...]).astype(BF16)
    g = _dot(hf, wfi_ref[:, 0:D_FF])
    up = _dot(hf, wfi_ref[:, D_FF:2 * D_FF])
    x = x + _dot((jax.nn.silu(g) * up).astype(BF16), wfo_ref[...])
    hp = _rms(x, nple_ref[...]).astype(BF16)
    gate = jax.nn.sigmoid(_dot(hp, wpg_ref[...]))
    o_ref[...] = x + gate * _dot(p_ref[...].astype(BF16), wpp_ref[...])


def _ffn_call(x, p, lw, *, tm, name):
    n = x.shape[0]
    consts = [lw['norm_ffn'], lw['w_ffn_in'], lw['w_ffn_out'], lw['norm_ple'], lw['w_ple_gate'], lw['w_ple_proj']]
    return pl.pallas_call(
        _ffn_kernel,
        grid=(n // tm,),
        in_specs=[pl.BlockSpec((tm, D_MODEL), lambda i: (i, 0)), pl.BlockSpec((tm, PLE_DIM), lambda i: (i, 0))]
        + [_const_spec(c.shape) for c in consts],
        out_specs=pl.BlockSpec((tm, D_MODEL), lambda i: (i, 0)),
        out_shape=jax.ShapeDtypeStruct((n, D_MODEL), F32),
        compiler_params=pltpu.CompilerParams(dimension_semantics=("arbitrary",), vmem_limit_bytes=VMEM_LIMIT),
        name=name,
    )(x, p, *consts)


def _block_diag_mean(n):
    r = jnp.arange(n)
    return jnp.where((r[:, None] // B_HEAD_DIM) == (r[None, :] // B_HEAD_DIM), 1.0 / B_HEAD_DIM, 0.0).astype(BF16)


def _prep_layer(l, t_sample, w):
    w_in = w['w_in'][l]
    small = jnp.pad(w_in[:, 2048:2088], ((0, 0), (0, LANES - IDX_DIM - IDX_HEADS)))
    w1 = jnp.concatenate([w_in[:, :2048], small, w_in[:, 2088:3624]], axis=1).astype(BF16)
    a_ws = w['a_ws'][l]
    a_bias_full = jnp.repeat(w['a_bias'][l].T, A_GROUP_DIM, axis=1)
    row = lambda a: a.reshape(1, -1)
    return {
        'norm_mix': row(w['norm_mix'][l]), 'w1': w1,
        'w_vt': w_in[:, 1664:1792].T.astype(BF16), 'w_iwt': w_in[:, 2080:2088].T.astype(BF16),
        'a_vnorm': row(w['a_vnorm'][l]),
        'qn': row(jnp.tile(w['b_qnorm'][l], B_HEADS)), 'kn': row(jnp.tile(w['b_knorm'][l], B_KV_HEADS)),
        'g512': _block_diag_mean(512), 'g128': _block_diag_mean(128),
        'tri1': _sb_tri2(LANES), 'tri2': _sb_tri2(SB_WIDE),
        'ma_prompt': a_ws, 'abias_prompt': a_bias_full,
        'ma_sample': jnp.tile(a_ws[:, :t_sample, :t_sample], (1, 16, 16)),
        'abias_sample': jnp.tile(a_bias_full[:t_sample], (16, 1)),
        'w_gate': w_in[:, 3624:].astype(BF16), 'gate_bias': row(w['gate_bias'][l]),
        'w_br_a': w['w_br_a'][l].astype(BF16), 'w_br_b': w['w_br_b'][l].astype(BF16),
        'w_br_c': w['w_br_c'][l].astype(BF16), 'w_out': w['w_out'][l].astype(BF16),
        'norm_ffn': row(w['norm_ffn'][l]), 'w_ffn_in': w['w_ffn_in'][l].astype(BF16),
        'w_ffn_out': w['w_ffn_out'][l].astype(BF16), 'norm_ple': row(w['norm_ple'][l]),
        'w_ple_gate': w['w_ple_gate'][l].astype(BF16), 'w_ple_proj': w['w_ple_proj'][l].astype(BF16),
    }


def _iwt_blocks(iwt, qb):
    return iwt.reshape(IDX_HEADS, -1, qb).transpose(1, 0, 2)


def _prompt_layer(x, p, lw, b, s):
    pr = _proj_call(x, lw, tm=512, ca=A_CHUNK, sample=False)
    r3 = lambda a: a.reshape(b, s, a.shape[-1])
    ob = _dsa_call(r3(pr['qg']), r3(pr['iq']), _iwt_blocks(pr['iwt'], 128), r3(pr['ikb']), r3(pr['bkb']), pr['vt'],
                   qb=128, kb_size=DSA_KB)
    oc = _sb_prompt_call(r3(pr['cq']), r3(pr['ckb']), r3(pr['cvb']), lw, qb=128)
    x = _merge_call(x, pr['oa'], ob.reshape(b * s, 512), oc.reshape(b * s, 512), lw, tm=256, name="merge_prompt")
    x = _ffn_call(x, p, lw, tm=256, name="ffn_prompt")
    return x, pr


def _sample_layer(x, p, lw, b, t, caches, layer):
    pik, pbk_t, pbv_t, pck_t, pcv_t = caches
    pr = _proj_call(x, lw, tm=b * t, ca=b * t, sample=True)
    r3 = lambda a: a.reshape(b, t, a.shape[-1])
    cb = DSA_KB

    def new_t(a, heads, width):
        at = a.reshape(b, t, heads, -1).transpose(0, 2, 3, 1).astype(BF16)
        return jnp.pad(at, ((0, 0), (0, 0), (0, 0), (0, width - t)))

    ikn = jnp.pad(r3(pr['ikb']), ((0, 0), (0, cb - t), (0, 0)))
    iwc = _iwt_blocks(pr['iwt'], t).reshape(b, IDX_HEADS * t, 1)
    ob = _dsa_sample_call(r3(pr['qh']), r3(pr['iq']), iwc, ikn, new_t(pr['bk'], B_KV_HEADS, cb),
                          new_t(pr['bv'], B_KV_HEADS, cb), pik, pbk_t, pbv_t, layer, cb=cb)
    oc = _sb_sample_call(r3(pr['cq']), new_t(pr['ck'], C_HEADS, LANES), new_t(pr['cv'], C_HEADS, LANES),
                         pck_t, pcv_t, lw, layer)
    x = _merge_call(x, pr['oa'], ob.reshape(b * t, 512), oc.reshape(b * t, 512), lw, tm=b * t, name="merge_sample")
    x = _ffn_call(x, p, lw, tm=b * t, name="ffn_sample")
    return x, pr


def kernel(x_prompt, x_sample, cache_b_k, cache_b_v, cache_b_kidx, cache_c_k, cache_c_v, p_prompt, p_sample,
           norm_mix, w_in, gate_bias, a_vnorm, a_ws, a_bias, b_qnorm, b_knorm, w_br_a, w_br_b, w_br_c, w_out,
           norm_ffn, w_ffn_in, w_ffn_out, norm_ple, w_ple_gate, w_ple_proj):
    weights = dict(norm_mix=norm_mix, w_in=w_in, gate_bias=gate_bias, a_vnorm=a_vnorm, a_ws=a_ws, a_bias=a_bias,
                   b_qnorm=b_qnorm, b_knorm=b_knorm, w_br_a=w_br_a, w_br_b=w_br_b, w_br_c=w_br_c, w_out=w_out,
                   norm_ffn=norm_ffn, w_ffn_in=w_ffn_in, w_ffn_out=w_ffn_out, norm_ple=norm_ple,
                   w_ple_gate=w_ple_gate, w_ple_proj=w_ple_proj)
    depth = w_in.shape[0]
    bp, sp, _ = x_prompt.shape
    bs, ts, _ = x_sample.shape
    yp = x_prompt.reshape(bp * sp, D_MODEL)
    ys = x_sample.reshape(bs * ts, D_MODEL)
    caches = (cache_b_kidx,) + tuple(c.transpose(0, 1, 3, 4, 2) for c in (cache_b_k, cache_b_v, cache_c_k, cache_c_v))
    new_p, new_s = [], []
    for l in range(depth):
        lw = _prep_layer(l, ts, weights)
        yp, pr_p = _prompt_layer(yp, p_prompt[l].reshape(bp * sp, PLE_DIM), lw, bp, sp)
        ys, pr_s = _sample_layer(ys, p_sample[l].reshape(bs * ts, PLE_DIM), lw, bs, ts, caches, l)
        new_p.append(pr_p)
        new_s.append(pr_s)

    def stack(states, name, lead, tail):
        return jnp.stack([s[name].reshape(*lead, *tail) for s in states])

    lp, ls = (bp, sp), (bs, ts)
    return (yp.reshape(bp, sp, D_MODEL), ys.reshape(bs, ts, D_MODEL),
            stack(new_p, 'bk', lp, (B_KV_HEADS, B_HEAD_DIM)), stack(new_p, 'bv', lp, (B_KV_HEADS, B_HEAD_DIM)),
            stack(new_p, 'ik', lp, (IDX_DIM,)),
            stack(new_p, 'ck', lp, (C_HEADS, C_HEAD_DIM)), stack(new_p, 'cv', lp, (C_HEADS, C_HEAD_DIM)),
            stack(new_s, 'bk', ls, (B_KV_HEADS, B_HEAD_DIM)), stack(new_s, 'bv', ls, (B_KV_HEADS, B_HEAD_DIM)),
            stack(new_s, 'ik', ls, (IDX_DIM,)),
            stack(new_s, 'ck', ls, (C_HEADS, C_HEAD_DIM)), stack(new_s, 'cv', ls, (C_HEADS, C_HEAD_DIM)),
            stack(new_s, 'av', ls, (A_HALF,)))
```

```python
import functools

import jax
import jax.numpy as jnp
from jax import lax
from jax.experimental import pallas as pl
from jax.experimental.pallas import tpu as pltpu

F32 = jnp.float32
BF16 = jnp.bfloat16
I32 = jnp.int32

D_MODEL = 1024
CHUNK = 64
EPS = 1e-6
A_CHUNK = 128
A_GROUPS = 4
A_GROUP_DIM = 128
A_HALF = 512
B_HEADS = 8
B_KV_HEADS = 2
B_HEAD_DIM = 64
B_TOPK_MAX = 256
IDX_HEADS = 8
IDX_DIM = 32
C_HEADS = 8
C_HEAD_DIM = 64
N_BRANCH = 3
D_FF = 2816
PLE_DIM = 256

LANES = 128
VMEM_LIMIT = 56 * 1024 * 1024

_C_AU, _C_AV, _C_BQ, _C_BK, _C_BV, _C_IQ = 0, 512, 1024, 1536, 1664, 1792
_C_SM, _C_CQ, _C_CK, _C_CV, _C_END = 2048, 2176, 2688, 3200, 3712
DSA_KB = 512

SELECT_MAX_STEPS = 48
SB_SKIP = -104.0
LOG2E = 1.4426950408889634


def _const_spec(shape):
    zeros = (0,) * len(shape)
    return pl.BlockSpec(shape, lambda *_: zeros, pipeline_mode=pl.Buffered(1))


def _rms(x, gain):
    return x * lax.rsqrt(jnp.mean(x * x, axis=-1, keepdims=True) + EPS) * gain


def _dot(a, b):
    return jnp.dot(a, b, preferred_element_type=F32)


def _dot_t(a, b):
    return lax.dot_general(a, b, (((1,), (1,)), ((), ())), preferred_element_type=F32)


def _split_dot(x, m_ref):
    hi = x.astype(BF16)
    lo = (x - hi.astype(F32)).astype(BF16)
    m = m_ref[...]
    return _dot(hi, m) + _dot(lo, m)


def _group_rms(z, g_ref, gain):
    return z * lax.rsqrt(_split_dot(z * z, g_ref) + EPS) * gain


def _proj_kernel(x_ref, nmix_ref, w_ref, wvt_ref, wiwt_ref, avn_ref, qn_ref, kn_ref, g512_ref, g128_ref,
                 ma_ref, abias_ref,
                 oa_ref, qg_ref, qh_ref, bk_ref, bkb_ref, bv_ref, vt_ref, iq_ref, ik_ref, ikb_ref, iwt_ref,
                 cq_ref, ck_ref, cv_ref, ckb_ref, cvb_ref, av_ref, *, tm, ca, sample):
    hb = _rms(x_ref[...], nmix_ref[...]).astype(BF16)

    def proj(c0, c1):
        return _dot(hb, w_ref[:, c0:c1])

    qn = _group_rms(proj(_C_BQ, _C_BK), g512_ref, qn_ref[...]) * (B_HEAD_DIM ** -0.5 * LOG2E)
    qh_ref[...] = qn.astype(BF16)
    lane = lax.broadcasted_iota(I32, (tm, LANES), 1)
    for h in range(B_HEADS):
        g = h // (B_HEADS // B_KV_HEADS)
        pair = qn[:, (h // 2) * LANES:(h // 2 + 1) * LANES]
        if h % 2 != g:
            pair = pltpu.roll(pair, B_HEAD_DIM, axis=1)
        qg_ref[:, h * LANES:(h + 1) * LANES] = jnp.where(lane // B_HEAD_DIM == g, pair, 0.0).astype(BF16)
    bk = _group_rms(proj(_C_BK, _C_BV), g128_ref, kn_ref[...])
    bk_ref[...] = bk
    bkb_ref[...] = bk.astype(BF16)
    bv_ref[...] = proj(_C_BV, _C_IQ)
    vt = _dot_t(wvt_ref[...], hb).astype(BF16)
    vw = min(DSA_KB, tm)
    for c in range(tm // vw):
        vt_ref[c] = vt[:, c * vw:(c + 1) * vw]
    iq_ref[...] = proj(_C_IQ, _C_SM).astype(BF16)
    ik = proj(_C_SM, _C_CQ)[:, 0:IDX_DIM]
    ik_ref[...] = ik
    ikb_ref[...] = ik.astype(BF16)
    iwt_ref[...] = _dot_t(wiwt_ref[...], hb) * (1.0 / 16.0)
    cq_ref[...] = (proj(_C_CQ, _C_CK) * (C_HEAD_DIM ** -0.5)).astype(BF16)
    ck = proj(_C_CK, _C_CV)
    ck_ref[...] = ck
    ckb_ref[...] = ck.astype(BF16)
    cv = proj(_C_CV, _C_END)
    cv_ref[...] = cv
    cvb_ref[...] = cv.astype(BF16)

    au = jax.nn.gelu(proj(_C_AU, _C_AV))
    av = _rms(jax.nn.gelu(proj(_C_AV, _C_BQ)), avn_ref[...])
    av_ref[...] = av
    avb = av.astype(BF16)
    row = lax.broadcasted_iota(I32, (ca, ca), 0)
    col = lax.broadcasted_iota(I32, (ca, ca), 1)
    if sample:
        vis = (col // 16) == (row // 16)
    else:
        vis = (col // CHUNK) <= (row // CHUNK)
    for g in range(A_GROUPS):
        wm = jnp.where(vis, ma_ref[g], 0.0).astype(BF16)
        cs = slice(g * A_GROUP_DIM, (g + 1) * A_GROUP_DIM)
        for c in range(tm // ca):
            rs = slice(c * ca, (c + 1) * ca)
            mixed = _dot(wm, avb[rs, cs]) + abias_ref[:, cs]
            oa_ref[rs, cs] = (au[rs, cs] * mixed).astype(BF16)


_PROJ_NAMES = ('oa', 'qg', 'qh', 'bk', 'bkb', 'bv', 'vt', 'iq', 'ik', 'ikb', 'iwt', 'cq', 'ck', 'cv', 'ckb', 'cvb', 'av')


def _proj_call(x, lw, *, tm, ca, sample):
    n = x.shape[0]
    grid = (n // tm,)

    def tile(c):
        return pl.BlockSpec((tm, c), lambda i: (i, 0))

    row_out = {'oa': (A_HALF, BF16), 'qg': (B_HEADS * LANES, BF16), 'qh': (512, BF16), 'bk': (128, F32), 'bkb': (128, BF16),
               'bv': (128, F32), 'iq': (256, BF16), 'ik': (IDX_DIM, F32), 'ikb': (IDX_DIM, BF16),
               'cq': (512, BF16), 'ck': (512, F32), 'cv': (512, F32), 'ckb': (512, BF16), 'cvb': (512, BF16),
               'av': (A_HALF, F32)}
    out_specs, out_shape = [], []
    for name in _PROJ_NAMES:
        if name == 'vt':
            vw = min(DSA_KB, tm)
            out_specs.append(pl.BlockSpec((tm // vw, 128, vw), lambda i: (i, 0, 0)))
            out_shape.append(jax.ShapeDtypeStruct((n // vw, 128, vw), BF16))
        elif name == 'iwt':
            out_specs.append(pl.BlockSpec((IDX_HEADS, tm), lambda i: (0, i)))
            out_shape.append(jax.ShapeDtypeStruct((IDX_HEADS, n), F32))
        else:
            c, dt = row_out[name]
            out_specs.append(tile(c))
            out_shape.append(jax.ShapeDtypeStruct((n, c), dt))
    ma = lw['ma_sample'] if sample else lw['ma_prompt']
    abias = lw['abias_sample'] if sample else lw['abias_prompt']
    consts = [lw['norm_mix'], lw['w1'], lw['w_vt'], lw['w_iwt'], lw['a_vnorm'], lw['qn'], lw['kn'],
              lw['g512'], lw['g128'], ma, abias]
    outs = pl.pallas_call(
        functools.partial(_proj_kernel, tm=tm, ca=ca, sample=sample),
        grid=grid,
        in_specs=[tile(D_MODEL)] + [_const_spec(c.shape) for c in consts],
        out_specs=out_specs,
        out_shape=out_shape,
        compiler_params=pltpu.CompilerParams(dimension_semantics=("arbitrary",), vmem_limit_bytes=VMEM_LIMIT),
        name="proj_sample" if sample else "proj_prompt",
    )(x, *consts)
    return dict(zip(_PROJ_NAMES, outs))


def _dsa_kernel(qg_ref, iq_ref, iwt_ref, ik_ref, k_ref, vt_ref, tri_ref, o_ref, sc_ref,
                *, qb, kb_size, n_valid, topk):
    q0 = pl.program_id(1) * qb
    nkb = (q0 + qb + kb_size - 1) // kb_size

    cb_size = kb_size
    ncb = nkb
    cpos0 = lax.broadcasted_iota(I32, (cb_size, qb), 0)
    qpos_row = q0 + lax.broadcasted_iota(I32, (1, qb), 1)
    qchunk_row = qpos_row // CHUNK
    n_adm = jnp.minimum((qchunk_row + 1) * CHUNK, n_valid)

    def fold(x, op):
        return op(op(x.reshape(kb_size // 8, 8, qb), axis=0), axis=0, keepdims=True)

    iq = iq_ref[...]
    iq_rows = jnp.concatenate([iq[:, h * IDX_DIM:(h + 1) * IDX_DIM] for h in range(IDX_HEADS)], axis=0)
    iwt = iwt_ref[...]

    def score_body(cb, carry):
        mx, mn = carry
        off = pl.multiple_of(cb * cb_size, cb_size)
        d = _dot_t(ik_ref[pl.ds(off, cb_size), :], iq_rows)
        acc = jnp.zeros((cb_size, qb), F32)
        for h in range(IDX_HEADS):
            acc = acc + jnp.maximum(d[:, h * qb:(h + 1) * qb], 0.0) * iwt[h:h + 1, :]
        kpos = off + cpos0
        adm = (kpos // CHUNK <= qchunk_row) & (kpos < n_valid)
        sc = jnp.where(adm, acc, -jnp.inf)
        sc_ref[pl.ds(off, cb_size), :] = sc
        mx = jnp.maximum(mx, jnp.max(sc.reshape(cb_size // 8, 8, qb), axis=0))
        mn = jnp.minimum(mn, jnp.min(jnp.where(adm, acc, jnp.inf).reshape(cb_size // 8, 8, qb), axis=0))
        return mx, mn

    mx, mn = lax.fori_loop(0, ncb, score_body,
                           (jnp.full((8, qb), -jnp.inf, F32), jnp.full((8, qb), jnp.inf, F32)))
    smax = jnp.max(mx, axis=0, keepdims=True)
    smin = jnp.min(mn, axis=0, keepdims=True)

    acc_rows = 32

    def count(pred):
        def body(cb, cnt):
            off = pl.multiple_of(cb * cb_size, cb_size)
            ind = jnp.where(pred(sc_ref[pl.ds(off, cb_size), :], off), 1.0, 0.0)
            return cnt + jnp.sum(ind.reshape(cb_size // acc_rows, acc_rows, qb), axis=0)
        cnt = lax.fori_loop(0, ncb, body, jnp.zeros((acc_rows, qb), F32))
        return jnp.sum(jnp.sum(cnt.reshape(acc_rows // 8, 8, qb), axis=0), axis=0, keepdims=True)

    def reduce_min(fn):
        def body(cb, acc):
            off = pl.multiple_of(cb * cb_size, cb_size)
            v = fn(sc_ref[pl.ds(off, cb_size), :])
            return jnp.minimum(acc, jnp.min(v.reshape(cb_size // 8, 8, qb), axis=0))
        acc = lax.fori_loop(0, ncb, body, jnp.full((8, qb), jnp.inf, F32))
        return jnp.min(acc, axis=0, keepdims=True)

    k_f = float(topk)
    few = n_adm <= topk
    steps_per_trip = 4

    c_ge0 = count(lambda blk, off: blk >= 0.0)
    c_gt0 = count(lambda blk, off: blk > 0.0)
    zero_tie = (c_ge0 >= k_f) & (c_gt0 < k_f)
    above = c_gt0 >= k_f

    def sel_cond(state):
        t, _, _, _, pending = state
        return (t < SELECT_MAX_STEPS) & (pending > 0.0)

    def sel_body(state):
        t, lo, hi, c_lo, _ = state
        for _ in range(steps_per_trip):
            mid = lo + (hi - lo) * 0.5
            c = count(lambda blk, off: blk >= mid)
            take = c >= k_f
            lo = jnp.where(take, mid, lo)
            hi = jnp.where(take, hi, mid)
            c_lo = jnp.where(take, c, c_lo)
        pending = jnp.max(jnp.where(few | zero_tie | (c_lo == k_f), 0.0, 1.0))
        return t + steps_per_trip, lo, hi, c_lo, pending

    _, lo, _, c_lo, pending = lax.while_loop(
        sel_cond, sel_body,
        (jnp.int32(0), jnp.where(above, 0.0, smin), jnp.where(above, smax, 0.0),
         jnp.where(above, c_ge0, n_adm.astype(F32)), jnp.float32(1.0)))

    def resolve(_):
        v = reduce_min(lambda blk: jnp.where(blk >= lo, blk, jnp.inf))
        return v, k_f - count(lambda blk, off: blk > v)

    v_cap, need_cap = lax.cond(pending > 0.0, resolve, lambda _: (lo, jnp.full((1, qb), jnp.inf, F32)), 0)
    unsettled = jnp.logical_not(few | zero_tie | (c_lo == k_f))
    thr = jnp.where(few, -jnp.inf, jnp.where(zero_tie, 0.0, jnp.where(unsettled, v_cap, lo)))
    need = jnp.where(few, jnp.inf, jnp.where(zero_tie, k_f - c_gt0, jnp.where(unsettled, need_cap, jnp.inf)))
    any_tie = jnp.max(jnp.where((zero_tie | unsettled) & jnp.logical_not(few), 1.0, 0.0))

    @pl.when(any_tie > 0.0)
    def _():
        tri = tri_ref[...]

        def body(cb, seen):
            off = pl.multiple_of(cb * cb_size, cb_size)
            sc = sc_ref[pl.ds(off, cb_size), :]
            eq = sc == thr
            rank = _dot(tri, jnp.where(eq, 1.0, 0.0).astype(BF16)) + seen
            sc_ref[pl.ds(off, cb_size), :] = jnp.where(eq & (rank > need), -jnp.inf, sc)
            return rank[cb_size - 1:cb_size, :]

        lax.fori_loop(0, ncb, body, jnp.zeros((1, qb), F32))

    qg = qg_ref[...]
    q_pairs = [jnp.concatenate([qg[:, (2 * p) * LANES:(2 * p + 1) * LANES],
                                qg[:, (2 * p + 1) * LANES:(2 * p + 2) * LANES]], axis=0)
               for p in range(B_HEADS // 2)]
    hpg = B_HEADS // B_KV_HEADS

    def att_body(kb, carry):
        ms, ls, accs = carry
        off = pl.multiple_of(kb * kb_size, kb_size)
        sc_blk = sc_ref[pl.ds(off, kb_size), :]
        sel = (sc_blk >= thr) & (sc_blk > -jnp.inf)
        bias = jnp.where(sel, 0.0, -jnp.inf)
        k_blk = k_ref[pl.ds(off, kb_size), :]
        vt_blk = vt_ref[kb]
        new_m, new_l, new_a = list(ms), list(ls), list(accs)
        sts = [_dot_t(k_blk, q_pairs[p]) for p in range(B_HEADS // 2)]
        pts, alphas = [], []
        for h in range(B_HEADS):
            s = sts[h // 2][:, (h % 2) * qb:(h % 2 + 1) * qb] + bias
            m_new = jnp.maximum(ms[h], fold(s, jnp.max))
            m_safe = jnp.where(m_new == -jnp.inf, 0.0, m_new)
            pt = jnp.exp2(s - m_safe)
            alpha = jnp.exp2(ms[h] - m_safe)
            new_m[h] = m_new
            new_l[h] = alpha * ls[h] + fold(pt, jnp.sum)
            pts.append(pt.astype(BF16))
            alphas.append(alpha)
        for h in range(B_HEADS):
            g = h // hpg
            new_a[h] = alphas[h] * accs[h] + _dot(vt_blk[g * B_HEAD_DIM:(g + 1) * B_HEAD_DIM, :], pts[h])
        return tuple(new_m), tuple(new_l), tuple(new_a)

    init = (tuple(jnp.full((1, qb), -jnp.inf, F32) for _ in range(B_HEADS)),
            tuple(jnp.zeros((1, qb), F32) for _ in range(B_HEADS)),
            tuple(jnp.zeros((B_HEAD_DIM, qb), F32) for _ in range(B_HEADS)))
    ms, ls, accs = lax.fori_loop(0, nkb, att_body, init)
    out_t = jnp.concatenate([accs[h] / ls[h] for h in range(B_HEADS)], axis=0)
    o_ref[...] = out_t.T.astype(BF16)


def _dsa_call(qg, iq, iwt, ik, k, vt, tri, *, qb, kb_size):
    b, sq, _ = qg.shape
    lp = ik.shape[1]
    nq = sq // qb
    nkb = lp // kb_size
    topk = min(B_TOPK_MAX, lp // 4)

    def qspec(c):
        return pl.BlockSpec((None, qb, c), lambda bi, i: (bi, i, 0))

    def kspec(c):
        return pl.BlockSpec((None, lp, c), lambda bi, i: (bi, 0, 0))

    return pl.pallas_call(
        functools.partial(_dsa_kernel, qb=qb, kb_size=kb_size, n_valid=lp, topk=topk),
        grid=(b, nq),
        in_specs=[qspec(B_HEADS * LANES), qspec(256),
                  pl.BlockSpec((None, IDX_HEADS, qb), lambda bi, i: (bi * nq + i, 0, 0)),
                  kspec(IDX_DIM), kspec(128),
                  pl.BlockSpec((nkb, 128, kb_size), lambda bi, i: (bi, 0, 0)), _const_spec(tri.shape)],
        out_specs=qspec(512),
        out_shape=jax.ShapeDtypeStruct((b, sq, 512), BF16),
        scratch_shapes=[pltpu.VMEM((lp, qb), F32)],
        compiler_params=pltpu.CompilerParams(dimension_semantics=("arbitrary", "arbitrary"),
                                             vmem_limit_bytes=VMEM_LIMIT),
        name="dsa_prompt",
    )(qg, iq, iwt, ik, k, vt, tri)


def _dsa_sample_kernel(qn_ref, iq_ref, iwc_ref, ikn_ref, ktn_ref, vtn_ref, ikp_ref, ktp_ref, vtp_ref, tri_ref, o_ref,
                       sc_ref, *, t, n_past, cb, topk):
    nb = n_past // cb
    hpg = B_HEADS // B_KV_HEADS
    lanes = lax.broadcasted_iota(I32, (t, cb), 1)
    iq = iq_ref[...]
    iq_rows = jnp.concatenate([iq[:, h * IDX_DIM:(h + 1) * IDX_DIM] for h in range(IDX_HEADS)], axis=0)
    iwc = iwc_ref[...]

    def scores(ik_blk):
        w = jnp.maximum(_dot_t(iq_rows, ik_blk), 0.0) * iwc
        s = w[0:t]
        for h in range(1, IDX_HEADS):
            s = s + w[h * t:(h + 1) * t]
        return s

    def fold_lanes(x, op):
        part = x[:, 0:LANES]
        for c in range(1, cb // LANES):
            part = op(part, x[:, c * LANES:(c + 1) * LANES])
        return part

    def past_scores(kb, carry):
        mx, mn = carry
        off = pl.multiple_of(kb * cb, cb)
        sc = scores(ikp_ref[pl.ds(off, cb), :].astype(BF16))
        sc_ref[kb] = sc
        return jnp.maximum(mx, fold_lanes(sc, jnp.maximum)), jnp.minimum(mn, fold_lanes(sc, jnp.minimum))

    mx, mn = lax.fori_loop(0, nb, past_scores,
                           (jnp.full((t, LANES), -jnp.inf, F32), jnp.full((t, LANES), jnp.inf, F32)))
    sc_new = scores(ikn_ref[...])
    valid = lanes < t
    sc_ref[nb] = jnp.where(valid, sc_new, -jnp.inf)
    mx = jnp.maximum(mx, fold_lanes(jnp.where(valid, sc_new, -jnp.inf), jnp.maximum))
    mn = jnp.minimum(mn, fold_lanes(jnp.where(valid, sc_new, jnp.inf), jnp.minimum))
    smax = jnp.max(mx, axis=1, keepdims=True)
    smin = jnp.min(mn, axis=1, keepdims=True)

    def count(pred):
        def body(kb, cnt):
            ind = jnp.where(pred(sc_ref[kb], kb * cb + lanes), 1.0, 0.0)
            return cnt + fold_lanes(ind, jnp.add)
        cnt = lax.fori_loop(0, nb + 1, body, jnp.zeros((t, LANES), F32))
        return jnp.sum(cnt, axis=1, keepdims=True)

    k_f = float(topk)
    steps_per_trip = 4

    c_ge0 = count(lambda blk, pos: blk >= 0.0)
    c_gt0 = count(lambda blk, pos: blk > 0.0)
    zero_tie = (c_ge0 >= k_f) & (c_gt0 < k_f)
    above = c_gt0 >= k_f

    def sel_cond(state):
        tt, _, _, _, pending = state
        return (tt < SELECT_MAX_STEPS) & (pending > 0.0)

    def sel_body(state):
        tt, lo, hi, c_lo, _ = state
        for _ in range(steps_per_trip):
            mid = lo + (hi - lo) * 0.5
            c = count(lambda blk, pos: blk >= mid)
            take = c >= k_f
            lo = jnp.where(take, mid, lo)
            hi = jnp.where(take, hi, mid)
            c_lo = jnp.where(take, c, c_lo)
        pending = jnp.max(jnp.where(zero_tie | (c_lo == k_f), 0.0, 1.0))
        return tt + steps_per_trip, lo, hi, c_lo, pending

    _, lo, _, c_lo, pending = lax.while_loop(
        sel_cond, sel_body,
        (jnp.int32(0), jnp.where(above, 0.0, smin), jnp.where(above, smax, 0.0),
         jnp.where(above, c_ge0, float(n_past + t)), jnp.float32(1.0)))

    def resolve(_):
        def body(kb, acc):
            blk = sc_ref[kb]
            return jnp.minimum(acc, fold_lanes(jnp.where(blk >= lo, blk, jnp.inf), jnp.minimum))
        acc = lax.fori_loop(0, nb + 1, body, jnp.full((t, LANES), jnp.inf, F32))
        v = jnp.min(acc, axis=1, keepdims=True)
        return v, k_f - count(lambda blk, pos: blk > v)

    v_cap, need_cap = lax.cond(pending > 0.0, resolve, lambda _: (lo, jnp.full((t, 1), jnp.inf, F32)), 0)
    unsettled = jnp.logical_not(zero_tie | (c_lo == k_f))
    thr = jnp.where(zero_tie, 0.0, jnp.where(unsettled, v_cap, lo))
    need = jnp.where(zero_tie, k_f - c_gt0, jnp.where(unsettled, need_cap, jnp.inf))
    any_tie = jnp.max(jnp.where(zero_tie | unsettled, 1.0, 0.0))

    @pl.when(any_tie > 0.0)
    def _():
        tri = tri_ref[...]

        def body(kb, seen):
            sc = sc_ref[kb]
            eq = sc == thr
            rank = _dot(jnp.where(eq, 1.0, 0.0).astype(BF16), tri) + seen
            sc_ref[kb] = jnp.where(eq & (rank > need), -jnp.inf, sc)
            return rank[:, cb - 1:cb]

        lax.fori_loop(0, nb + 1, body, jnp.zeros((t, 1), F32))

    qn = qn_ref[...]
    q_groups = [jnp.concatenate([qn[:, h * B_HEAD_DIM:(h + 1) * B_HEAD_DIM]
                                 for h in range(g * hpg, (g + 1) * hpg)], axis=0)
                for g in range(B_KV_HEADS)]

    def attend(carry, kb, kt, vt):
        ms, ls, accs = carry
        sc_blk = sc_ref[kb]
        sel = (sc_blk >= thr) & (sc_blk > -jnp.inf)
        bias = jnp.where(sel, 0.0, -jnp.inf)
        bias_rows = jnp.concatenate([bias] * hpg, axis=0)
        new_m, new_l, new_a = [], [], []
        for g in range(B_KV_HEADS):
            s = _dot(q_groups[g], kt(g)) + bias_rows
            m_new = jnp.maximum(ms[g], jnp.max(s, axis=1, keepdims=True))
            m_safe = jnp.where(m_new == -jnp.inf, 0.0, m_new)
            p = jnp.exp2(s - m_safe)
            alpha = jnp.exp2(ms[g] - m_safe)
            new_m.append(m_new)
            new_l.append(alpha * ls[g] + jnp.sum(p, axis=1, keepdims=True))
            new_a.append(alpha * accs[g] + _dot_t(p.astype(BF16), vt(g)))
        return tuple(new_m), tuple(new_l), tuple(new_a)

    def past_attend(kb, carry):
        off = pl.multiple_of(kb * cb, cb)
        return attend(carry, kb, lambda g: ktp_ref[g, :, pl.ds(off, cb)].astype(BF16),
                      lambda g: vtp_ref[g, :, pl.ds(off, cb)].astype(BF16))

    init = (tuple(jnp.full((hpg * t, 1), -jnp.inf, F32) for _ in range(B_KV_HEADS)),
            tuple(jnp.zeros((hpg * t, 1), F32) for _ in range(B_KV_HEADS)),
            tuple(jnp.zeros((hpg * t, B_HEAD_DIM), F32) for _ in range(B_KV_HEADS)))
    carry = lax.fori_loop(0, nb, past_attend, init)
    _, ls, accs = attend(carry, nb, lambda g: ktn_ref[g], lambda g: vtn_ref[g])
    for g in range(B_KV_HEADS):
        out = accs[g] / ls[g]
        for hh in range(hpg):
            h = g * hpg + hh
            o_ref[:, h * B_HEAD_DIM:(h + 1) * B_HEAD_DIM] = out[hh * t:(hh + 1) * t].astype(BF16)


def _dsa_sample_call(qn, iq, iwc, ikn, ktn, vtn, ikp, ktp, vtp, tri, layer, *, cb):
    b, t, _ = qn.shape
    n_past = ikp.shape[2]
    topk = min(B_TOPK_MAX, (n_past + t) // 4)
    assert n_past % cb == 0 and n_past + t > topk

    def bspec(*shape):
        zeros = (0,) * len(shape)
        return pl.BlockSpec((None,) + shape, lambda bi: (bi,) + zeros)

    def cspec(*shape):
        zeros = (0,) * len(shape)
        return pl.BlockSpec((None, None) + shape, lambda bi: (layer, bi) + zeros)

    return pl.pallas_call(
        functools.partial(_dsa_sample_kernel, t=t, n_past=n_past, cb=cb, topk=topk),
        grid=(b,),
        in_specs=[bspec(t, 512), bspec(t, 256), bspec(IDX_HEADS * t, 1),
                  bspec(cb, IDX_DIM), bspec(B_KV_HEADS, B_HEAD_DIM, cb), bspec(B_KV_HEADS, B_HEAD_DIM, cb),
                  cspec(n_past, IDX_DIM), cspec(B_KV_HEADS, B_HEAD_DIM, n_past),
                  cspec(B_KV_HEADS, B_HEAD_DIM, n_past), _const_spec(tri.shape)],
        out_specs=bspec(t, 512),
        out_shape=jax.ShapeDtypeStruct((b, t, 512), BF16),
        scratch_shapes=[pltpu.VMEM((n_past // cb + 1, t, cb), F32)],
        compiler_params=pltpu.CompilerParams(dimension_semantics=("arbitrary",), vmem_limit_bytes=VMEM_LIMIT),
        name="dsa_sample",
    )(qn, iq, iwc, ikn, ktn, vtn, ikp, ktp, vtp, tri)


SB_WIDE = 256
SB_GROUP = 4


def _sb_tri2(n):
    r = jnp.arange(2 * n)[:, None] % n
    c = jnp.arange(n)[None, :]
    return jnp.where(r > c, 1.0, 0.0).astype(BF16)


def _sb_stage(z_fn, pv, tri2_ref, carries, accs, vis):
    tri2 = tri2_ref[...]
    new_c, new_a = list(carries), list(accs)
    for h0 in range(0, C_HEADS, SB_GROUP):
        heads = range(h0, h0 + SB_GROUP)
        zs = [z_fn(h) for h in heads]
        sps = [jnp.maximum(z, 0.0) + jnp.log(1.0 + jnp.exp(-jnp.abs(z))) for z in zs]
        stays = [-sp if vis is None else jnp.where(vis, -sp, 0.0) for sp in sps]
        afters = []
        for ls in stays:
            hi = ls.astype(BF16)
            lo = (ls - hi.astype(F32)).astype(BF16)
            afters.append(_dot(jnp.concatenate([hi, lo], axis=1), tri2))
        for i, h in enumerate(heads):
            w = jnp.exp((zs[i] - sps[i]) + (afters[i] + carries[h]))
            if vis is not None:
                w = jnp.where(vis, w, 0.0)
            new_a[h] = accs[h] + pv(h, w.astype(BF16))
            new_c[h] = carries[h] + jnp.sum(stays[i], axis=1, keepdims=True)
    return tuple(new_c), tuple(new_a)


def _sb_max_carry(carries):
    m = carries[0]
    for c in carries[1:]:
        m = jnp.maximum(m, c)
    return jnp.max(m)


def _sb_init(qb, d):
    return (tuple(jnp.zeros((qb, 1), F32) for _ in range(C_HEADS)),
            tuple(jnp.zeros((qb, d), F32) for _ in range(C_HEADS)))


def _sb_prompt_kernel(q_ref, k_ref, v_ref, tri_ref, o_ref, *, qb):
    i = pl.program_id(1)
    q = q_ref[...]
    lane = lax.broadcasted_iota(I32, (qb, LANES), 1)
    q_heads = []
    for h in range(C_HEADS):
        pair = q[:, (h // 2) * LANES:(h // 2 + 1) * LANES]
        q_heads.append(jnp.where(lane // C_HEAD_DIM == h % 2, pair, jnp.zeros_like(pair)))

    def block(off, carries, accs, vis):
        k_blk = k_ref[pl.ds(off, qb), :]
        v_blk = v_ref[pl.ds(off, qb), :]
        z_fn = lambda h: _dot_t(q_heads[h], k_blk[:, (h // 2) * LANES:(h // 2 + 1) * LANES])
        pv = lambda h, w: _dot(w, v_blk[:, (h // 2) * LANES:(h // 2 + 1) * LANES])
        return _sb_stage(z_fn, pv, tri_ref, carries, accs, vis)

    rows = lax.broadcasted_iota(I32, (qb, qb), 0)
    cols = lax.broadcasted_iota(I32, (qb, qb), 1)
    carries, accs = _sb_init(qb, LANES)
    carries, accs = block(pl.multiple_of(i * qb, qb), carries, accs, cols < rows)

    def cond(state):
        j, mx, _, _ = state
        return (j < i) & (mx > SB_SKIP)

    def body(state):
        j, _, carries, accs = state
        off = pl.multiple_of((i - j - 1) * qb, qb)
        carries, accs = block(off, carries, accs, None)
        return j + 1, _sb_max_carry(carries), carries, accs

    _, _, _, accs = lax.while_loop(cond, body, (jnp.int32(0), _sb_max_carry(carries), carries, accs))
    for p in range(C_HEADS // 2):
        o_ref[:, p * LANES:(p + 1) * LANES] = jnp.where(lane < C_HEAD_DIM, accs[2 * p], accs[2 * p + 1]).astype(BF16)


def _sb_prompt_call(cq, ckb, cvb, lw, *, qb):
    b, s, _ = cq.shape
    qspec = pl.BlockSpec((None, qb, 512), lambda bi, i: (bi, i, 0))
    kspec = pl.BlockSpec((None, s, 512), lambda bi, i: (bi, 0, 0))
    return pl.pallas_call(
        functools.partial(_sb_prompt_kernel, qb=qb),
        grid=(b, s // qb),
        in_specs=[qspec, kspec, kspec, _const_spec(lw['tri2'].shape)],
        out_specs=qspec,
        out_shape=jax.ShapeDtypeStruct((b, s, 512), BF16),
        compiler_params=pltpu.CompilerParams(dimension_semantics=("arbitrary", "arbitrary"),
                                             vmem_limit_bytes=VMEM_LIMIT),
        name="stick_prompt",
    )(cq, ckb, cvb, lw['tri2'])


def _sb_sample_kernel(q_ref, knt_ref, vnt_ref, kpt_ref, vpt_ref, tri1_ref, tri2_ref, o_ref, *, qb, n_past):
    q = q_ref[...]
    q_heads = [q[:, h * C_HEAD_DIM:(h + 1) * C_HEAD_DIM] for h in range(C_HEADS)]
    lane = lax.broadcasted_iota(I32, (qb, LANES), 1)
    rows = lax.broadcasted_iota(I32, (qb, LANES), 0)

    def block(kt, vt, tri_ref, carries, accs, vis):
        z_fn = lambda h: _dot(q_heads[h], kt(h))
        pv = lambda h, w: _dot_t(w, vt(h))
        return _sb_stage(z_fn, pv, tri_ref, carries, accs, vis)

    carries, accs = _sb_init(qb, C_HEAD_DIM)
    carries, accs = block(lambda h: knt_ref[h], lambda h: vnt_ref[h], tri1_ref, carries, accs, lane < rows)

    def cond(state):
        j, mx, _, _ = state
        return (j < n_past // SB_WIDE) & (mx > SB_SKIP)

    def body(state):
        j, _, carries, accs = state
        off = pl.multiple_of(n_past - (j + 1) * SB_WIDE, SB_WIDE)
        carries, accs = block(lambda h: kpt_ref[h, :, pl.ds(off, SB_WIDE)].astype(BF16),
                              lambda h: vpt_ref[h, :, pl.ds(off, SB_WIDE)].astype(BF16),
                              tri2_ref, carries, accs, None)
        return j + 1, _sb_max_carry(carries), carries, accs

    _, _, _, accs = lax.while_loop(cond, body, (jnp.int32(0), _sb_max_carry(carries), carries, accs))
    for h in range(C_HEADS):
        o_ref[:, h * C_HEAD_DIM:(h + 1) * C_HEAD_DIM] = accs[h].astype(BF16)


def _sb_sample_call(cq, knt, vnt, kpt, vpt, lw, layer):
    b, t, _ = cq.shape
    n_past = kpt.shape[-1]
    qspec = pl.BlockSpec((None, t, 512), lambda bi: (bi, 0, 0))
    nspec = pl.BlockSpec((None, C_HEADS, C_HEAD_DIM, LANES), lambda bi: (bi, 0, 0, 0))
    pspec = pl.BlockSpec((None, None, C_HEADS, C_HEAD_DIM, n_past), lambda bi: (layer, bi, 0, 0, 0))
    return pl.pallas_call(
        functools.partial(_sb_sample_kernel, qb=t, n_past=n_past),
        grid=(b,),
        in_specs=[qspec, nspec, nspec, pspec, pspec, _const_spec(lw['tri1'].shape), _const_spec(lw['tri2'].shape)],
        out_specs=qspec,
        out_shape=jax.ShapeDtypeStruct((b, t, 512), BF16),
        compiler_params=pltpu.CompilerParams(dimension_semantics=("arbitrary",), vmem_limit_bytes=VMEM_LIMIT),
        name="stick_sample",
    )(cq, knt, vnt, kpt, vpt, lw['tri1'], lw['tri2'])


def _merge_kernel(x_ref, nmix_ref, wg_ref, gb_ref, oa_ref, ob_ref, oc_ref, wa_ref, wb_ref, wc_ref, wo_ref, o_ref):
    x = x_ref[...]
    hb = _rms(x, nmix_ref[...]).astype(BF16)
    merged = None
    for idx, (br_ref, w_ref) in enumerate(((oa_ref, wa_ref), (ob_ref, wb_ref), (oc_ref, wc_ref))):
        cs = slice(idx * D_MODEL, (idx + 1) * D_MODEL)
        gate = jax.nn.sigmoid(_dot(hb, wg_ref[:, cs]) + gb_ref[:, cs])
        term = gate * _dot(br_ref[...], w_ref[...])
        merged = term if merged is None else merged + term
    o_ref[...] = x + _dot(merged.astype(BF16), wo_ref[...])


def _merge_call(x, oa, ob, oc, lw, *, tm, name):
    n = x.shape[0]

    def tile(c):
        return pl.BlockSpec((tm, c), lambda i: (i, 0))

    consts_a = [lw['norm_mix'], lw['w_gate'], lw['gate_bias']]
    consts_b = [lw['w_br_a'], lw['w_br_b'], lw['w_br_c'], lw['w_out']]
    return pl.pallas_call(
        _merge_kernel,
        grid=(n // tm,),
        in_specs=([tile(D_MODEL)] + [_const_spec(c.shape) for c in consts_a] + [tile(512)] * 3
                  + [_const_spec(c.shape) for c in consts_b]),
        out_specs=tile(D_MODEL),
        out_shape=jax.ShapeDtypeStruct((n, D_MODEL), F32),
        compiler_params=pltpu.CompilerParams(dimension_semantics=("arbitrary",), vmem_limit_bytes=VMEM_LIMIT),
        name=name,
    )(x, *consts_a, oa, ob, oc, *consts_b)


def _ffn_kernel(x_ref, p_ref, nffn_ref, wfi_ref, wfo_ref, nple_ref, wpg_ref, wpp_ref, o_ref):
    x = x_ref[...]
    hf = _rms(x, nffn_ref[...]).astype(BF16)
    g = _dot(hf, wfi_ref[:, 0:D_FF])
    up = _dot(hf, wfi_ref[:, D_FF:2 * D_FF])
    x = x + _dot((jax.nn.silu(g) * up).astype(BF16), wfo_ref[...])
    hp = _rms(x, nple_ref[...]).astype(BF16)
    gate = jax.nn.sigmoid(_dot(hp, wpg_ref[...]))
    o_ref[...] = x + gate * _dot(p_ref[...].astype(BF16), wpp_ref[...])


def _ffn_call(x, p, layer, lw, *, tm, name):
    n = x.shape[0]
    consts = [lw['norm_ffn'], lw['w_ffn_in'], lw['w_ffn_out'], lw['norm_ple'], lw['w_ple_gate'], lw['w_ple_proj']]
    return pl.pallas_call(
        _ffn_kernel,
        grid=(n // tm,),
        in_specs=[pl.BlockSpec((tm, D_MODEL), lambda i: (i, 0)),
                  pl.BlockSpec((None, tm, PLE_DIM), lambda i: (layer, i, 0))]
        + [_const_spec(c.shape) for c in consts],
        out_specs=pl.BlockSpec((tm, D_MODEL), lambda i: (i, 0)),
        out_shape=jax.ShapeDtypeStruct((n, D_MODEL), F32),
        compiler_params=pltpu.CompilerParams(dimension_semantics=("arbitrary",), vmem_limit_bytes=VMEM_LIMIT),
        name=name,
    )(x, p, *consts)


def _block_diag_mean(n):
    r = jnp.arange(n)
    return jnp.where((r[:, None] // B_HEAD_DIM) == (r[None, :] // B_HEAD_DIM), 1.0 / B_HEAD_DIM, 0.0).astype(BF16)


def _rank_tri(n):
    r = jnp.arange(n)
    return jnp.where(r[None, :] <= r[:, None], 1.0, 0.0).astype(BF16)


def _prep_layer(l, t_sample, w):
    w_in = w['w_in'][l]
    small = jnp.pad(w_in[:, 2048:2088], ((0, 0), (0, LANES - IDX_DIM - IDX_HEADS)))
    w1 = jnp.concatenate([w_in[:, :2048], small, w_in[:, 2088:3624]], axis=1).astype(BF16)
    a_ws = w['a_ws'][l]
    a_bias_full = jnp.repeat(w['a_bias'][l].T, A_GROUP_DIM, axis=1)
    row = lambda a: a.reshape(1, -1)
    return {
        'norm_mix': row(w['norm_mix'][l]), 'w1': w1,
        'w_vt': w_in[:, 1664:1792].T.astype(BF16), 'w_iwt': w_in[:, 2080:2088].T.astype(BF16),
        'a_vnorm': row(w['a_vnorm'][l]),
        'qn': row(jnp.tile(w['b_qnorm'][l], B_HEADS)), 'kn': row(jnp.tile(w['b_knorm'][l], B_KV_HEADS)),
        'g512': _block_diag_mean(512), 'g128': _block_diag_mean(128),
        'tri1': _sb_tri2(LANES), 'tri2': _sb_tri2(SB_WIDE),
        'tri_rank': _rank_tri(DSA_KB), 'tri_rank_t': _rank_tri(2 * DSA_KB).T,
        'ma_prompt': a_ws, 'abias_prompt': a_bias_full,
        'ma_sample': jnp.tile(a_ws[:, :t_sample, :t_sample], (1, 16, 16)),
        'abias_sample': jnp.tile(a_bias_full[:t_sample], (16, 1)),
        'w_gate': w_in[:, 3624:].astype(BF16), 'gate_bias': row(w['gate_bias'][l]),
        'w_br_a': w['w_br_a'][l].astype(BF16), 'w_br_b': w['w_br_b'][l].astype(BF16),
        'w_br_c': w['w_br_c'][l].astype(BF16), 'w_out': w['w_out'][l].astype(BF16),
        'norm_ffn': row(w['norm_ffn'][l]), 'w_ffn_in': w['w_ffn_in'][l].astype(BF16),
        'w_ffn_out': w['w_ffn_out'][l].astype(BF16), 'norm_ple': row(w['norm_ple'][l]),
        'w_ple_gate': w['w_ple_gate'][l].astype(BF16), 'w_ple_proj': w['w_ple_proj'][l].astype(BF16),
    }


def _iwt_blocks(iwt, qb):
    return iwt.reshape(IDX_HEADS, -1, qb).transpose(1, 0, 2)


def _prompt_layer(x, p, lw, b, s, layer):
    pr = _proj_call(x, lw, tm=512, ca=A_CHUNK, sample=False)
    r3 = lambda a: a.reshape(b, s, a.shape[-1])
    ob = _dsa_call(r3(pr['qg']), r3(pr['iq']), _iwt_blocks(pr['iwt'], 256), r3(pr['ikb']), r3(pr['bkb']), pr['vt'],
                   lw['tri_rank'], qb=256, kb_size=DSA_KB)
    oc = _sb_prompt_call(r3(pr['cq']), r3(pr['ckb']), r3(pr['cvb']), lw, qb=SB_WIDE)
    x = _merge_call(x, pr['oa'], ob.reshape(b * s, 512), oc.reshape(b * s, 512), lw, tm=256, name="merge_prompt")
    x = _ffn_call(x, p, layer, lw, tm=256, name="ffn_prompt")
    return x, pr


def _sample_layer(x, p, lw, b, t, caches, layer):
    pik, pbk_t, pbv_t, pck_t, pcv_t = caches
    pr = _proj_call(x, lw, tm=b * t, ca=b * t, sample=True)
    r3 = lambda a: a.reshape(b, t, a.shape[-1])
    cb = 2 * DSA_KB

    def new_t(a, heads, width):
        at = a.reshape(b, t, heads, -1).transpose(0, 2, 3, 1).astype(BF16)
        return jnp.pad(at, ((0, 0), (0, 0), (0, 0), (0, width - t)))

    ikn = jnp.pad(r3(pr['ikb']), ((0, 0), (0, cb - t), (0, 0)))
    iwc = _iwt_blocks(pr['iwt'], t).reshape(b, IDX_HEADS * t, 1)
    ob = _dsa_sample_call(r3(pr['qh']), r3(pr['iq']), iwc, ikn, new_t(pr['bk'], B_KV_HEADS, cb),
                          new_t(pr['bv'], B_KV_HEADS, cb), pik, pbk_t, pbv_t, lw['tri_rank_t'], layer, cb=cb)
    oc = _sb_sample_call(r3(pr['cq']), new_t(pr['ck'], C_HEADS, LANES), new_t(pr['cv'], C_HEADS, LANES),
                         pck_t, pcv_t, lw, layer)
    x = _merge_call(x, pr['oa'], ob.reshape(b * t, 512), oc.reshape(b * t, 512), lw, tm=b * t, name="merge_sample")
    x = _ffn_call(x, p, layer, lw, tm=b * t, name="ffn_sample")
    return x, pr


def kernel(x_prompt, x_sample, cache_b_k, cache_b_v, cache_b_kidx, cache_c_k, cache_c_v, p_prompt, p_sample,
           norm_mix, w_in, gate_bias, a_vnorm, a_ws, a_bias, b_qnorm, b_knorm, w_br_a, w_br_b, w_br_c, w_out,
           norm_ffn, w_ffn_in, w_ffn_out, norm_ple, w_ple_gate, w_ple_proj):
    weights = dict(norm_mix=norm_mix, w_in=w_in, gate_bias=gate_bias, a_vnorm=a_vnorm, a_ws=a_ws, a_bias=a_bias,
                   b_qnorm=b_qnorm, b_knorm=b_knorm, w_br_a=w_br_a, w_br_b=w_br_b, w_br_c=w_br_c, w_out=w_out,
                   norm_ffn=norm_ffn, w_ffn_in=w_ffn_in, w_ffn_out=w_ffn_out, norm_ple=norm_ple,
                   w_ple_gate=w_ple_gate, w_ple_proj=w_ple_proj)
    depth = w_in.shape[0]
    bp, sp, _ = x_prompt.shape
    bs, ts, _ = x_sample.shape
    yp = x_prompt.reshape(bp * sp, D_MODEL)
    ys = x_sample.reshape(bs * ts, D_MODEL)
    caches = (cache_b_kidx,) + tuple(c.transpose(0, 1, 3, 4, 2) for c in (cache_b_k, cache_b_v, cache_c_k, cache_c_v))
    new_p, new_s = [], []
    for l in range(depth):
        lw = _prep_layer(l, ts, weights)
        yp, pr_p = _prompt_layer(yp, p_prompt.reshape(depth, bp * sp, PLE_DIM), lw, bp, sp, l)
        ys, pr_s = _sample_layer(ys, p_sample.reshape(depth, bs * ts, PLE_DIM), lw, bs, ts, caches, l)
        new_p.append(pr_p)
        new_s.append(pr_s)

    def stack(states, name, lead, tail):
        return jnp.stack([s[name].reshape(*lead, *tail) for s in states])

    lp, ls = (bp, sp), (bs, ts)
    return (yp.reshape(bp, sp, D_MODEL), ys.reshape(bs, ts, D_MODEL),
            stack(new_p, 'bk', lp, (B_KV_HEADS, B_HEAD_DIM)), stack(new_p, 'bv', lp, (B_KV_HEADS, B_HEAD_DIM)),
            stack(new_p, 'ik', lp, (IDX_DIM,)),
            stack(new_p, 'ck', lp, (C_HEADS, C_HEAD_DIM)), stack(new_p, 'cv', lp, (C_HEADS, C_HEAD_DIM)),
            stack(new_s, 'bk', ls, (B_KV_HEADS, B_HEAD_DIM)), stack(new_s, 'bv', ls, (B_KV_HEADS, B_HEAD_DIM)),
            stack(new_s, 'ik', ls, (IDX_DIM,)),
            stack(new_s, 'ck', ls, (C_HEADS, C_HEAD_DIM)), stack(new_s, 'cv', ls, (C_HEADS, C_HEAD_DIM)),
            stack(new_s, 'av', ls, (A_HALF,)))
```

```python
import functools

import jax
import jax.numpy as jnp
from jax import lax
from jax.experimental import pallas as pl
from jax.experimental.pallas import tpu as pltpu

F32 = jnp.float32
BF16 = jnp.bfloat16
I32 = jnp.int32

D_MODEL = 1024
CHUNK = 64
EPS = 1e-6
A_CHUNK = 128
A_GROUPS = 4
A_GROUP_DIM = 128
A_HALF = 512
B_HEADS = 8
B_KV_HEADS = 2
B_HEAD_DIM = 64
B_TOPK_MAX = 256
IDX_HEADS = 8
IDX_DIM = 32
C_HEADS = 8
C_HEAD_DIM = 64
N_BRANCH = 3
D_FF = 2816
PLE_DIM = 256

LANES = 128
VMEM_LIMIT = 56 * 1024 * 1024

_C_AU, _C_AV, _C_BQ, _C_BK, _C_BV, _C_IQ = 0, 512, 1024, 1536, 1664, 1792
_C_SM, _C_CQ, _C_CK, _C_CV, _C_END = 2048, 2176, 2688, 3200, 3712
DSA_KB = 512

SELECT_MAX_STEPS = 48
SB_SKIP = -104.0
LOG2E = 1.4426950408889634


def _const_spec(shape):
    zeros = (0,) * len(shape)
    return pl.BlockSpec(shape, lambda *_: zeros, pipeline_mode=pl.Buffered(1))


def _rms(x, gain):
    return x * lax.rsqrt(jnp.mean(x * x, axis=-1, keepdims=True) + EPS) * gain


def _dot(a, b):
    return jnp.dot(a, b, preferred_element_type=F32)


def _dot_t(a, b):
    return lax.dot_general(a, b, (((1,), (1,)), ((), ())), preferred_element_type=F32)


def _split_dot(x, m_ref):
    hi = x.astype(BF16)
    lo = (x - hi.astype(F32)).astype(BF16)
    m = m_ref[...]
    return _dot(hi, m) + _dot(lo, m)


def _group_rms(z, g_ref, gain):
    return z * lax.rsqrt(_split_dot(z * z, g_ref) + EPS) * gain


def _proj_kernel(x_ref, nmix_ref, w_ref, wvt_ref, wiwt_ref, avn_ref, qn_ref, kn_ref, g512_ref, g128_ref,
                 ma_ref, abias_ref, *rest, tm, ca, sample, n_alias):
    (oa_ref, qg_ref, qh_ref, bk_ref, bkb_ref, bv_ref, vt_ref, iq_ref, ik_ref, ikb_ref, iwt_ref,
     cq_ref, ck_ref, cv_ref, ckb_ref, cvb_ref, av_ref) = rest[n_alias:]
    hb = _rms(x_ref[...], nmix_ref[...]).astype(BF16)

    def proj(c0, c1):
        return _dot(hb, w_ref[:, c0:c1])

    qn = _group_rms(proj(_C_BQ, _C_BK), g512_ref, qn_ref[...]) * (B_HEAD_DIM ** -0.5 * LOG2E)
    qh_ref[...] = qn.astype(BF16)
    lane = lax.broadcasted_iota(I32, (tm, LANES), 1)
    for h in range(B_HEADS):
        g = h // (B_HEADS // B_KV_HEADS)
        pair = qn[:, (h // 2) * LANES:(h // 2 + 1) * LANES]
        if h % 2 != g:
            pair = pltpu.roll(pair, B_HEAD_DIM, axis=1)
        qg_ref[:, h * LANES:(h + 1) * LANES] = jnp.where(lane // B_HEAD_DIM == g, pair, 0.0).astype(BF16)
    bk = _group_rms(proj(_C_BK, _C_BV), g128_ref, kn_ref[...])
    bk_ref[...] = bk
    bkb_ref[...] = bk.astype(BF16)
    bv_ref[...] = proj(_C_BV, _C_IQ)
    vt = _dot_t(wvt_ref[...], hb).astype(BF16)
    vw = min(DSA_KB, tm)
    for c in range(tm // vw):
        vt_ref[c] = vt[:, c * vw:(c + 1) * vw]
    iq_ref[...] = proj(_C_IQ, _C_SM).astype(BF16)
    ik = proj(_C_SM, _C_CQ)[:, 0:IDX_DIM]
    ik_ref[...] = ik
    ikb_ref[...] = ik.astype(BF16)
    iwt_ref[...] = _dot_t(wiwt_ref[...], hb) * (1.0 / 16.0)
    cq_ref[...] = (proj(_C_CQ, _C_CK) * (C_HEAD_DIM ** -0.5)).astype(BF16)
    ck = proj(_C_CK, _C_CV)
    ck_ref[...] = ck
    ckb_ref[...] = ck.astype(BF16)
    cv = proj(_C_CV, _C_END)
    cv_ref[...] = cv
    cvb_ref[...] = cv.astype(BF16)

    au = jax.nn.gelu(proj(_C_AU, _C_AV))
    av = _rms(jax.nn.gelu(proj(_C_AV, _C_BQ)), avn_ref[...])
    av_ref[...] = av
    avb = av.astype(BF16)
    row = lax.broadcasted_iota(I32, (ca, ca), 0)
    col = lax.broadcasted_iota(I32, (ca, ca), 1)
    if sample:
        vis = (col // 16) == (row // 16)
    else:
        vis = (col // CHUNK) <= (row // CHUNK)
    for g in range(A_GROUPS):
        wm = jnp.where(vis, ma_ref[g], 0.0).astype(BF16)
        cs = slice(g * A_GROUP_DIM, (g + 1) * A_GROUP_DIM)
        for c in range(tm // ca):
            rs = slice(c * ca, (c + 1) * ca)
            mixed = _dot(wm, avb[rs, cs]) + abias_ref[:, cs]
            oa_ref[rs, cs] = (au[rs, cs] * mixed).astype(BF16)


_PROJ_NAMES = ('oa', 'qg', 'qh', 'bk', 'bkb', 'bv', 'vt', 'iq', 'ik', 'ikb', 'iwt', 'cq', 'ck', 'cv', 'ckb', 'cvb', 'av')


_LEAF_NAMES = ('bk', 'bv', 'ik', 'ck', 'cv')


def _proj_call(x, lw, *, tm, ca, sample, layer=0, depth=1, prev=None):
    n = x.shape[0]
    grid = (n // tm,)

    def tile(c):
        return pl.BlockSpec((tm, c), lambda i: (i, 0))

    row_out = {'oa': (A_HALF, BF16), 'qg': (B_HEADS * LANES, BF16), 'qh': (512, BF16), 'bk': (128, F32), 'bkb': (128, BF16),
               'bv': (128, F32), 'iq': (256, BF16), 'ik': (IDX_DIM, F32), 'ikb': (IDX_DIM, BF16),
               'cq': (512, BF16), 'ck': (512, F32), 'cv': (512, F32), 'ckb': (512, BF16), 'cvb': (512, BF16),
               'av': (A_HALF, F32)}
    out_specs, out_shape = [], []
    for name in _PROJ_NAMES:
        if name == 'vt':
            vw = min(DSA_KB, tm)
            out_specs.append(pl.BlockSpec((tm // vw, 128, vw), lambda i: (i, 0, 0)))
            out_shape.append(jax.ShapeDtypeStruct((n // vw, 128, vw), BF16))
        elif name == 'iwt':
            out_specs.append(pl.BlockSpec((IDX_HEADS, tm), lambda i: (0, i)))
            out_shape.append(jax.ShapeDtypeStruct((IDX_HEADS, n), F32))
        elif depth > 1 and name in _LEAF_NAMES:
            c, dt = row_out[name]
            out_specs.append(pl.BlockSpec((None, tm, c), lambda i: (layer, i, 0)))
            out_shape.append(jax.ShapeDtypeStruct((depth, n, c), dt))
        else:
            c, dt = row_out[name]
            out_specs.append(tile(c))
            out_shape.append(jax.ShapeDtypeStruct((n, c), dt))
    ma = lw['ma_sample'] if sample else lw['ma_prompt']
    abias = lw['abias_sample'] if sample else lw['abias_prompt']
    consts = [lw['norm_mix'], lw['w1'], lw['w_vt'], lw['w_iwt'], lw['a_vnorm'], lw['qn'], lw['kn'],
              lw['g512'], lw['g128'], ma, abias]
    shared = [prev[name] for name in _LEAF_NAMES] if prev is not None else []
    aliases = {1 + len(consts) + j: _PROJ_NAMES.index(name) for j, name in enumerate(_LEAF_NAMES)} if shared else {}
    outs = pl.pallas_call(
        functools.partial(_proj_kernel, tm=tm, ca=ca, sample=sample, n_alias=len(shared)),
        grid=grid,
        in_specs=[tile(D_MODEL)] + [_const_spec(c.shape) for c in consts]
        + [pl.BlockSpec(memory_space=pl.ANY)] * len(shared),
        out_specs=out_specs,
        out_shape=out_shape,
        input_output_aliases=aliases,
        compiler_params=pltpu.CompilerParams(dimension_semantics=("arbitrary",), vmem_limit_bytes=VMEM_LIMIT),
        name="proj_sample" if sample else "proj_prompt",
    )(x, *consts, *shared)
    return dict(zip(_PROJ_NAMES, outs))


def _dsa_kernel(qg_ref, iq_ref, iwt_ref, ik_ref, k_ref, vt_ref, tri_ref, o_ref, sc_ref,
                *, qb, kb_size, n_valid, topk):
    q0 = pl.program_id(1) * qb
    nkb = (q0 + qb + kb_size - 1) // kb_size

    cb_size = kb_size
    ncb = nkb
    cpos0 = lax.broadcasted_iota(I32, (cb_size, qb), 0)
    qpos_row = q0 + lax.broadcasted_iota(I32, (1, qb), 1)
    qchunk_row = qpos_row // CHUNK
    n_adm = jnp.minimum((qchunk_row + 1) * CHUNK, n_valid)

    def fold(x, op):
        return op(op(x.reshape(kb_size // 8, 8, qb), axis=0), axis=0, keepdims=True)

    iq = iq_ref[...]
    iq_rows = jnp.concatenate([iq[:, h * IDX_DIM:(h + 1) * IDX_DIM] for h in range(IDX_HEADS)], axis=0)
    iwt = iwt_ref[...]

    def score_body(cb, carry):
        mx, mn = carry
        off = pl.multiple_of(cb * cb_size, cb_size)
        d = _dot_t(ik_ref[pl.ds(off, cb_size), :], iq_rows)
        acc = jnp.zeros((cb_size, qb), F32)
        for h in range(IDX_HEADS):
            acc = acc + jnp.maximum(d[:, h * qb:(h + 1) * qb], 0.0) * iwt[h:h + 1, :]
        kpos = off + cpos0
        adm = (kpos // CHUNK <= qchunk_row) & (kpos < n_valid)
        sc = jnp.where(adm, acc, -jnp.inf)
        sc_ref[pl.ds(off, cb_size), :] = sc
        mx = jnp.maximum(mx, jnp.max(sc.reshape(cb_size // 8, 8, qb), axis=0))
        mn = jnp.minimum(mn, jnp.min(jnp.where(adm, acc, jnp.inf).reshape(cb_size // 8, 8, qb), axis=0))
        return mx, mn

    mx, mn = lax.fori_loop(0, ncb, score_body,
                           (jnp.full((8, qb), -jnp.inf, F32), jnp.full((8, qb), jnp.inf, F32)))
    smax = jnp.max(mx, axis=0, keepdims=True)
    smin = jnp.min(mn, axis=0, keepdims=True)

    acc_rows = 32

    def count(pred):
        def body(cb, cnt):
            off = pl.multiple_of(cb * cb_size, cb_size)
            ind = jnp.where(pred(sc_ref[pl.ds(off, cb_size), :], off), 1.0, 0.0)
            return cnt + jnp.sum(ind.reshape(cb_size // acc_rows, acc_rows, qb), axis=0)
        cnt = lax.fori_loop(0, ncb, body, jnp.zeros((acc_rows, qb), F32))
        return jnp.sum(jnp.sum(cnt.reshape(acc_rows // 8, 8, qb), axis=0), axis=0, keepdims=True)

    def reduce_min(fn):
        def body(cb, acc):
            off = pl.multiple_of(cb * cb_size, cb_size)
            v = fn(sc_ref[pl.ds(off, cb_size), :])
            return jnp.minimum(acc, jnp.min(v.reshape(cb_size // 8, 8, qb), axis=0))
        acc = lax.fori_loop(0, ncb, body, jnp.full((8, qb), jnp.inf, F32))
        return jnp.min(acc, axis=0, keepdims=True)

    k_f = float(topk)
    few = n_adm <= topk
    steps_per_trip = 4

    c_ge0 = count(lambda blk, off: blk >= 0.0)
    c_gt0 = count(lambda blk, off: blk > 0.0)
    zero_tie = (c_ge0 >= k_f) & (c_gt0 < k_f)
    above = c_gt0 >= k_f

    def sel_cond(state):
        t, _, _, _, pending = state
        return (t < SELECT_MAX_STEPS) & (pending > 0.0)

    def sel_body(state):
        t, lo, hi, c_lo, _ = state
        for _ in range(steps_per_trip):
            mid = lo + (hi - lo) * 0.5
            c = count(lambda blk, off: blk >= mid)
            take = c >= k_f
            lo = jnp.where(take, mid, lo)
            hi = jnp.where(take, hi, mid)
            c_lo = jnp.where(take, c, c_lo)
        pending = jnp.max(jnp.where(few | zero_tie | (c_lo == k_f), 0.0, 1.0))
        return t + steps_per_trip, lo, hi, c_lo, pending

    _, lo, _, c_lo, pending = lax.while_loop(
        sel_cond, sel_body,
        (jnp.int32(0), jnp.where(above, 0.0, smin), jnp.where(above, smax, 0.0),
         jnp.where(above, c_ge0, n_adm.astype(F32)), jnp.float32(1.0)))

    def resolve(_):
        v = reduce_min(lambda blk: jnp.where(blk >= lo, blk, jnp.inf))
        return v, k_f - count(lambda blk, off: blk > v)

    v_cap, need_cap = lax.cond(pending > 0.0, resolve, lambda _: (lo, jnp.full((1, qb), jnp.inf, F32)), 0)
    unsettled = jnp.logical_not(few | zero_tie | (c_lo == k_f))
    thr = jnp.where(few, -jnp.inf, jnp.where(zero_tie, 0.0, jnp.where(unsettled, v_cap, lo)))
    need = jnp.where(few, jnp.inf, jnp.where(zero_tie, k_f - c_gt0, jnp.where(unsettled, need_cap, jnp.inf)))
    any_tie = jnp.max(jnp.where((zero_tie | unsettled) & jnp.logical_not(few), 1.0, 0.0))

    @pl.when(any_tie > 0.0)
    def _():
        tri = tri_ref[...]

        def body(cb, seen):
            off = pl.multiple_of(cb * cb_size, cb_size)
            sc = sc_ref[pl.ds(off, cb_size), :]
            eq = sc == thr
            rank = _dot(tri, jnp.where(eq, 1.0, 0.0).astype(BF16)) + seen
            sc_ref[pl.ds(off, cb_size), :] = jnp.where(eq & (rank > need), -jnp.inf, sc)
            return rank[cb_size - 1:cb_size, :]

        lax.fori_loop(0, ncb, body, jnp.zeros((1, qb), F32))

    qg = qg_ref[...]
    q_pairs = [jnp.concatenate([qg[:, (2 * p) * LANES:(2 * p + 1) * LANES],
                                qg[:, (2 * p + 1) * LANES:(2 * p + 2) * LANES]], axis=0)
               for p in range(B_HEADS // 2)]
    hpg = B_HEADS // B_KV_HEADS

    def att_body(kb, carry):
        ms, ls, accs = carry
        off = pl.multiple_of(kb * kb_size, kb_size)
        sc_blk = sc_ref[pl.ds(off, kb_size), :]
        sel = (sc_blk >= thr) & (sc_blk > -jnp.inf)
        bias = jnp.where(sel, 0.0, -jnp.inf)
        k_blk = k_ref[pl.ds(off, kb_size), :]
        vt_blk = vt_ref[kb]
        new_m, new_l, new_a = list(ms), list(ls), list(accs)
        sts = [_dot_t(k_blk, q_pairs[p]) for p in range(B_HEADS // 2)]
        pts, alphas = [], []
        for h in range(B_HEADS):
            s = sts[h // 2][:, (h % 2) * qb:(h % 2 + 1) * qb] + bias
            m_new = jnp.maximum(ms[h], fold(s, jnp.max))
            m_safe = jnp.where(m_new == -jnp.inf, 0.0, m_new)
            pt = jnp.exp2(s - m_safe)
            alpha = jnp.exp2(ms[h] - m_safe)
            new_m[h] = m_new
            new_l[h] = alpha * ls[h] + fold(pt, jnp.sum)
            pts.append(pt.astype(BF16))
            alphas.append(alpha)
        for h in range(B_HEADS):
            g = h // hpg
            new_a[h] = alphas[h] * accs[h] + _dot(vt_blk[g * B_HEAD_DIM:(g + 1) * B_HEAD_DIM, :], pts[h])
        return tuple(new_m), tuple(new_l), tuple(new_a)

    init = (tuple(jnp.full((1, qb), -jnp.inf, F32) for _ in range(B_HEADS)),
            tuple(jnp.zeros((1, qb), F32) for _ in range(B_HEADS)),
            tuple(jnp.zeros((B_HEAD_DIM, qb), F32) for _ in range(B_HEADS)))
    ms, ls, accs = lax.fori_loop(0, nkb, att_body, init)
    out_t = jnp.concatenate([accs[h] / ls[h] for h in range(B_HEADS)], axis=0)
    o_ref[...] = out_t.T.astype(BF16)


def _dsa_call(qg, iq, iwt, ik, k, vt, tri, *, qb, kb_size):
    b, sq, _ = qg.shape
    lp = ik.shape[1]
    nq = sq // qb
    nkb = lp // kb_size
    topk = min(B_TOPK_MAX, lp // 4)

    def qspec(c):
        return pl.BlockSpec((None, qb, c), lambda bi, i: (bi, i, 0))

    def kspec(c):
        return pl.BlockSpec((None, lp, c), lambda bi, i: (bi, 0, 0))

    return pl.pallas_call(
        functools.partial(_dsa_kernel, qb=qb, kb_size=kb_size, n_valid=lp, topk=topk),
        grid=(b, nq),
        in_specs=[qspec(B_HEADS * LANES), qspec(256),
                  pl.BlockSpec((None, IDX_HEADS, qb), lambda bi, i: (bi * nq + i, 0, 0)),
                  kspec(IDX_DIM), kspec(128),
                  pl.BlockSpec((nkb, 128, kb_size), lambda bi, i: (bi, 0, 0)), _const_spec(tri.shape)],
        out_specs=qspec(512),
        out_shape=jax.ShapeDtypeStruct((b, sq, 512), BF16),
        scratch_shapes=[pltpu.VMEM((lp, qb), F32)],
        compiler_params=pltpu.CompilerParams(dimension_semantics=("arbitrary", "arbitrary"),
                                             vmem_limit_bytes=VMEM_LIMIT),
        name="dsa_prompt",
    )(qg, iq, iwt, ik, k, vt, tri)


def _dsa_sample_kernel(qn_ref, iq_ref, iwc_ref, ikn_ref, ktn_ref, vtn_ref, ikp_ref, ktp_ref, vtp_ref, tri_ref, o_ref,
                       sc_ref, *, t, n_past, cb, topk):
    nb = n_past // cb
    hpg = B_HEADS // B_KV_HEADS
    lanes = lax.broadcasted_iota(I32, (t, cb), 1)
    iq = iq_ref[...]
    iq_rows = jnp.concatenate([iq[:, h * IDX_DIM:(h + 1) * IDX_DIM] for h in range(IDX_HEADS)], axis=0)
    iwc = iwc_ref[...]

    def scores(ik_blk):
        w = jnp.maximum(_dot_t(iq_rows, ik_blk), 0.0) * iwc
        s = w[0:t]
        for h in range(1, IDX_HEADS):
            s = s + w[h * t:(h + 1) * t]
        return s

    def fold_lanes(x, op):
        part = x[:, 0:LANES]
        for c in range(1, cb // LANES):
            part = op(part, x[:, c * LANES:(c + 1) * LANES])
        return part

    def past_scores(kb, carry):
        mx, mn = carry
        off = pl.multiple_of(kb * cb, cb)
        sc = scores(ikp_ref[pl.ds(off, cb), :].astype(BF16))
        sc_ref[kb] = sc
        return jnp.maximum(mx, fold_lanes(sc, jnp.maximum)), jnp.minimum(mn, fold_lanes(sc, jnp.minimum))

    mx, mn = lax.fori_loop(0, nb, past_scores,
                           (jnp.full((t, LANES), -jnp.inf, F32), jnp.full((t, LANES), jnp.inf, F32)))
    sc_new = scores(ikn_ref[...])
    valid = lanes < t
    sc_ref[nb] = jnp.where(valid, sc_new, -jnp.inf)
    mx = jnp.maximum(mx, fold_lanes(jnp.where(valid, sc_new, -jnp.inf), jnp.maximum))
    mn = jnp.minimum(mn, fold_lanes(jnp.where(valid, sc_new, jnp.inf), jnp.minimum))
    smax = jnp.max(mx, axis=1, keepdims=True)
    smin = jnp.min(mn, axis=1, keepdims=True)

    def count(pred):
        def body(kb, cnt):
            ind = jnp.where(pred(sc_ref[kb], kb * cb + lanes), 1.0, 0.0)
            return cnt + fold_lanes(ind, jnp.add)
        cnt = lax.fori_loop(0, nb + 1, body, jnp.zeros((t, LANES), F32))
        return jnp.sum(cnt, axis=1, keepdims=True)

    k_f = float(topk)
    steps_per_trip = 4

    c_ge0 = count(lambda blk, pos: blk >= 0.0)
    c_gt0 = count(lambda blk, pos: blk > 0.0)
    zero_tie = (c_ge0 >= k_f) & (c_gt0 < k_f)
    above = c_gt0 >= k_f

    def sel_cond(state):
        tt, _, _, _, pending = state
        return (tt < SELECT_MAX_STEPS) & (pending > 0.0)

    def sel_body(state):
        tt, lo, hi, c_lo, _ = state
        for _ in range(steps_per_trip):
            mid = lo + (hi - lo) * 0.5
            c = count(lambda blk, pos: blk >= mid)
            take = c >= k_f
            lo = jnp.where(take, mid, lo)
            hi = jnp.where(take, hi, mid)
            c_lo = jnp.where(take, c, c_lo)
        pending = jnp.max(jnp.where(zero_tie | (c_lo == k_f), 0.0, 1.0))
        return tt + steps_per_trip, lo, hi, c_lo, pending

    _, lo, _, c_lo, pending = lax.while_loop(
        sel_cond, sel_body,
        (jnp.int32(0), jnp.where(above, 0.0, smin), jnp.where(above, smax, 0.0),
         jnp.where(above, c_ge0, float(n_past + t)), jnp.float32(1.0)))

    def resolve(_):
        def body(kb, acc):
            blk = sc_ref[kb]
            return jnp.minimum(acc, fold_lanes(jnp.where(blk >= lo, blk, jnp.inf), jnp.minimum))
        acc = lax.fori_loop(0, nb + 1, body, jnp.full((t, LANES), jnp.inf, F32))
        v = jnp.min(acc, axis=1, keepdims=True)
        return v, k_f - count(lambda blk, pos: blk > v)

    v_cap, need_cap = lax.cond(pending > 0.0, resolve, lambda _: (lo, jnp.full((t, 1), jnp.inf, F32)), 0)
    unsettled = jnp.logical_not(zero_tie | (c_lo == k_f))
    thr = jnp.where(zero_tie, 0.0, jnp.where(unsettled, v_cap, lo))
    need = jnp.where(zero_tie, k_f - c_gt0, jnp.where(unsettled, need_cap, jnp.inf))
    any_tie = jnp.max(jnp.where(zero_tie | unsettled, 1.0, 0.0))

    @pl.when(any_tie > 0.0)
    def _():
        tri = tri_ref[...]

        def body(kb, seen):
            sc = sc_ref[kb]
            eq = sc == thr
            rank = _dot(jnp.where(eq, 1.0, 0.0).astype(BF16), tri) + seen
            sc_ref[kb] = jnp.where(eq & (rank > need), -jnp.inf, sc)
            return rank[:, cb - 1:cb]

        lax.fori_loop(0, nb + 1, body, jnp.zeros((t, 1), F32))

    qn = qn_ref[...]
    q_groups = [jnp.concatenate([qn[:, h * B_HEAD_DIM:(h + 1) * B_HEAD_DIM]
                                 for h in range(g * hpg, (g + 1) * hpg)], axis=0)
                for g in range(B_KV_HEADS)]

    def attend(carry, kb, kt, vt):
        ms, ls, accs = carry
        sc_blk = sc_ref[kb]
        sel = (sc_blk >= thr) & (sc_blk > -jnp.inf)
        bias = jnp.where(sel, 0.0, -jnp.inf)
        bias_rows = jnp.concatenate([bias] * hpg, axis=0)
        new_m, new_l, new_a = [], [], []
        for g in range(B_KV_HEADS):
            s = _dot(q_groups[g], kt(g)) + bias_rows
            m_new = jnp.maximum(ms[g], jnp.max(s, axis=1, keepdims=True))
            m_safe = jnp.where(m_new == -jnp.inf, 0.0, m_new)
            p = jnp.exp2(s - m_safe)
            alpha = jnp.exp2(ms[g] - m_safe)
            new_m.append(m_new)
            new_l.append(alpha * ls[g] + jnp.sum(p, axis=1, keepdims=True))
            new_a.append(alpha * accs[g] + _dot_t(p.astype(BF16), vt(g)))
        return tuple(new_m), tuple(new_l), tuple(new_a)

    def past_attend(kb, carry):
        off = pl.multiple_of(kb * cb, cb)
        return attend(carry, kb, lambda g: ktp_ref[g, :, pl.ds(off, cb)].astype(BF16),
                      lambda g: vtp_ref[g, :, pl.ds(off, cb)].astype(BF16))

    init = (tuple(jnp.full((hpg * t, 1), -jnp.inf, F32) for _ in range(B_KV_HEADS)),
            tuple(jnp.zeros((hpg * t, 1), F32) for _ in range(B_KV_HEADS)),
            tuple(jnp.zeros((hpg * t, B_HEAD_DIM), F32) for _ in range(B_KV_HEADS)))
    carry = lax.fori_loop(0, nb, past_attend, init)
    _, ls, accs = attend(carry, nb, lambda g: ktn_ref[g], lambda g: vtn_ref[g])
    for g in range(B_KV_HEADS):
        out = accs[g] / ls[g]
        for hh in range(hpg):
            h = g * hpg + hh
            o_ref[:, h * B_HEAD_DIM:(h + 1) * B_HEAD_DIM] = out[hh * t:(hh + 1) * t].astype(BF16)


def _dsa_sample_call(qn, iq, iwc, ikn, ktn, vtn, ikp, ktp, vtp, tri, layer, *, cb):
    b, t, _ = qn.shape
    n_past = ikp.shape[2]
    topk = min(B_TOPK_MAX, (n_past + t) // 4)
    assert n_past % cb == 0 and n_past + t > topk

    def bspec(*shape):
        zeros = (0,) * len(shape)
        return pl.BlockSpec((None,) + shape, lambda bi: (bi,) + zeros)

    def cspec(*shape):
        zeros = (0,) * len(shape)
        return pl.BlockSpec((None, None) + shape, lambda bi: (layer, bi) + zeros)

    return pl.pallas_call(
        functools.partial(_dsa_sample_kernel, t=t, n_past=n_past, cb=cb, topk=topk),
        grid=(b,),
        in_specs=[bspec(t, 512), bspec(t, 256), bspec(IDX_HEADS * t, 1),
                  bspec(cb, IDX_DIM), bspec(B_KV_HEADS, B_HEAD_DIM, cb), bspec(B_KV_HEADS, B_HEAD_DIM, cb),
                  cspec(n_past, IDX_DIM), cspec(B_KV_HEADS, B_HEAD_DIM, n_past),
                  cspec(B_KV_HEADS, B_HEAD_DIM, n_past), _const_spec(tri.shape)],
        out_specs=bspec(t, 512),
        out_shape=jax.ShapeDtypeStruct((b, t, 512), BF16),
        scratch_shapes=[pltpu.VMEM((n_past // cb + 1, t, cb), F32)],
        compiler_params=pltpu.CompilerParams(dimension_semantics=("arbitrary",), vmem_limit_bytes=VMEM_LIMIT),
        name="dsa_sample",
    )(qn, iq, iwc, ikn, ktn, vtn, ikp, ktp, vtp, tri)


SB_WIDE = 256
SB_GROUP = 4
SB_SAMPLE_WINDOW = 1024


def _sb_tri2(n):
    r = jnp.arange(2 * n)[:, None] % n
    c = jnp.arange(n)[None, :]
    return jnp.where(r > c, 1.0, 0.0).astype(BF16)


def _sb_stage(z_fn, pv, tri2_ref, carries, accs, vis):
    tri2 = tri2_ref[...]
    new_c, new_a = list(carries), list(accs)
    for h0 in range(0, C_HEADS, SB_GROUP):
        heads = range(h0, h0 + SB_GROUP)
        zs = [z_fn(h) for h in heads]
        sps = [jnp.maximum(z, 0.0) + jnp.log(1.0 + jnp.exp(-jnp.abs(z))) for z in zs]
        stays = [-sp if vis is None else jnp.where(vis, -sp, 0.0) for sp in sps]
        afters = []
        for ls in stays:
            hi = ls.astype(BF16)
            lo = (ls - hi.astype(F32)).astype(BF16)
            afters.append(_dot(jnp.concatenate([hi, lo], axis=1), tri2))
        for i, h in enumerate(heads):
            w = jnp.exp((zs[i] - sps[i]) + (afters[i] + carries[h]))
            if vis is not None:
                w = jnp.where(vis, w, 0.0)
            new_a[h] = accs[h] + pv(h, w.astype(BF16))
            new_c[h] = carries[h] + jnp.sum(stays[i], axis=1, keepdims=True)
    return tuple(new_c), tuple(new_a)


def _sb_max_carry(carries):
    m = carries[0]
    for c in carries[1:]:
        m = jnp.maximum(m, c)
    return jnp.max(m)


def _sb_init(qb, d):
    return (tuple(jnp.zeros((qb, 1), F32) for _ in range(C_HEADS)),
            tuple(jnp.zeros((qb, d), F32) for _ in range(C_HEADS)))


def _sb_prompt_kernel(q_ref, k_ref, v_ref, tri_ref, o_ref, *, qb):
    i = pl.program_id(1)
    q = q_ref[...]
    lane = lax.broadcasted_iota(I32, (qb, LANES), 1)
    q_heads = []
    for h in range(C_HEADS):
        pair = q[:, (h // 2) * LANES:(h // 2 + 1) * LANES]
        q_heads.append(jnp.where(lane // C_HEAD_DIM == h % 2, pair, jnp.zeros_like(pair)))

    def block(off, carries, accs, vis):
        k_blk = k_ref[pl.ds(off, qb), :]
        v_blk = v_ref[pl.ds(off, qb), :]
        z_fn = lambda h: _dot_t(q_heads[h], k_blk[:, (h // 2) * LANES:(h // 2 + 1) * LANES])
        pv = lambda h, w: _dot(w, v_blk[:, (h // 2) * LANES:(h // 2 + 1) * LANES])
        return _sb_stage(z_fn, pv, tri_ref, carries, accs, vis)

    rows = lax.broadcasted_iota(I32, (qb, qb), 0)
    cols = lax.broadcasted_iota(I32, (qb, qb), 1)
    carries, accs = _sb_init(qb, LANES)
    carries, accs = block(pl.multiple_of(i * qb, qb), carries, accs, cols < rows)

    def cond(state):
        j, mx, _, _ = state
        return (j < i) & (mx > SB_SKIP)

    def body(state):
        j, _, carries, accs = state
        off = pl.multiple_of((i - j - 1) * qb, qb)
        carries, accs = block(off, carries, accs, None)
        return j + 1, _sb_max_carry(carries), carries, accs

    _, _, _, accs = lax.while_loop(cond, body, (jnp.int32(0), _sb_max_carry(carries), carries, accs))
    for p in range(C_HEADS // 2):
        o_ref[:, p * LANES:(p + 1) * LANES] = jnp.where(lane < C_HEAD_DIM, accs[2 * p], accs[2 * p + 1]).astype(BF16)


def _sb_prompt_call(cq, ckb, cvb, lw, *, qb):
    b, s, _ = cq.shape
    qspec = pl.BlockSpec((None, qb, 512), lambda bi, i: (bi, i, 0))
    kspec = pl.BlockSpec((None, s, 512), lambda bi, i: (bi, 0, 0))
    return pl.pallas_call(
        functools.partial(_sb_prompt_kernel, qb=qb),
        grid=(b, s // qb),
        in_specs=[qspec, kspec, kspec, _const_spec(lw['tri2'].shape)],
        out_specs=qspec,
        out_shape=jax.ShapeDtypeStruct((b, s, 512), BF16),
        compiler_params=pltpu.CompilerParams(dimension_semantics=("arbitrary", "arbitrary"),
                                             vmem_limit_bytes=VMEM_LIMIT),
        name="stick_prompt",
    )(cq, ckb, cvb, lw['tri2'])


def _sb_sample_kernel(q_ref, knt_ref, vnt_ref, kpt_ref, vpt_ref, tri1_ref, tri2_ref, o_ref, done_ref, *, qb, n_hist):
    q = q_ref[...]
    q_heads = [q[:, h * C_HEAD_DIM:(h + 1) * C_HEAD_DIM] for h in range(C_HEADS)]
    lane = lax.broadcasted_iota(I32, (qb, LANES), 1)
    rows = lax.broadcasted_iota(I32, (qb, LANES), 0)

    def block(kt, vt, tri_ref, carries, accs, vis):
        z_fn = lambda h: _dot(q_heads[h], kt(h))
        pv = lambda h, w: _dot_t(w, vt(h))
        return _sb_stage(z_fn, pv, tri_ref, carries, accs, vis)

    carries, accs = _sb_init(qb, C_HEAD_DIM)
    carries, accs = block(lambda h: knt_ref[h], lambda h: vnt_ref[h], tri1_ref, carries, accs, lane < rows)

    def cond(state):
        j, mx, _, _ = state
        return (j < n_hist // SB_WIDE) & (mx > SB_SKIP)

    def body(state):
        j, _, carries, accs = state
        off = pl.multiple_of(n_hist - (j + 1) * SB_WIDE, SB_WIDE)
        carries, accs = block(lambda h: kpt_ref[h, :, pl.ds(off, SB_WIDE)].astype(BF16),
                              lambda h: vpt_ref[h, :, pl.ds(off, SB_WIDE)].astype(BF16),
                              tri2_ref, carries, accs, None)
        return j + 1, _sb_max_carry(carries), carries, accs

    _, mx, _, accs = lax.while_loop(cond, body, (jnp.int32(0), _sb_max_carry(carries), carries, accs))
    for h in range(C_HEADS):
        o_ref[:, h * C_HEAD_DIM:(h + 1) * C_HEAD_DIM] = accs[h].astype(BF16)
    done_ref[...] = jnp.full(done_ref.shape, jnp.where(mx > SB_SKIP, 0.0, 1.0), F32)


def _sb_sample_call(cq, knt, vnt, kpt, vpt, lw, layer, n_hist):
    b, t, _ = cq.shape
    n_past = kpt.shape[-1]
    qspec = pl.BlockSpec((None, t, 512), lambda bi: (bi, 0, 0))
    nspec = pl.BlockSpec((None, C_HEADS, C_HEAD_DIM, LANES), lambda bi: (bi, 0, 0, 0))
    pspec = pl.BlockSpec((None, None, C_HEADS, C_HEAD_DIM, n_hist), lambda bi: (layer, bi, 0, 0, n_past // n_hist - 1))
    return pl.pallas_call(
        functools.partial(_sb_sample_kernel, qb=t, n_hist=n_hist),
        grid=(b,),
        in_specs=[qspec, nspec, nspec, pspec, pspec, _const_spec(lw['tri1'].shape), _const_spec(lw['tri2'].shape)],
        out_specs=[qspec, pl.BlockSpec((None, 8, LANES), lambda bi: (bi, 0, 0))],
        out_shape=[jax.ShapeDtypeStruct((b, t, 512), BF16), jax.ShapeDtypeStruct((b, 8, LANES), F32)],
        compiler_params=pltpu.CompilerParams(dimension_semantics=("arbitrary",), vmem_limit_bytes=VMEM_LIMIT),
        name="stick_sample",
    )(cq, knt, vnt, kpt, vpt, lw['tri1'], lw['tri2'])


def _sb_sample(cq, knt, vnt, kpt, vpt, lw, layer):
    n_past = kpt.shape[-1]
    n_hist = min(SB_SAMPLE_WINDOW, n_past)
    oc, done = _sb_sample_call(cq, knt, vnt, kpt, vpt, lw, layer, n_hist)
    if n_hist == n_past:
        return oc
    return lax.cond(jnp.all(done > 0.0), lambda: oc,
                    lambda: _sb_sample_call(cq, knt, vnt, kpt, vpt, lw, layer, n_past)[0])


def _merge_kernel(x_ref, nmix_ref, wg_ref, gb_ref, oa_ref, ob_ref, oc_ref, wa_ref, wb_ref, wc_ref, wo_ref, o_ref):
    x = x_ref[...]
    hb = _rms(x, nmix_ref[...]).astype(BF16)
    merged = None
    for idx, (br_ref, w_ref) in enumerate(((oa_ref, wa_ref), (ob_ref, wb_ref), (oc_ref, wc_ref))):
        cs = slice(idx * D_MODEL, (idx + 1) * D_MODEL)
        gate = jax.nn.sigmoid(_dot(hb, wg_ref[:, cs]) + gb_ref[:, cs])
        term = gate * _dot(br_ref[...], w_ref[...])
        merged = term if merged is None else merged + term
    o_ref[...] = x + _dot(merged.astype(BF16), wo_ref[...])


def _merge_call(x, oa, ob, oc, lw, *, tm, name):
    n = x.shape[0]

    def tile(c):
        return pl.BlockSpec((tm, c), lambda i: (i, 0))

    consts_a = [lw['norm_mix'], lw['w_gate'], lw['gate_bias']]
    consts_b = [lw['w_br_a'], lw['w_br_b'], lw['w_br_c'], lw['w_out']]
    return pl.pallas_call(
        _merge_kernel,
        grid=(n // tm,),
        in_specs=([tile(D_MODEL)] + [_const_spec(c.shape) for c in consts_a] + [tile(512)] * 3
                  + [_const_spec(c.shape) for c in consts_b]),
        out_specs=tile(D_MODEL),
        out_shape=jax.ShapeDtypeStruct((n, D_MODEL), F32),
        compiler_params=pltpu.CompilerParams(dimension_semantics=("arbitrary",), vmem_limit_bytes=VMEM_LIMIT),
        name=name,
    )(x, *consts_a, oa, ob, oc, *consts_b)


def _ffn_kernel(x_ref, p_ref, nffn_ref, wfi_ref, wfo_ref, nple_ref, wpg_ref, wpp_ref, o_ref):
    x = x_ref[...]
    hf = _rms(x, nffn_ref[...]).astype(BF16)
    g = _dot(hf, wfi_ref[:, 0:D_FF])
    up = _dot(hf, wfi_ref[:, D_FF:2 * D_FF])
    x = x + _dot((jax.nn.silu(g) * up).astype(BF16), wfo_ref[...])
    hp = _rms(x, nple_ref[...]).astype(BF16)
    gate = jax.nn.sigmoid(_dot(hp, wpg_ref[...]))
    o_ref[...] = x + gate * _dot(p_ref[...].astype(BF16), wpp_ref[...])


def _ffn_call(x, p, layer, lw, *, tm, name):
    n = x.shape[0]
    consts = [lw['norm_ffn'], lw['w_ffn_in'], lw['w_ffn_out'], lw['norm_ple'], lw['w_ple_gate'], lw['w_ple_proj']]
    return pl.pallas_call(
        _ffn_kernel,
        grid=(n // tm,),
        in_specs=[pl.BlockSpec((tm, D_MODEL), lambda i: (i, 0)),
                  pl.BlockSpec((None, tm, PLE_DIM), lambda i: (layer, i, 0))]
        + [_const_spec(c.shape) for c in consts],
        out_specs=pl.BlockSpec((tm, D_MODEL), lambda i: (i, 0)),
        out_shape=jax.ShapeDtypeStruct((n, D_MODEL), F32),
        compiler_params=pltpu.CompilerParams(dimension_semantics=("arbitrary",), vmem_limit_bytes=VMEM_LIMIT),
        name=name,
    )(x, p, *consts)


def _block_diag_mean(n):
    r = jnp.arange(n)
    return jnp.where((r[:, None] // B_HEAD_DIM) == (r[None, :] // B_HEAD_DIM), 1.0 / B_HEAD_DIM, 0.0).astype(BF16)


def _rank_tri(n):
    r = jnp.arange(n)
    return jnp.where(r[None, :] <= r[:, None], 1.0, 0.0).astype(BF16)


def _prep_layer(l, t_sample, w):
    w_in = w['w_in'][l]
    small = jnp.pad(w_in[:, 2048:2088], ((0, 0), (0, LANES - IDX_DIM - IDX_HEADS)))
    w1 = jnp.concatenate([w_in[:, :2048], small, w_in[:, 2088:3624]], axis=1).astype(BF16)
    a_ws = w['a_ws'][l]
    a_bias_full = jnp.repeat(w['a_bias'][l].T, A_GROUP_DIM, axis=1)
    row = lambda a: a.reshape(1, -1)
    return {
        'norm_mix': row(w['norm_mix'][l]), 'w1': w1,
        'w_vt': w_in[:, 1664:1792].T.astype(BF16), 'w_iwt': w_in[:, 2080:2088].T.astype(BF16),
        'a_vnorm': row(w['a_vnorm'][l]),
        'qn': row(jnp.tile(w['b_qnorm'][l], B_HEADS)), 'kn': row(jnp.tile(w['b_knorm'][l], B_KV_HEADS)),
        'g512': _block_diag_mean(512), 'g128': _block_diag_mean(128),
        'tri1': _sb_tri2(LANES), 'tri2': _sb_tri2(SB_WIDE),
        'tri_rank': _rank_tri(DSA_KB), 'tri_rank_t': _rank_tri(2 * DSA_KB).T,
        'ma_prompt': a_ws, 'abias_prompt': a_bias_full,
        'ma_sample': jnp.tile(a_ws[:, :t_sample, :t_sample], (1, 16, 16)),
        'abias_sample': jnp.tile(a_bias_full[:t_sample], (16, 1)),
        'w_gate': w_in[:, 3624:].astype(BF16), 'gate_bias': row(w['gate_bias'][l]),
        'w_br_a': w['w_br_a'][l].astype(BF16), 'w_br_b': w['w_br_b'][l].astype(BF16),
        'w_br_c': w['w_br_c'][l].astype(BF16), 'w_out': w['w_out'][l].astype(BF16),
        'norm_ffn': row(w['norm_ffn'][l]), 'w_ffn_in': w['w_ffn_in'][l].astype(BF16),
        'w_ffn_out': w['w_ffn_out'][l].astype(BF16), 'norm_ple': row(w['norm_ple'][l]),
        'w_ple_gate': w['w_ple_gate'][l].astype(BF16), 'w_ple_proj': w['w_ple_proj'][l].astype(BF16),
    }


def _iwt_blocks(iwt, qb):
    return iwt.reshape(IDX_HEADS, -1, qb).transpose(1, 0, 2)


def _prompt_layer(x, p, lw, b, s, layer, depth, prev):
    pr = _proj_call(x, lw, tm=512, ca=A_CHUNK, sample=False, layer=layer, depth=depth, prev=prev)
    r3 = lambda a: a.reshape(b, s, a.shape[-1])
    ob = _dsa_call(r3(pr['qg']), r3(pr['iq']), _iwt_blocks(pr['iwt'], 256), r3(pr['ikb']), r3(pr['bkb']), pr['vt'],
                   lw['tri_rank'], qb=256, kb_size=DSA_KB)
    oc = _sb_prompt_call(r3(pr['cq']), r3(pr['ckb']), r3(pr['cvb']), lw, qb=SB_WIDE)
    x = _merge_call(x, pr['oa'], ob.reshape(b * s, 512), oc.reshape(b * s, 512), lw, tm=256, name="merge_prompt")
    x = _ffn_call(x, p, layer, lw, tm=256, name="ffn_prompt")
    return x, pr


def _sample_layer(x, p, lw, b, t, caches, layer):
    pik, pbk_t, pbv_t, pck_t, pcv_t = caches
    pr = _proj_call(x, lw, tm=b * t, ca=b * t, sample=True)
    r3 = lambda a: a.reshape(b, t, a.shape[-1])
    cb = 2 * DSA_KB

    def new_t(a, heads, width):
        at = a.reshape(b, t, heads, -1).transpose(0, 2, 3, 1).astype(BF16)
        return jnp.pad(at, ((0, 0), (0, 0), (0, 0), (0, width - t)))

    ikn = jnp.pad(r3(pr['ikb']), ((0, 0), (0, cb - t), (0, 0)))
    iwc = _iwt_blocks(pr['iwt'], t).reshape(b, IDX_HEADS * t, 1)
    ob = _dsa_sample_call(r3(pr['qh']), r3(pr['iq']), iwc, ikn, new_t(pr['bk'], B_KV_HEADS, cb),
                          new_t(pr['bv'], B_KV_HEADS, cb), pik, pbk_t, pbv_t, lw['tri_rank_t'], layer, cb=cb)
    oc = _sb_sample(r3(pr['cq']), new_t(pr['ck'], C_HEADS, LANES), new_t(pr['cv'], C_HEADS, LANES),
                    pck_t, pcv_t, lw, layer)
    x = _merge_call(x, pr['oa'], ob.reshape(b * t, 512), oc.reshape(b * t, 512), lw, tm=b * t, name="merge_sample")
    x = _ffn_call(x, p, layer, lw, tm=b * t, name="ffn_sample")
    return x, pr


def kernel(x_prompt, x_sample, cache_b_k, cache_b_v, cache_b_kidx, cache_c_k, cache_c_v, p_prompt, p_sample,
           norm_mix, w_in, gate_bias, a_vnorm, a_ws, a_bias, b_qnorm, b_knorm, w_br_a, w_br_b, w_br_c, w_out,
           norm_ffn, w_ffn_in, w_ffn_out, norm_ple, w_ple_gate, w_ple_proj):
    weights = dict(norm_mix=norm_mix, w_in=w_in, gate_bias=gate_bias, a_vnorm=a_vnorm, a_ws=a_ws, a_bias=a_bias,
                   b_qnorm=b_qnorm, b_knorm=b_knorm, w_br_a=w_br_a, w_br_b=w_br_b, w_br_c=w_br_c, w_out=w_out,
                   norm_ffn=norm_ffn, w_ffn_in=w_ffn_in, w_ffn_out=w_ffn_out, norm_ple=norm_ple,
                   w_ple_gate=w_ple_gate, w_ple_proj=w_ple_proj)
    depth = w_in.shape[0]
    bp, sp, _ = x_prompt.shape
    bs, ts, _ = x_sample.shape
    yp = x_prompt.reshape(bp * sp, D_MODEL)
    ys = x_sample.reshape(bs * ts, D_MODEL)
    caches = (cache_b_kidx,) + tuple(c.transpose(0, 1, 3, 4, 2) for c in (cache_b_k, cache_b_v, cache_c_k, cache_c_v))
    new_p, new_s = [], []
    for l in range(depth):
        lw = _prep_layer(l, ts, weights)
        yp, pr_p = _prompt_layer(yp, p_prompt.reshape(depth, bp * sp, PLE_DIM), lw, bp, sp, l, depth,
                                 new_p[-1] if new_p else None)
        ys, pr_s = _sample_layer(ys, p_sample.reshape(depth, bs * ts, PLE_DIM), lw, bs, ts, caches, l)
        new_p.append(pr_p)
        new_s.append(pr_s)

    def stack(states, name, lead, tail):
        return jnp.stack([s[name].reshape(*lead, *tail) for s in states])

    def shared(name, tail):
        return new_p[-1][name].reshape(depth, bp, sp, *tail)

    ls = (bs, ts)
    return (yp.reshape(bp, sp, D_MODEL), ys.reshape(bs, ts, D_MODEL),
            shared('bk', (B_KV_HEADS, B_HEAD_DIM)), shared('bv', (B_KV_HEADS, B_HEAD_DIM)), shared('ik', (IDX_DIM,)),
            shared('ck', (C_HEADS, C_HEAD_DIM)), shared('cv', (C_HEADS, C_HEAD_DIM)),
            stack(new_s, 'bk', ls, (B_KV_HEADS, B_HEAD_DIM)), stack(new_s, 'bv', ls, (B_KV_HEADS, B_HEAD_DIM)),
            stack(new_s, 'ik', ls, (IDX_DIM,)),
            stack(new_s, 'ck', ls, (C_HEADS, C_HEAD_DIM)), stack(new_s, 'cv', ls, (C_HEADS, C_HEAD_DIM)),
            stack(new_s, 'av', ls, (A_HALF,)))
```

```python
import functools

import jax
import jax.numpy as jnp
from jax import lax
from jax.experimental import pallas as pl
from jax.experimental.pallas import tpu as pltpu

F32 = jnp.float32
BF16 = jnp.bfloat16
I32 = jnp.int32

D_MODEL = 1024
CHUNK = 64
EPS = 1e-6
A_CHUNK = 128
A_GROUPS = 4
A_GROUP_DIM = 128
A_HALF = 512
B_HEADS = 8
B_KV_HEADS = 2
B_HEAD_DIM = 64
B_TOPK_MAX = 256
IDX_HEADS = 8
IDX_DIM = 32
C_HEADS = 8
C_HEAD_DIM = 64
N_BRANCH = 3
D_FF = 2816
PLE_DIM = 256

LANES = 128
VMEM_LIMIT = 56 * 1024 * 1024

_C_AU, _C_AV, _C_BQ, _C_BK, _C_BV, _C_IQ = 0, 512, 1024, 1536, 1664, 1792
_C_SM, _C_CQ, _C_CK, _C_CV, _C_END = 2048, 2176, 2688, 3200, 3712
DSA_KB = 512

SELECT_MAX_STEPS = 48
SB_SKIP = -104.0
LOG2E = 1.4426950408889634


def _const_spec(shape):
    zeros = (0,) * len(shape)
    return pl.BlockSpec(shape, lambda *_: zeros, pipeline_mode=pl.Buffered(1))


def _rms(x, gain):
    return x * lax.rsqrt(jnp.mean(x * x, axis=-1, keepdims=True) + EPS) * gain


def _dot(a, b):
    return jnp.dot(a, b, preferred_element_type=F32)


def _dot_t(a, b):
    return lax.dot_general(a, b, (((1,), (1,)), ((), ())), preferred_element_type=F32)


def _split_dot(x, m_ref):
    hi = x.astype(BF16)
    lo = (x - hi.astype(F32)).astype(BF16)
    m = m_ref[...]
    return _dot(hi, m) + _dot(lo, m)


def _group_rms(z, g_ref, gain):
    return z * lax.rsqrt(_split_dot(z * z, g_ref) + EPS) * gain


def _proj_kernel(x_ref, nmix_ref, w_ref, wvt_ref, wiwt_ref, avn_ref, qn_ref, kn_ref, g512_ref, g128_ref,
                 ma_ref, abias_ref, *rest, tm, ca, sample, names, n_alias):
    out = dict(zip(names, rest[n_alias:]))
    hb = _rms(x_ref[...], nmix_ref[...]).astype(BF16)

    def proj(c0, c1):
        return _dot(hb, w_ref[:, c0:c1])

    qn = _group_rms(proj(_C_BQ, _C_BK), g512_ref, qn_ref[...]) * (B_HEAD_DIM ** -0.5 * LOG2E)
    if 'qh' in out:
        out['qh'][...] = qn.astype(BF16)
    lane = lax.broadcasted_iota(I32, (tm, LANES), 1)
    for h in range(B_HEADS if 'qg' in out else 0):
        g = h // (B_HEADS // B_KV_HEADS)
        pair = qn[:, (h // 2) * LANES:(h // 2 + 1) * LANES]
        if h % 2 != g:
            pair = pltpu.roll(pair, B_HEAD_DIM, axis=1)
        out['qg'][:, h * LANES:(h + 1) * LANES] = jnp.where(lane // B_HEAD_DIM == g, pair, 0.0).astype(BF16)
    bk = _group_rms(proj(_C_BK, _C_BV), g128_ref, kn_ref[...])
    out['bk'][...] = bk
    if 'bkb' in out:
        out['bkb'][...] = bk.astype(BF16)
    out['bv'][...] = proj(_C_BV, _C_IQ)
    if 'vt' in out:
        vt = _dot_t(wvt_ref[...], hb).astype(BF16)
        for c in range(tm // DSA_KB):
            out['vt'][c] = vt[:, c * DSA_KB:(c + 1) * DSA_KB]
    out['iq'][...] = proj(_C_IQ, _C_SM).astype(BF16)
    ik = proj(_C_SM, _C_CQ)[:, 0:IDX_DIM]
    out['ik'][...] = ik
    out['ikb'][...] = ik.astype(BF16)
    out['iwt'][...] = _dot_t(wiwt_ref[...], hb) * (1.0 / 16.0)
    out['cq'][...] = (proj(_C_CQ, _C_CK) * (C_HEAD_DIM ** -0.5)).astype(BF16)
    ck = proj(_C_CK, _C_CV)
    out['ck'][...] = ck
    cv = proj(_C_CV, _C_END)
    out['cv'][...] = cv
    if 'ckb' in out:
        out['ckb'][...] = ck.astype(BF16)
        out['cvb'][...] = cv.astype(BF16)

    au = jax.nn.gelu(proj(_C_AU, _C_AV))
    av = _rms(jax.nn.gelu(proj(_C_AV, _C_BQ)), avn_ref[...])
    if 'av' in out:
        out['av'][...] = av
    avb = av.astype(BF16)
    row = lax.broadcasted_iota(I32, (ca, ca), 0)
    col = lax.broadcasted_iota(I32, (ca, ca), 1)
    if sample:
        vis = (col // 16) == (row // 16)
    else:
        vis = (col // CHUNK) <= (row // CHUNK)
    for g in range(A_GROUPS):
        wm = jnp.where(vis, ma_ref[g], 0.0).astype(BF16)
        cs = slice(g * A_GROUP_DIM, (g + 1) * A_GROUP_DIM)
        for c in range(tm // ca):
            rs = slice(c * ca, (c + 1) * ca)
            mixed = _dot(wm, avb[rs, cs]) + abias_ref[:, cs]
            out['oa'][rs, cs] = (au[rs, cs] * mixed).astype(BF16)


_PROMPT_OUTS = ('oa', 'qg', 'bk', 'bkb', 'bv', 'vt', 'iq', 'ik', 'ikb', 'iwt', 'cq', 'ck', 'cv', 'ckb', 'cvb')
_SAMPLE_OUTS = ('oa', 'qh', 'bk', 'bv', 'iq', 'ik', 'ikb', 'iwt', 'cq', 'ck', 'cv', 'av')


_LEAF_NAMES = ('bk', 'bv', 'ik', 'ck', 'cv')


def _proj_call(x, lw, *, tm, ca, sample, layer=0, depth=1, prev=None):
    n = x.shape[0]
    grid = (n // tm,)

    def tile(c):
        return pl.BlockSpec((tm, c), lambda i: (i, 0))

    row_out = {'oa': (A_HALF, BF16), 'qg': (B_HEADS * LANES, BF16), 'qh': (512, BF16), 'bk': (128, F32), 'bkb': (128, BF16),
               'bv': (128, F32), 'iq': (256, BF16), 'ik': (IDX_DIM, F32), 'ikb': (IDX_DIM, BF16),
               'cq': (512, BF16), 'ck': (512, F32), 'cv': (512, F32), 'ckb': (512, BF16), 'cvb': (512, BF16),
               'av': (A_HALF, F32)}
    names = _SAMPLE_OUTS if sample else _PROMPT_OUTS
    out_specs, out_shape = [], []
    for name in names:
        if name == 'vt':
            out_specs.append(pl.BlockSpec((tm // DSA_KB, 128, DSA_KB), lambda i: (i, 0, 0)))
            out_shape.append(jax.ShapeDtypeStruct((n // DSA_KB, 128, DSA_KB), BF16))
        elif name == 'iwt':
            out_specs.append(pl.BlockSpec((IDX_HEADS, tm), lambda i: (0, i)))
            out_shape.append(jax.ShapeDtypeStruct((IDX_HEADS, n), F32))
        elif depth > 1 and name in _LEAF_NAMES:
            c, dt = row_out[name]
            out_specs.append(pl.BlockSpec((None, tm, c), lambda i: (layer, i, 0)))
            out_shape.append(jax.ShapeDtypeStruct((depth, n, c), dt))
        else:
            c, dt = row_out[name]
            out_specs.append(tile(c))
            out_shape.append(jax.ShapeDtypeStruct((n, c), dt))
    ma = lw['ma_sample'] if sample else lw['ma_prompt']
    abias = lw['abias_sample'] if sample else lw['abias_prompt']
    consts = [lw['norm_mix'], lw['w1'], lw['w_vt'], lw['w_iwt'], lw['a_vnorm'], lw['qn'], lw['kn'],
              lw['g512'], lw['g128'], ma, abias]
    shared = [prev[name] for name in _LEAF_NAMES] if prev is not None else []
    aliases = {1 + len(consts) + j: names.index(name) for j, name in enumerate(_LEAF_NAMES)} if shared else {}
    outs = pl.pallas_call(
        functools.partial(_proj_kernel, tm=tm, ca=ca, sample=sample, names=names, n_alias=len(shared)),
        grid=grid,
        in_specs=[tile(D_MODEL)] + [_const_spec(c.shape) for c in consts]
        + [pl.BlockSpec(memory_space=pl.ANY)] * len(shared),
        out_specs=out_specs,
        out_shape=out_shape,
        input_output_aliases=aliases,
        compiler_params=pltpu.CompilerParams(dimension_semantics=("arbitrary",), vmem_limit_bytes=VMEM_LIMIT),
        name="proj_sample" if sample else "proj_prompt",
    )(x, *consts, *shared)
    return dict(zip(names, outs))


def _dsa_kernel(qg_ref, iq_ref, iwt_ref, ik_ref, k_ref, vt_ref, tri_ref, o_ref, sc_ref,
                *, qb, kb_size, n_valid, topk):
    q0 = pl.program_id(1) * qb
    nkb = (q0 + qb + kb_size - 1) // kb_size

    cb_size = kb_size
    ncb = nkb
    cpos0 = lax.broadcasted_iota(I32, (cb_size, qb), 0)
    qpos_row = q0 + lax.broadcasted_iota(I32, (1, qb), 1)
    qchunk_row = qpos_row // CHUNK
    n_adm = jnp.minimum((qchunk_row + 1) * CHUNK, n_valid)

    def fold(x, op):
        return op(op(x.reshape(kb_size // 8, 8, qb), axis=0), axis=0, keepdims=True)

    iq = iq_ref[...]
    iq_rows = jnp.concatenate([iq[:, h * IDX_DIM:(h + 1) * IDX_DIM] for h in range(IDX_HEADS)], axis=0)
    iwt = iwt_ref[...]

    def score_body(cb, carry):
        mx, mn = carry
        off = pl.multiple_of(cb * cb_size, cb_size)
        d = _dot_t(ik_ref[pl.ds(off, cb_size), :], iq_rows)
        acc = jnp.zeros((cb_size, qb), F32)
        for h in range(IDX_HEADS):
            acc = acc + jnp.maximum(d[:, h * qb:(h + 1) * qb], 0.0) * iwt[h:h + 1, :]
        kpos = off + cpos0
        adm = (kpos // CHUNK <= qchunk_row) & (kpos < n_valid)
        sc = jnp.where(adm, acc, -jnp.inf)
        sc_ref[pl.ds(off, cb_size), :] = sc
        mx = jnp.maximum(mx, jnp.max(sc.reshape(cb_size // 8, 8, qb), axis=0))
        mn = jnp.minimum(mn, jnp.min(jnp.where(adm, acc, jnp.inf).reshape(cb_size // 8, 8, qb), axis=0))
        return mx, mn

    mx, mn = lax.fori_loop(0, ncb, score_body,
                           (jnp.full((8, qb), -jnp.inf, F32), jnp.full((8, qb), jnp.inf, F32)))
    smax = jnp.max(mx, axis=0, keepdims=True)
    smin = jnp.min(mn, axis=0, keepdims=True)

    acc_rows = 32

    def count(pred):
        def body(cb, cnt):
            off = pl.multiple_of(cb * cb_size, cb_size)
            ind = jnp.where(pred(sc_ref[pl.ds(off, cb_size), :], off), 1.0, 0.0)
            return cnt + jnp.sum(ind.reshape(cb_size // acc_rows, acc_rows, qb), axis=0)
        cnt = lax.fori_loop(0, ncb, body, jnp.zeros((acc_rows, qb), F32))
        return jnp.sum(jnp.sum(cnt.reshape(acc_rows // 8, 8, qb), axis=0), axis=0, keepdims=True)

    def reduce_min(fn):
        def body(cb, acc):
            off = pl.multiple_of(cb * cb_size, cb_size)
            v = fn(sc_ref[pl.ds(off, cb_size), :])
            return jnp.minimum(acc, jnp.min(v.reshape(cb_size // 8, 8, qb), axis=0))
        acc = lax.fori_loop(0, ncb, body, jnp.full((8, qb), jnp.inf, F32))
        return jnp.min(acc, axis=0, keepdims=True)

    k_f = float(topk)
    few = n_adm <= topk
    steps_per_trip = 4

    c_ge0 = count(lambda blk, off: blk >= 0.0)
    c_gt0 = count(lambda blk, off: blk > 0.0)
    zero_tie = (c_ge0 >= k_f) & (c_gt0 < k_f)
    above = c_gt0 >= k_f

    def sel_cond(state):
        t, _, _, _, pending = state
        return (t < SELECT_MAX_STEPS) & (pending > 0.0)

    def sel_body(state):
        t, lo, hi, c_lo, _ = state
        for _ in range(steps_per_trip):
            mid = lo + (hi - lo) * 0.5
            c = count(lambda blk, off: blk >= mid)
            take = c >= k_f
            lo = jnp.where(take, mid, lo)
            hi = jnp.where(take, hi, mid)
            c_lo = jnp.where(take, c, c_lo)
        pending = jnp.max(jnp.where(few | zero_tie | (c_lo == k_f), 0.0, 1.0))
        return t + steps_per_trip, lo, hi, c_lo, pending

    _, lo, _, c_lo, pending = lax.while_loop(
        sel_cond, sel_body,
        (jnp.int32(0), jnp.where(above, 0.0, smin), jnp.where(above, smax, 0.0),
         jnp.where(above, c_ge0, n_adm.astype(F32)), jnp.float32(1.0)))

    def resolve(_):
        v = reduce_min(lambda blk: jnp.where(blk >= lo, blk, jnp.inf))
        return v, k_f - count(lambda blk, off: blk > v)

    v_cap, need_cap = lax.cond(pending > 0.0, resolve, lambda _: (lo, jnp.full((1, qb), jnp.inf, F32)), 0)
    unsettled = jnp.logical_not(few | zero_tie | (c_lo == k_f))
    thr = jnp.where(few, -jnp.inf, jnp.where(zero_tie, 0.0, jnp.where(unsettled, v_cap, lo)))
    need = jnp.where(few, jnp.inf, jnp.where(zero_tie, k_f - c_gt0, jnp.where(unsettled, need_cap, jnp.inf)))
    any_tie = jnp.max(jnp.where((zero_tie | unsettled) & jnp.logical_not(few), 1.0, 0.0))

    @pl.when(any_tie > 0.0)
    def _():
        tri = tri_ref[...]

        def body(cb, seen):
            off = pl.multiple_of(cb * cb_size, cb_size)
            sc = sc_ref[pl.ds(off, cb_size), :]
            eq = sc == thr
            rank = _dot(tri, jnp.where(eq, 1.0, 0.0).astype(BF16)) + seen
            sc_ref[pl.ds(off, cb_size), :] = jnp.where(eq & (rank > need), -jnp.inf, sc)
            return rank[cb_size - 1:cb_size, :]

        lax.fori_loop(0, ncb, body, jnp.zeros((1, qb), F32))

    qg = qg_ref[...]
    q_pairs = [jnp.concatenate([qg[:, (2 * p) * LANES:(2 * p + 1) * LANES],
                                qg[:, (2 * p + 1) * LANES:(2 * p + 2) * LANES]], axis=0)
               for p in range(B_HEADS // 2)]
    hpg = B_HEADS // B_KV_HEADS

    def att_body(kb, carry):
        ms, ls, accs = carry
        off = pl.multiple_of(kb * kb_size, kb_size)
        sc_blk = sc_ref[pl.ds(off, kb_size), :]
        sel = (sc_blk >= thr) & (sc_blk > -jnp.inf)
        bias = jnp.where(sel, 0.0, -jnp.inf)
        k_blk = k_ref[pl.ds(off, kb_size), :]
        vt_blk = vt_ref[kb]
        new_m, new_l, new_a = list(ms), list(ls), list(accs)
        sts = [_dot_t(k_blk, q_pairs[p]) for p in range(B_HEADS // 2)]
        pts, alphas = [], []
        for h in range(B_HEADS):
            s = sts[h // 2][:, (h % 2) * qb:(h % 2 + 1) * qb] + bias
            m_new = jnp.maximum(ms[h], fold(s, jnp.max))
            m_safe = jnp.where(m_new == -jnp.inf, 0.0, m_new)
            pt = jnp.exp2(s - m_safe)
            alpha = jnp.exp2(ms[h] - m_safe)
            new_m[h] = m_new
            new_l[h] = alpha * ls[h] + fold(pt, jnp.sum)
            pts.append(pt.astype(BF16))
            alphas.append(alpha)
        for h in range(B_HEADS):
            g = h // hpg
            new_a[h] = alphas[h] * accs[h] + _dot(vt_blk[g * B_HEAD_DIM:(g + 1) * B_HEAD_DIM, :], pts[h])
        return tuple(new_m), tuple(new_l), tuple(new_a)

    init = (tuple(jnp.full((1, qb), -jnp.inf, F32) for _ in range(B_HEADS)),
            tuple(jnp.zeros((1, qb), F32) for _ in range(B_HEADS)),
            tuple(jnp.zeros((B_HEAD_DIM, qb), F32) for _ in range(B_HEADS)))
    ms, ls, accs = lax.fori_loop(0, nkb, att_body, init)
    out_t = jnp.concatenate([accs[h] / ls[h] for h in range(B_HEADS)], axis=0)
    o_ref[...] = out_t.T.astype(BF16)


def _dsa_call(qg, iq, iwt, ik, k, vt, tri, *, qb, kb_size):
    b, sq, _ = qg.shape
    lp = ik.shape[1]
    nq = sq // qb
    nkb = lp // kb_size
    topk = min(B_TOPK_MAX, lp // 4)

    def qspec(c):
        return pl.BlockSpec((None, qb, c), lambda bi, i: (bi, i, 0))

    def kspec(c):
        return pl.BlockSpec((None, lp, c), lambda bi, i: (bi, 0, 0))

    return pl.pallas_call(
        functools.partial(_dsa_kernel, qb=qb, kb_size=kb_size, n_valid=lp, topk=topk),
        grid=(b, nq),
        in_specs=[qspec(B_HEADS * LANES), qspec(256),
                  pl.BlockSpec((None, IDX_HEADS, qb), lambda bi, i: (bi * nq + i, 0, 0)),
                  kspec(IDX_DIM), kspec(128),
                  pl.BlockSpec((nkb, 128, kb_size), lambda bi, i: (bi, 0, 0)), _const_spec(tri.shape)],
        out_specs=qspec(512),
        out_shape=jax.ShapeDtypeStruct((b, sq, 512), BF16),
        scratch_shapes=[pltpu.VMEM((lp, qb), F32)],
        compiler_params=pltpu.CompilerParams(dimension_semantics=("arbitrary", "arbitrary"),
                                             vmem_limit_bytes=VMEM_LIMIT),
        name="dsa_prompt",
    )(qg, iq, iwt, ik, k, vt, tri)


def _dsa_sample_kernel(qn_ref, iq_ref, iwc_ref, ikn_ref, ktn_ref, vtn_ref, ikp_ref, ktp_ref, vtp_ref, tri_ref, o_ref,
                       sc_ref, *, t, n_past, cb, topk):
    nb = n_past // cb
    hpg = B_HEADS // B_KV_HEADS
    lanes = lax.broadcasted_iota(I32, (t, cb), 1)
    iq = iq_ref[...]
    iq_rows = jnp.concatenate([iq[:, h * IDX_DIM:(h + 1) * IDX_DIM] for h in range(IDX_HEADS)], axis=0)
    iwc = iwc_ref[...]

    def scores(ik_blk):
        w = jnp.maximum(_dot_t(iq_rows, ik_blk), 0.0) * iwc
        s = w[0:t]
        for h in range(1, IDX_HEADS):
            s = s + w[h * t:(h + 1) * t]
        return s

    def fold_lanes(x, op):
        part = x[:, 0:LANES]
        for c in range(1, cb // LANES):
            part = op(part, x[:, c * LANES:(c + 1) * LANES])
        return part

    def past_scores(kb, carry):
        mx, mn = carry
        off = pl.multiple_of(kb * cb, cb)
        sc = scores(ikp_ref[pl.ds(off, cb), :].astype(BF16))
        sc_ref[kb] = sc
        return jnp.maximum(mx, fold_lanes(sc, jnp.maximum)), jnp.minimum(mn, fold_lanes(sc, jnp.minimum))

    mx, mn = lax.fori_loop(0, nb, past_scores,
                           (jnp.full((t, LANES), -jnp.inf, F32), jnp.full((t, LANES), jnp.inf, F32)))
    sc_new = scores(ikn_ref[...])
    valid = lanes < t
    sc_ref[nb] = jnp.where(valid, sc_new, -jnp.inf)
    mx = jnp.maximum(mx, fold_lanes(jnp.where(valid, sc_new, -jnp.inf), jnp.maximum))
    mn = jnp.minimum(mn, fold_lanes(jnp.where(valid, sc_new, jnp.inf), jnp.minimum))
    smax = jnp.max(mx, axis=1, keepdims=True)
    smin = jnp.min(mn, axis=1, keepdims=True)

    def count(pred):
        def body(kb, cnt):
            ind = jnp.where(pred(sc_ref[kb], kb * cb + lanes), 1.0, 0.0)
            return cnt + fold_lanes(ind, jnp.add)
        cnt = lax.fori_loop(0, nb + 1, body, jnp.zeros((t, LANES), F32))
        return jnp.sum(cnt, axis=1, keepdims=True)

    k_f = float(topk)
    steps_per_trip = 4

    c_ge0 = count(lambda blk, pos: blk >= 0.0)
    c_gt0 = count(lambda blk, pos: blk > 0.0)
    zero_tie = (c_ge0 >= k_f) & (c_gt0 < k_f)
    above = c_gt0 >= k_f

    def sel_cond(state):
        tt, _, _, _, pending = state
        return (tt < SELECT_MAX_STEPS) & (pending > 0.0)

    def sel_body(state):
        tt, lo, hi, c_lo, _ = state
        for _ in range(steps_per_trip):
            mid = lo + (hi - lo) * 0.5
            c = count(lambda blk, pos: blk >= mid)
            take = c >= k_f
            lo = jnp.where(take, mid, lo)
            hi = jnp.where(take, hi, mid)
            c_lo = jnp.where(take, c, c_lo)
        pending = jnp.max(jnp.where(zero_tie | (c_lo == k_f), 0.0, 1.0))
        return tt + steps_per_trip, lo, hi, c_lo, pending

    _, lo, _, c_lo, pending = lax.while_loop(
        sel_cond, sel_body,
        (jnp.int32(0), jnp.where(above, 0.0, smin), jnp.where(above, smax, 0.0),
         jnp.where(above, c_ge0, float(n_past + t)), jnp.float32(1.0)))

    def resolve(_):
        def body(kb, acc):
            blk = sc_ref[kb]
            return jnp.minimum(acc, fold_lanes(jnp.where(blk >= lo, blk, jnp.inf), jnp.minimum))
        acc = lax.fori_loop(0, nb + 1, body, jnp.full((t, LANES), jnp.inf, F32))
        v = jnp.min(acc, axis=1, keepdims=True)
        return v, k_f - count(lambda blk, pos: blk > v)

    v_cap, need_cap = lax.cond(pending > 0.0, resolve, lambda _: (lo, jnp.full((t, 1), jnp.inf, F32)), 0)
    unsettled = jnp.logical_not(zero_tie | (c_lo == k_f))
    thr = jnp.where(zero_tie, 0.0, jnp.where(unsettled, v_cap, lo))
    need = jnp.where(zero_tie, k_f - c_gt0, jnp.where(unsettled, need_cap, jnp.inf))
    any_tie = jnp.max(jnp.where(zero_tie | unsettled, 1.0, 0.0))

    @pl.when(any_tie > 0.0)
    def _():
        tri = tri_ref[...]

        def body(kb, seen):
            sc = sc_ref[kb]
            eq = sc == thr
            rank = _dot(jnp.where(eq, 1.0, 0.0).astype(BF16), tri) + seen
            sc_ref[kb] = jnp.where(eq & (rank > need), -jnp.inf, sc)
            return rank[:, cb - 1:cb]

        lax.fori_loop(0, nb + 1, body, jnp.zeros((t, 1), F32))

    qn = qn_ref[...]
    q_groups = [jnp.concatenate([qn[:, h * B_HEAD_DIM:(h + 1) * B_HEAD_DIM]
                                 for h in range(g * hpg, (g + 1) * hpg)], axis=0)
                for g in range(B_KV_HEADS)]

    def attend(carry, kb, kt, vt):
        ms, ls, accs = carry
        sc_blk = sc_ref[kb]
        sel = (sc_blk >= thr) & (sc_blk > -jnp.inf)
        bias = jnp.where(sel, 0.0, -jnp.inf)
        bias_rows = jnp.concatenate([bias] * hpg, axis=0)
        new_m, new_l, new_a = [], [], []
        for g in range(B_KV_HEADS):
            s = _dot(q_groups[g], kt(g)) + bias_rows
            m_new = jnp.maximum(ms[g], jnp.max(s, axis=1, keepdims=True))
            m_safe = jnp.where(m_new == -jnp.inf, 0.0, m_new)
            p = jnp.exp2(s - m_safe)
            alpha = jnp.exp2(ms[g] - m_safe)
            new_m.append(m_new)
            new_l.append(alpha * ls[g] + jnp.sum(p, axis=1, keepdims=True))
            new_a.append(alpha * accs[g] + _dot_t(p.astype(BF16), vt(g)))
        return tuple(new_m), tuple(new_l), tuple(new_a)

    def past_attend(kb, carry):
        off = pl.multiple_of(kb * cb, cb)
        return attend(carry, kb, lambda g: ktp_ref[g, :, pl.ds(off, cb)].astype(BF16),
                      lambda g: vtp_ref[g, :, pl.ds(off, cb)].astype(BF16))

    init = (tuple(jnp.full((hpg * t, 1), -jnp.inf, F32) for _ in range(B_KV_HEADS)),
            tuple(jnp.zeros((hpg * t, 1), F32) for _ in range(B_KV_HEADS)),
            tuple(jnp.zeros((hpg * t, B_HEAD_DIM), F32) for _ in range(B_KV_HEADS)))
    carry = lax.fori_loop(0, nb, past_attend, init)
    _, ls, accs = attend(carry, nb, lambda g: ktn_ref[g], lambda g: vtn_ref[g])
    for g in range(B_KV_HEADS):
        out = accs[g] / ls[g]
        for hh in range(hpg):
            h = g * hpg + hh
            o_ref[:, h * B_HEAD_DIM:(h + 1) * B_HEAD_DIM] = out[hh * t:(hh + 1) * t].astype(BF16)


def _dsa_sample_call(qn, iq, iwc, ikn, ktn, vtn, ikp, ktp, vtp, tri, layer, *, cb):
    b, t, _ = qn.shape
    n_past = ikp.shape[2]
    topk = min(B_TOPK_MAX, (n_past + t) // 4)
    assert n_past % cb == 0 and n_past + t > topk

    def bspec(*shape):
        zeros = (0,) * len(shape)
        return pl.BlockSpec((None,) + shape, lambda bi: (bi,) + zeros)

    def cspec(*shape):
        zeros = (0,) * len(shape)
        return pl.BlockSpec((None, None) + shape, lambda bi: (layer, bi) + zeros)

    return pl.pallas_call(
        functools.partial(_dsa_sample_kernel, t=t, n_past=n_past, cb=cb, topk=topk),
        grid=(b,),
        in_specs=[bspec(t, 512), bspec(t, 256), bspec(IDX_HEADS * t, 1),
                  bspec(cb, IDX_DIM), bspec(B_KV_HEADS, B_HEAD_DIM, cb), bspec(B_KV_HEADS, B_HEAD_DIM, cb),
                  cspec(n_past, IDX_DIM), cspec(B_KV_HEADS, B_HEAD_DIM, n_past),
                  cspec(B_KV_HEADS, B_HEAD_DIM, n_past), _const_spec(tri.shape)],
        out_specs=bspec(t, 512),
        out_shape=jax.ShapeDtypeStruct((b, t, 512), BF16),
        scratch_shapes=[pltpu.VMEM((n_past // cb + 1, t, cb), F32)],
        compiler_params=pltpu.CompilerParams(dimension_semantics=("arbitrary",), vmem_limit_bytes=VMEM_LIMIT),
        name="dsa_sample",
    )(qn, iq, iwc, ikn, ktn, vtn, ikp, ktp, vtp, tri)


SB_WIDE = 256
SB_GROUP = 4
SB_SAMPLE_WINDOW = 1024


def _sb_tri2(n):
    r = jnp.arange(2 * n)[:, None] % n
    c = jnp.arange(n)[None, :]
    return jnp.where(r > c, 1.0, 0.0).astype(BF16)


def _sb_stage(z_fn, pv, tri2_ref, carries, accs, vis):
    tri2 = tri2_ref[...]
    new_c, new_a = list(carries), list(accs)
    for h0 in range(0, C_HEADS, SB_GROUP):
        heads = range(h0, h0 + SB_GROUP)
        zs = [z_fn(h) for h in heads]
        sps = [jnp.maximum(z, 0.0) + jnp.log(1.0 + jnp.exp(-jnp.abs(z))) for z in zs]
        stays = [-sp if vis is None else jnp.where(vis, -sp, 0.0) for sp in sps]
        afters = []
        for ls in stays:
            hi = ls.astype(BF16)
            lo = (ls - hi.astype(F32)).astype(BF16)
            afters.append(_dot(jnp.concatenate([hi, lo], axis=1), tri2))
        for i, h in enumerate(heads):
            w = jnp.exp((zs[i] - sps[i]) + (afters[i] + carries[h]))
            if vis is not None:
                w = jnp.where(vis, w, 0.0)
            new_a[h] = accs[h] + pv(h, w.astype(BF16))
            new_c[h] = carries[h] + jnp.sum(stays[i], axis=1, keepdims=True)
    return tuple(new_c), tuple(new_a)


def _sb_max_carry(carries):
    m = carries[0]
    for c in carries[1:]:
        m = jnp.maximum(m, c)
    return jnp.max(m)


def _sb_init(qb, d):
    return (tuple(jnp.zeros((qb, 1), F32) for _ in range(C_HEADS)),
            tuple(jnp.zeros((qb, d), F32) for _ in range(C_HEADS)))


def _sb_prompt_kernel(q_ref, k_ref, v_ref, tri_ref, o_ref, *, qb):
    i = pl.program_id(1)
    q = q_ref[...]
    lane = lax.broadcasted_iota(I32, (qb, LANES), 1)
    q_heads = []
    for h in range(C_HEADS):
        pair = q[:, (h // 2) * LANES:(h // 2 + 1) * LANES]
        q_heads.append(jnp.where(lane // C_HEAD_DIM == h % 2, pair, jnp.zeros_like(pair)))

    def block(off, carries, accs, vis):
        k_blk = k_ref[pl.ds(off, qb), :]
        v_blk = v_ref[pl.ds(off, qb), :]
        z_fn = lambda h: _dot_t(q_heads[h], k_blk[:, (h // 2) * LANES:(h // 2 + 1) * LANES])
        pv = lambda h, w: _dot(w, v_blk[:, (h // 2) * LANES:(h // 2 + 1) * LANES])
        return _sb_stage(z_fn, pv, tri_ref, carries, accs, vis)

    rows = lax.broadcasted_iota(I32, (qb, qb), 0)
    cols = lax.broadcasted_iota(I32, (qb, qb), 1)
    carries, accs = _sb_init(qb, LANES)
    carries, accs = block(pl.multiple_of(i * qb, qb), carries, accs, cols < rows)

    def cond(state):
        j, mx, _, _ = state
        return (j < i) & (mx > SB_SKIP)

    def body(state):
        j, _, carries, accs = state
        off = pl.multiple_of((i - j - 1) * qb, qb)
        carries, accs = block(off, carries, accs, None)
        return j + 1, _sb_max_carry(carries), carries, accs

    _, _, _, accs = lax.while_loop(cond, body, (jnp.int32(0), _sb_max_carry(carries), carries, accs))
    for p in range(C_HEADS // 2):
        o_ref[:, p * LANES:(p + 1) * LANES] = jnp.where(lane < C_HEAD_DIM, accs[2 * p], accs[2 * p + 1]).astype(BF16)


def _sb_prompt_call(cq, ckb, cvb, lw, *, qb):
    b, s, _ = cq.shape
    qspec = pl.BlockSpec((None, qb, 512), lambda bi, i: (bi, i, 0))
    kspec = pl.BlockSpec((None, s, 512), lambda bi, i: (bi, 0, 0))
    return pl.pallas_call(
        functools.partial(_sb_prompt_kernel, qb=qb),
        grid=(b, s // qb),
        in_specs=[qspec, kspec, kspec, _const_spec(lw['tri2'].shape)],
        out_specs=qspec,
        out_shape=jax.ShapeDtypeStruct((b, s, 512), BF16),
        compiler_params=pltpu.CompilerParams(dimension_semantics=("arbitrary", "arbitrary"),
                                             vmem_limit_bytes=VMEM_LIMIT),
        name="stick_prompt",
    )(cq, ckb, cvb, lw['tri2'])


def _sb_sample_kernel(q_ref, knt_ref, vnt_ref, kpt_ref, vpt_ref, tri1_ref, tri2_ref, o_ref, done_ref, *, qb, n_hist):
    q = q_ref[...]
    q_heads = [q[:, h * C_HEAD_DIM:(h + 1) * C_HEAD_DIM] for h in range(C_HEADS)]
    lane = lax.broadcasted_iota(I32, (qb, LANES), 1)
    rows = lax.broadcasted_iota(I32, (qb, LANES), 0)

    def block(kt, vt, tri_ref, carries, accs, vis):
        z_fn = lambda h: _dot(q_heads[h], kt(h))
        pv = lambda h, w: _dot_t(w, vt(h))
        return _sb_stage(z_fn, pv, tri_ref, carries, accs, vis)

    carries, accs = _sb_init(qb, C_HEAD_DIM)
    carries, accs = block(lambda h: knt_ref[h], lambda h: vnt_ref[h], tri1_ref, carries, accs, lane < rows)

    def cond(state):
        j, mx, _, _ = state
        return (j < n_hist // SB_WIDE) & (mx > SB_SKIP)

    def body(state):
        j, _, carries, accs = state
        off = pl.multiple_of(n_hist - (j + 1) * SB_WIDE, SB_WIDE)
        carries, accs = block(lambda h: kpt_ref[h, :, pl.ds(off, SB_WIDE)].astype(BF16),
                              lambda h: vpt_ref[h, :, pl.ds(off, SB_WIDE)].astype(BF16),
                              tri2_ref, carries, accs, None)
        return j + 1, _sb_max_carry(carries), carries, accs

    _, mx, _, accs = lax.while_loop(cond, body, (jnp.int32(0), _sb_max_carry(carries), carries, accs))
    for h in range(C_HEADS):
        o_ref[:, h * C_HEAD_DIM:(h + 1) * C_HEAD_DIM] = accs[h].astype(BF16)
    done_ref[...] = jnp.full(done_ref.shape, jnp.where(mx > SB_SKIP, 0.0, 1.0), F32)


def _sb_sample_call(cq, knt, vnt, kpt, vpt, lw, layer, n_hist):
    b, t, _ = cq.shape
    n_past = kpt.shape[-1]
    qspec = pl.BlockSpec((None, t, 512), lambda bi: (bi, 0, 0))
    nspec = pl.BlockSpec((None, C_HEADS, C_HEAD_DIM, LANES), lambda bi: (bi, 0, 0, 0))
    pspec = pl.BlockSpec((None, None, C_HEADS, C_HEAD_DIM, n_hist), lambda bi: (layer, bi, 0, 0, n_past // n_hist - 1))
    return pl.pallas_call(
        functools.partial(_sb_sample_kernel, qb=t, n_hist=n_hist),
        grid=(b,),
        in_specs=[qspec, nspec, nspec, pspec, pspec, _const_spec(lw['tri1'].shape), _const_spec(lw['tri2'].shape)],
        out_specs=[qspec, pl.BlockSpec((None, 8, LANES), lambda bi: (bi, 0, 0))],
        out_shape=[jax.ShapeDtypeStruct((b, t, 512), BF16), jax.ShapeDtypeStruct((b, 8, LANES), F32)],
        compiler_params=pltpu.CompilerParams(dimension_semantics=("arbitrary",), vmem_limit_bytes=VMEM_LIMIT),
        name="stick_sample",
    )(cq, knt, vnt, kpt, vpt, lw['tri1'], lw['tri2'])


def _sb_sample(cq, knt, vnt, kpt, vpt, lw, layer):
    n_past = kpt.shape[-1]
    n_hist = min(SB_SAMPLE_WINDOW, n_past)
    oc, done = _sb_sample_call(cq, knt, vnt, kpt, vpt, lw, layer, n_hist)
    if n_hist == n_past:
        return oc
    return lax.cond(jnp.all(done > 0.0), lambda: oc,
                    lambda: _sb_sample_call(cq, knt, vnt, kpt, vpt, lw, layer, n_past)[0])


def _merge_kernel(x_ref, nmix_ref, wg_ref, gb_ref, oa_ref, ob_ref, oc_ref, wa_ref, wb_ref, wc_ref, wo_ref, o_ref):
    x = x_ref[...]
    hb = _rms(x, nmix_ref[...]).astype(BF16)
    merged = None
    for idx, (br_ref, w_ref) in enumerate(((oa_ref, wa_ref), (ob_ref, wb_ref), (oc_ref, wc_ref))):
        cs = slice(idx * D_MODEL, (idx + 1) * D_MODEL)
        gate = jax.nn.sigmoid(_dot(hb, wg_ref[:, cs]) + gb_ref[:, cs])
        term = gate * _dot(br_ref[...], w_ref[...])
        merged = term if merged is None else merged + term
    o_ref[...] = x + _dot(merged.astype(BF16), wo_ref[...])


def _merge_call(x, oa, ob, oc, lw, *, tm, name):
    n = x.shape[0]

    def tile(c):
        return pl.BlockSpec((tm, c), lambda i: (i, 0))

    consts_a = [lw['norm_mix'], lw['w_gate'], lw['gate_bias']]
    consts_b = [lw['w_br_a'], lw['w_br_b'], lw['w_br_c'], lw['w_out']]
    return pl.pallas_call(
        _merge_kernel,
        grid=(n // tm,),
        in_specs=([tile(D_MODEL)] + [_const_spec(c.shape) for c in consts_a] + [tile(512)] * 3
                  + [_const_spec(c.shape) for c in consts_b]),
        out_specs=tile(D_MODEL),
        out_shape=jax.ShapeDtypeStruct((n, D_MODEL), F32),
        compiler_params=pltpu.CompilerParams(dimension_semantics=("arbitrary",), vmem_limit_bytes=VMEM_LIMIT),
        name=name,
    )(x, *consts_a, oa, ob, oc, *consts_b)


def _ffn_kernel(x_ref, p_ref, nffn_ref, wfi_ref, wfo_ref, nple_ref, wpg_ref, wpp_ref, o_ref):
    x = x_ref[...]
    hf = _rms(x, nffn_ref[...]).astype(BF16)
    g = _dot(hf, wfi_ref[:, 0:D_FF])
    up = _dot(hf, wfi_ref[:, D_FF:2 * D_FF])
    x = x + _dot((jax.nn.silu(g) * up).astype(BF16), wfo_ref[...])
    hp = _rms(x, nple_ref[...]).astype(BF16)
    gate = jax.nn.sigmoid(_dot(hp, wpg_ref[...]))
    o_ref[...] = x + gate * _dot(p_ref[...].astype(BF16), wpp_ref[...])


def _ffn_call(x, p, layer, lw, *, tm, name):
    n = x.shape[0]
    consts = [lw['norm_ffn'], lw['w_ffn_in'], lw['w_ffn_out'], lw['norm_ple'], lw['w_ple_gate'], lw['w_ple_proj']]
    return pl.pallas_call(
        _ffn_kernel,
        grid=(n // tm,),
        in_specs=[pl.BlockSpec((tm, D_MODEL), lambda i: (i, 0)),
                  pl.BlockSpec((None, tm, PLE_DIM), lambda i: (layer, i, 0))]
        + [_const_spec(c.shape) for c in consts],
        out_specs=pl.BlockSpec((tm, D_MODEL), lambda i: (i, 0)),
        out_shape=jax.ShapeDtypeStruct((n, D_MODEL), F32),
        compiler_params=pltpu.CompilerParams(dimension_semantics=("arbitrary",), vmem_limit_bytes=VMEM_LIMIT),
        name=name,
    )(x, p, *consts)


def _block_diag_mean(n):
    r = jnp.arange(n)
    return jnp.where((r[:, None] // B_HEAD_DIM) == (r[None, :] // B_HEAD_DIM), 1.0 / B_HEAD_DIM, 0.0).astype(BF16)


def _rank_tri(n):
    r = jnp.arange(n)
    return jnp.where(r[None, :] <= r[:, None], 1.0, 0.0).astype(BF16)


def _prep_layer(l, t_sample, w):
    w_in = w['w_in'][l]
    small = jnp.pad(w_in[:, 2048:2088], ((0, 0), (0, LANES - IDX_DIM - IDX_HEADS)))
    w1 = jnp.concatenate([w_in[:, :2048], small, w_in[:, 2088:3624]], axis=1).astype(BF16)
    a_ws = w['a_ws'][l]
    a_bias_full = jnp.repeat(w['a_bias'][l].T, A_GROUP_DIM, axis=1)
    row = lambda a: a.reshape(1, -1)
    return {
        'norm_mix': row(w['norm_mix'][l]), 'w1': w1,
        'w_vt': w_in[:, 1664:1792].T.astype(BF16), 'w_iwt': w_in[:, 2080:2088].T.astype(BF16),
        'a_vnorm': row(w['a_vnorm'][l]),
        'qn': row(jnp.tile(w['b_qnorm'][l], B_HEADS)), 'kn': row(jnp.tile(w['b_knorm'][l], B_KV_HEADS)),
        'g512': _block_diag_mean(512), 'g128': _block_diag_mean(128),
        'tri1': _sb_tri2(LANES), 'tri2': _sb_tri2(SB_WIDE),
        'tri_rank': _rank_tri(DSA_KB), 'tri_rank_t': _rank_tri(2 * DSA_KB).T,
        'ma_prompt': a_ws, 'abias_prompt': a_bias_full,
        'ma_sample': jnp.tile(a_ws[:, :t_sample, :t_sample], (1, 16, 16)),
        'abias_sample': jnp.tile(a_bias_full[:t_sample], (16, 1)),
        'w_gate': w_in[:, 3624:].astype(BF16), 'gate_bias': row(w['gate_bias'][l]),
        'w_br_a': w['w_br_a'][l].astype(BF16), 'w_br_b': w['w_br_b'][l].astype(BF16),
        'w_br_c': w['w_br_c'][l].astype(BF16), 'w_out': w['w_out'][l].astype(BF16),
        'norm_ffn': row(w['norm_ffn'][l]), 'w_ffn_in': w['w_ffn_in'][l].astype(BF16),
        'w_ffn_out': w['w_ffn_out'][l].astype(BF16), 'norm_ple': row(w['norm_ple'][l]),
        'w_ple_gate': w['w_ple_gate'][l].astype(BF16), 'w_ple_proj': w['w_ple_proj'][l].astype(BF16),
    }


def _iwt_blocks(iwt, qb):
    return iwt.reshape(IDX_HEADS, -1, qb).transpose(1, 0, 2)


def _prompt_layer(x, p, lw, b, s, layer, depth, prev):
    pr = _proj_call(x, lw, tm=512, ca=A_CHUNK, sample=False, layer=layer, depth=depth, prev=prev)
    r3 = lambda a: a.reshape(b, s, a.shape[-1])
    ob = _dsa_call(r3(pr['qg']), r3(pr['iq']), _iwt_blocks(pr['iwt'], 256), r3(pr['ikb']), r3(pr['bkb']), pr['vt'],
                   lw['tri_rank'], qb=256, kb_size=DSA_KB)
    oc = _sb_prompt_call(r3(pr['cq']), r3(pr['ckb']), r3(pr['cvb']), lw, qb=SB_WIDE)
    x = _merge_call(x, pr['oa'], ob.reshape(b * s, 512), oc.reshape(b * s, 512), lw, tm=256, name="merge_prompt")
    x = _ffn_call(x, p, layer, lw, tm=256, name="ffn_prompt")
    return x, pr


def _sample_layer(x, p, lw, b, t, caches, layer):
    pik, pbk_t, pbv_t, pck_t, pcv_t = caches
    pr = _proj_call(x, lw, tm=b * t, ca=b * t, sample=True)
    r3 = lambda a: a.reshape(b, t, a.shape[-1])
    cb = 2 * DSA_KB

    def new_t(a, heads, width):
        at = a.reshape(b, t, heads, -1).transpose(0, 2, 3, 1).astype(BF16)
        return jnp.pad(at, ((0, 0), (0, 0), (0, 0), (0, width - t)))

    ikn = jnp.pad(r3(pr['ikb']), ((0, 0), (0, cb - t), (0, 0)))
    iwc = _iwt_blocks(pr['iwt'], t).reshape(b, IDX_HEADS * t, 1)
    ob = _dsa_sample_call(r3(pr['qh']), r3(pr['iq']), iwc, ikn, new_t(pr['bk'], B_KV_HEADS, cb),
                          new_t(pr['bv'], B_KV_HEADS, cb), pik, pbk_t, pbv_t, lw['tri_rank_t'], layer, cb=cb)
    oc = _sb_sample(r3(pr['cq']), new_t(pr['ck'], C_HEADS, LANES), new_t(pr['cv'], C_HEADS, LANES),
                    pck_t, pcv_t, lw, layer)
    x = _merge_call(x, pr['oa'], ob.reshape(b * t, 512), oc.reshape(b * t, 512), lw, tm=b * t, name="merge_sample")
    x = _ffn_call(x, p, layer, lw, tm=b * t, name="ffn_sample")
    return x, pr


def kernel(x_prompt, x_sample, cache_b_k, cache_b_v, cache_b_kidx, cache_c_k, cache_c_v, p_prompt, p_sample,
           norm_mix, w_in, gate_bias, a_vnorm, a_ws, a_bias, b_qnorm, b_knorm, w_br_a, w_br_b, w_br_c, w_out,
           norm_ffn, w_ffn_in, w_ffn_out, norm_ple, w_ple_gate, w_ple_proj):
    weights = dict(norm_mix=norm_mix, w_in=w_in, gate_bias=gate_bias, a_vnorm=a_vnorm, a_ws=a_ws, a_bias=a_bias,
                   b_qnorm=b_qnorm, b_knorm=b_knorm, w_br_a=w_br_a, w_br_b=w_br_b, w_br_c=w_br_c, w_out=w_out,
                   norm_ffn=norm_ffn, w_ffn_in=w_ffn_in, w_ffn_out=w_ffn_out, norm_ple=norm_ple,
                   w_ple_gate=w_ple_gate, w_ple_proj=w_ple_proj)
    depth = w_in.shape[0]
    bp, sp, _ = x_prompt.shape
    bs, ts, _ = x_sample.shape
    yp = x_prompt.reshape(bp * sp, D_MODEL)
    ys = x_sample.reshape(bs * ts, D_MODEL)
    caches = (cache_b_kidx,) + tuple(c.transpose(0, 1, 3, 4, 2) for c in (cache_b_k, cache_b_v, cache_c_k, cache_c_v))
    new_p, new_s = [], []
    for l in range(depth):
        lw = _prep_layer(l, ts, weights)
        yp, pr_p = _prompt_layer(yp, p_prompt.reshape(depth, bp * sp, PLE_DIM), lw, bp, sp, l, depth,
                                 new_p[-1] if new_p else None)
        ys, pr_s = _sample_layer(ys, p_sample.reshape(depth, bs * ts, PLE_DIM), lw, bs, ts, caches, l)
        new_p.append(pr_p)
        new_s.append(pr_s)

    def stack(states, name, lead, tail):
        return jnp.stack([s[name].reshape(*lead, *tail) for s in states])

    def shared(name, tail):
        return new_p[-1][name].reshape(depth, bp, sp, *tail)

    ls = (bs, ts)
    return (yp.reshape(bp, sp, D_MODEL), ys.reshape(bs, ts, D_MODEL),
            shared('bk', (B_KV_HEADS, B_HEAD_DIM)), shared('bv', (B_KV_HEADS, B_HEAD_DIM)), shared('ik', (IDX_DIM,)),
            shared('ck', (C_HEADS, C_HEAD_DIM)), shared('cv', (C_HEADS, C_HEAD_DIM)),
            stack(new_s, 'bk', ls, (B_KV_HEADS, B_HEAD_DIM)), stack(new_s, 'bv', ls, (B_KV_HEADS, B_HEAD_DIM)),
            stack(new_s, 'ik', ls, (IDX_DIM,)),
            stack(new_s, 'ck', ls, (C_HEADS, C_HEAD_DIM)), stack(new_s, 'cv', ls, (C_HEADS, C_HEAD_DIM)),
            stack(new_s, 'av', ls, (A_HALF,)))
```

```python
import functools

import jax
import jax.numpy as jnp
from jax import lax
from jax.experimental import pallas as pl
from jax.experimental.pallas import tpu as pltpu

F32 = jnp.float32
BF16 = jnp.bfloat16
I32 = jnp.int32

D_MODEL = 1024
CHUNK = 64
EPS = 1e-6
A_CHUNK = 128
A_GROUPS = 4
A_GROUP_DIM = 128
A_HALF = 512
B_HEADS = 8
B_KV_HEADS = 2
B_HEAD_DIM = 64
B_TOPK_MAX = 256
IDX_HEADS = 8
IDX_DIM = 32
C_HEADS = 8
C_HEAD_DIM = 64
N_BRANCH = 3
D_FF = 2816
PLE_DIM = 256

LANES = 128
VMEM_LIMIT = 56 * 1024 * 1024

_C_AU, _C_AV, _C_BQ, _C_BK, _C_BV, _C_IQ = 0, 512, 1024, 1536, 1664, 1792
_C_SM, _C_CQ, _C_CK, _C_CV, _C_END = 2048, 2176, 2688, 3200, 3712
DSA_KB = 512

SELECT_MAX_STEPS = 48
SB_SKIP = -104.0
LOG2E = 1.4426950408889634


def _const_spec(shape):
    zeros = (0,) * len(shape)
    return pl.BlockSpec(shape, lambda *_: zeros, pipeline_mode=pl.Buffered(1))


def _rms(x, gain):
    return x * lax.rsqrt(jnp.mean(x * x, axis=-1, keepdims=True) + EPS) * gain


def _dot(a, b):
    return jnp.dot(a, b, preferred_element_type=F32)


def _dot_t(a, b):
    return lax.dot_general(a, b, (((1,), (1,)), ((), ())), preferred_element_type=F32)


def _split_dot(x, m_ref):
    hi = x.astype(BF16)
    lo = (x - hi.astype(F32)).astype(BF16)
    m = m_ref[...]
    return _dot(hi, m) + _dot(lo, m)


def _group_rms(z, g_ref, gain):
    return z * lax.rsqrt(_split_dot(z * z, g_ref) + EPS) * gain


def _proj_kernel(x_ref, nmix_ref, w_ref, wvt_ref, wiwt_ref, avn_ref, qn_ref, kn_ref, g512_ref, g128_ref,
                 ma_ref, abias_ref, *rest, tm, ca, sample, names, n_alias):
    out = dict(zip(names, rest[n_alias:]))
    hb = _rms(x_ref[...], nmix_ref[...]).astype(BF16)

    def proj(c0, c1):
        return _dot(hb, w_ref[:, c0:c1])

    qn = _group_rms(proj(_C_BQ, _C_BK), g512_ref, qn_ref[...]) * (B_HEAD_DIM ** -0.5 * LOG2E)
    if 'qh' in out:
        out['qh'][...] = qn.astype(BF16)
    lane = lax.broadcasted_iota(I32, (tm, LANES), 1)
    for h in range(B_HEADS if 'qg' in out else 0):
        g = h // (B_HEADS // B_KV_HEADS)
        pair = qn[:, (h // 2) * LANES:(h // 2 + 1) * LANES]
        if h % 2 != g:
            pair = pltpu.roll(pair, B_HEAD_DIM, axis=1)
        out['qg'][:, h * LANES:(h + 1) * LANES] = jnp.where(lane // B_HEAD_DIM == g, pair, 0.0).astype(BF16)
    bk = _group_rms(proj(_C_BK, _C_BV), g128_ref, kn_ref[...])
    out['bk'][...] = bk
    if 'bkb' in out:
        out['bkb'][...] = bk.astype(BF16)
    out['bv'][...] = proj(_C_BV, _C_IQ)
    if 'vt' in out:
        vt = _dot_t(wvt_ref[...], hb).astype(BF16)
        for c in range(tm // DSA_KB):
            out['vt'][c] = vt[:, c * DSA_KB:(c + 1) * DSA_KB]
    out['iq'][...] = proj(_C_IQ, _C_SM).astype(BF16)
    ik = proj(_C_SM, _C_CQ)[:, 0:IDX_DIM]
    out['ik'][...] = ik
    out['ikb'][...] = ik.astype(BF16)
    out['iwt'][...] = _dot_t(wiwt_ref[...], hb) * (1.0 / 16.0)
    out['cq'][...] = (proj(_C_CQ, _C_CK) * (C_HEAD_DIM ** -0.5)).astype(BF16)
    ck = proj(_C_CK, _C_CV)
    out['ck'][...] = ck
    cv = proj(_C_CV, _C_END)
    out['cv'][...] = cv
    if 'ckb' in out:
        out['ckb'][...] = ck.astype(BF16)
        out['cvb'][...] = cv.astype(BF16)

    au = jax.nn.gelu(proj(_C_AU, _C_AV))
    av = _rms(jax.nn.gelu(proj(_C_AV, _C_BQ)), avn_ref[...])
    if 'av' in out:
        out['av'][...] = av
    avb = av.astype(BF16)
    row = lax.broadcasted_iota(I32, (ca, ca), 0)
    col = lax.broadcasted_iota(I32, (ca, ca), 1)
    if sample:
        vis = (col // 16) == (row // 16)
    else:
        vis = (col // CHUNK) <= (row // CHUNK)
    for g in range(A_GROUPS):
        wm = jnp.where(vis, ma_ref[g], 0.0).astype(BF16)
        cs = slice(g * A_GROUP_DIM, (g + 1) * A_GROUP_DIM)
        for c in range(tm // ca):
            rs = slice(c * ca, (c + 1) * ca)
            mixed = _dot(wm, avb[rs, cs]) + abias_ref[:, cs]
            out['oa'][rs, cs] = (au[rs, cs] * mixed).astype(BF16)


_PROMPT_OUTS = ('oa', 'qg', 'bk', 'bkb', 'bv', 'vt', 'iq', 'ik', 'ikb', 'iwt', 'cq', 'ck', 'cv', 'ckb', 'cvb')
_SAMPLE_OUTS = ('oa', 'qh', 'bk', 'bv', 'iq', 'ik', 'ikb', 'iwt', 'cq', 'ck', 'cv', 'av')


_LEAF_NAMES = ('bk', 'bv', 'ik', 'ck', 'cv')


def _proj_call(x, lw, *, tm, ca, sample, layer=0, depth=1, prev=None):
    n = x.shape[0]
    grid = (n // tm,)

    def tile(c):
        return pl.BlockSpec((tm, c), lambda i: (i, 0))

    row_out = {'oa': (A_HALF, BF16), 'qg': (B_HEADS * LANES, BF16), 'qh': (512, BF16), 'bk': (128, F32), 'bkb': (128, BF16),
               'bv': (128, F32), 'iq': (256, BF16), 'ik': (IDX_DIM, F32), 'ikb': (IDX_DIM, BF16),
               'cq': (512, BF16), 'ck': (512, F32), 'cv': (512, F32), 'ckb': (512, BF16), 'cvb': (512, BF16),
               'av': (A_HALF, F32)}
    names = _SAMPLE_OUTS if sample else _PROMPT_OUTS
    out_specs, out_shape = [], []
    for name in names:
        if name == 'vt':
            out_specs.append(pl.BlockSpec((tm // DSA_KB, 128, DSA_KB), lambda i: (i, 0, 0)))
            out_shape.append(jax.ShapeDtypeStruct((n // DSA_KB, 128, DSA_KB), BF16))
        elif name == 'iwt':
            out_specs.append(pl.BlockSpec((IDX_HEADS, tm), lambda i: (0, i)))
            out_shape.append(jax.ShapeDtypeStruct((IDX_HEADS, n), F32))
        elif depth > 1 and name in _LEAF_NAMES:
            c, dt = row_out[name]
            out_specs.append(pl.BlockSpec((None, tm, c), lambda i: (layer, i, 0)))
            out_shape.append(jax.ShapeDtypeStruct((depth, n, c), dt))
        else:
            c, dt = row_out[name]
            out_specs.append(tile(c))
            out_shape.append(jax.ShapeDtypeStruct((n, c), dt))
    ma = lw['ma_sample'] if sample else lw['ma_prompt']
    abias = lw['abias_sample'] if sample else lw['abias_prompt']
    consts = [lw['norm_mix'], lw['w1'], lw['w_vt'], lw['w_iwt'], lw['a_vnorm'], lw['qn'], lw['kn'],
              lw['g512'], lw['g128'], ma, abias]
    shared = [prev[name] for name in _LEAF_NAMES] if prev is not None else []
    aliases = {1 + len(consts) + j: names.index(name) for j, name in enumerate(_LEAF_NAMES)} if shared else {}
    outs = pl.pallas_call(
        functools.partial(_proj_kernel, tm=tm, ca=ca, sample=sample, names=names, n_alias=len(shared)),
        grid=grid,
        in_specs=[tile(D_MODEL)] + [_const_spec(c.shape) for c in consts]
        + [pl.BlockSpec(memory_space=pl.ANY)] * len(shared),
        out_specs=out_specs,
        out_shape=out_shape,
        input_output_aliases=aliases,
        compiler_params=pltpu.CompilerParams(dimension_semantics=("arbitrary",), vmem_limit_bytes=VMEM_LIMIT),
        name="proj_sample" if sample else "proj_prompt",
    )(x, *consts, *shared)
    return dict(zip(names, outs))


def _dsa_kernel(qg_ref, iq_ref, iwt_ref, ik_ref, k_ref, vt_ref, tri_ref, o_ref, sc_ref,
                *, qb, kb_size, n_valid, topk):
    q0 = pl.program_id(1) * qb
    nkb = (q0 + qb + kb_size - 1) // kb_size

    cb_size = kb_size
    ncb = nkb
    cpos0 = lax.broadcasted_iota(I32, (cb_size, qb), 0)
    qpos_row = q0 + lax.broadcasted_iota(I32, (1, qb), 1)
    qchunk_row = qpos_row // CHUNK
    n_adm = jnp.minimum((qchunk_row + 1) * CHUNK, n_valid)

    def fold(x, op):
        return op(op(x.reshape(kb_size // 8, 8, qb), axis=0), axis=0, keepdims=True)

    iq = iq_ref[...]
    iq_rows = jnp.concatenate([iq[:, h * IDX_DIM:(h + 1) * IDX_DIM] for h in range(IDX_HEADS)], axis=0)
    iwt = iwt_ref[...]

    def score_body(cb, carry):
        mx, mn = carry
        off = pl.multiple_of(cb * cb_size, cb_size)
        d = _dot_t(ik_ref[pl.ds(off, cb_size), :], iq_rows)
        acc = jnp.zeros((cb_size, qb), F32)
        for h in range(IDX_HEADS):
            acc = acc + jnp.maximum(d[:, h * qb:(h + 1) * qb], 0.0) * iwt[h:h + 1, :]
        kpos = off + cpos0
        adm = (kpos // CHUNK <= qchunk_row) & (kpos < n_valid)
        sc = jnp.where(adm, acc, -jnp.inf)
        sc_ref[pl.ds(off, cb_size), :] = sc
        mx = jnp.maximum(mx, jnp.max(sc.reshape(cb_size // 8, 8, qb), axis=0))
        mn = jnp.minimum(mn, jnp.min(jnp.where(adm, acc, jnp.inf).reshape(cb_size // 8, 8, qb), axis=0))
        return mx, mn

    mx, mn = lax.fori_loop(0, ncb, score_body,
                           (jnp.full((8, qb), -jnp.inf, F32), jnp.full((8, qb), jnp.inf, F32)))
    smax = jnp.max(mx, axis=0, keepdims=True)
    smin = jnp.min(mn, axis=0, keepdims=True)

    acc_rows = 32

    def count(pred):
        def body(cb, cnt):
            off = pl.multiple_of(cb * cb_size, cb_size)
            ind = jnp.where(pred(sc_ref[pl.ds(off, cb_size), :], off), 1.0, 0.0)
            return cnt + jnp.sum(ind.reshape(cb_size // acc_rows, acc_rows, qb), axis=0)
        cnt = lax.fori_loop(0, ncb, body, jnp.zeros((acc_rows, qb), F32))
        return jnp.sum(jnp.sum(cnt.reshape(acc_rows // 8, 8, qb), axis=0), axis=0, keepdims=True)

    def reduce_min(fn):
        def body(cb, acc):
            off = pl.multiple_of(cb * cb_size, cb_size)
            v = fn(sc_ref[pl.ds(off, cb_size), :])
            return jnp.minimum(acc, jnp.min(v.reshape(cb_size // 8, 8, qb), axis=0))
        acc = lax.fori_loop(0, ncb, body, jnp.full((8, qb), jnp.inf, F32))
        return jnp.min(acc, axis=0, keepdims=True)

    k_f = float(topk)
    few = n_adm <= topk
    steps_per_trip = 4

    c_ge0 = count(lambda blk, off: blk >= 0.0)
    c_gt0 = count(lambda blk, off: blk > 0.0)
    zero_tie = (c_ge0 >= k_f) & (c_gt0 < k_f)
    above = c_gt0 >= k_f

    def sel_cond(state):
        t, _, _, _, pending = state
        return (t < SELECT_MAX_STEPS) & (pending > 0.0)

    def sel_body(state):
        t, lo, hi, c_lo, _ = state
        for _ in range(steps_per_trip):
            mid = lo + (hi - lo) * 0.5
            c = count(lambda blk, off: blk >= mid)
            take = c >= k_f
            lo = jnp.where(take, mid, lo)
            hi = jnp.where(take, hi, mid)
            c_lo = jnp.where(take, c, c_lo)
        pending = jnp.max(jnp.where(few | zero_tie | (c_lo == k_f), 0.0, 1.0))
        return t + steps_per_trip, lo, hi, c_lo, pending

    _, lo, _, c_lo, pending = lax.while_loop(
        sel_cond, sel_body,
        (jnp.int32(0), jnp.where(above, 0.0, smin), jnp.where(above, smax, 0.0),
         jnp.where(above, c_ge0, n_adm.astype(F32)), jnp.float32(1.0)))

    def resolve(_):
        v = reduce_min(lambda blk: jnp.where(blk >= lo, blk, jnp.inf))
        return v, k_f - count(lambda blk, off: blk > v)

    v_cap, need_cap = lax.cond(pending > 0.0, resolve, lambda _: (lo, jnp.full((1, qb), jnp.inf, F32)), 0)
    unsettled = jnp.logical_not(few | zero_tie | (c_lo == k_f))
    thr = jnp.where(few, -jnp.inf, jnp.where(zero_tie, 0.0, jnp.where(unsettled, v_cap, lo)))
    need = jnp.where(few, jnp.inf, jnp.where(zero_tie, k_f - c_gt0, jnp.where(unsettled, need_cap, jnp.inf)))
    any_tie = jnp.max(jnp.where((zero_tie | unsettled) & jnp.logical_not(few), 1.0, 0.0))

    @pl.when(any_tie > 0.0)
    def _():
        tri = tri_ref[...]

        def body(cb, seen):
            off = pl.multiple_of(cb * cb_size, cb_size)
            sc = sc_ref[pl.ds(off, cb_size), :]
            eq = sc == thr
            rank = _dot(tri, jnp.where(eq, 1.0, 0.0).astype(BF16)) + seen
            sc_ref[pl.ds(off, cb_size), :] = jnp.where(eq & (rank > need), -jnp.inf, sc)
            return rank[cb_size - 1:cb_size, :]

        lax.fori_loop(0, ncb, body, jnp.zeros((1, qb), F32))

    qg = qg_ref[...]
    q_pairs = [jnp.concatenate([qg[:, (2 * p) * LANES:(2 * p + 1) * LANES],
                                qg[:, (2 * p + 1) * LANES:(2 * p + 2) * LANES]], axis=0)
               for p in range(B_HEADS // 2)]
    hpg = B_HEADS // B_KV_HEADS

    def att_body(kb, carry):
        ms, ls, accs = carry
        off = pl.multiple_of(kb * kb_size, kb_size)
        sc_blk = sc_ref[pl.ds(off, kb_size), :]
        sel = (sc_blk >= thr) & (sc_blk > -jnp.inf)
        bias = jnp.where(sel, 0.0, -jnp.inf)
        k_blk = k_ref[pl.ds(off, kb_size), :]
        vt_blk = vt_ref[kb]
        new_m, new_l, new_a = list(ms), list(ls), list(accs)
        sts = [_dot_t(k_blk, q_pairs[p]) for p in range(B_HEADS // 2)]
        pts, alphas = [], []
        for h in range(B_HEADS):
            s = sts[h // 2][:, (h % 2) * qb:(h % 2 + 1) * qb] + bias
            m_new = jnp.maximum(ms[h], fold(s, jnp.max))
            m_safe = jnp.where(m_new == -jnp.inf, 0.0, m_new)
            pt = jnp.exp2(s - m_safe)
            alpha = jnp.exp2(ms[h] - m_safe)
            new_m[h] = m_new
            new_l[h] = alpha * ls[h] + fold(pt, jnp.sum)
            pts.append(pt.astype(BF16))
            alphas.append(alpha)
        for h in range(B_HEADS):
            g = h // hpg
            new_a[h] = alphas[h] * accs[h] + _dot(vt_blk[g * B_HEAD_DIM:(g + 1) * B_HEAD_DIM, :], pts[h])
        return tuple(new_m), tuple(new_l), tuple(new_a)

    init = (tuple(jnp.full((1, qb), -jnp.inf, F32) for _ in range(B_HEADS)),
            tuple(jnp.zeros((1, qb), F32) for _ in range(B_HEADS)),
            tuple(jnp.zeros((B_HEAD_DIM, qb), F32) for _ in range(B_HEADS)))
    ms, ls, accs = lax.fori_loop(0, nkb, att_body, init)
    out_t = jnp.concatenate([accs[h] / ls[h] for h in range(B_HEADS)], axis=0)
    o_ref[...] = out_t.T.astype(BF16)


def _dsa_call(qg, iq, iwt, ik, k, vt, tri, *, qb, kb_size):
    b, sq, _ = qg.shape
    lp = ik.shape[1]
    nq = sq // qb
    nkb = lp // kb_size
    topk = min(B_TOPK_MAX, lp // 4)

    def qspec(c):
        return pl.BlockSpec((None, qb, c), lambda bi, i: (bi, i, 0))

    def kspec(c):
        return pl.BlockSpec((None, lp, c), lambda bi, i: (bi, 0, 0))

    return pl.pallas_call(
        functools.partial(_dsa_kernel, qb=qb, kb_size=kb_size, n_valid=lp, topk=topk),
        grid=(b, nq),
        in_specs=[qspec(B_HEADS * LANES), qspec(256),
                  pl.BlockSpec((None, IDX_HEADS, qb), lambda bi, i: (bi * nq + i, 0, 0)),
                  kspec(IDX_DIM), kspec(128),
                  pl.BlockSpec((nkb, 128, kb_size), lambda bi, i: (bi, 0, 0)), _const_spec(tri.shape)],
        out_specs=qspec(512),
        out_shape=jax.ShapeDtypeStruct((b, sq, 512), BF16),
        scratch_shapes=[pltpu.VMEM((lp, qb), F32)],
        compiler_params=pltpu.CompilerParams(dimension_semantics=("arbitrary", "arbitrary"),
                                             vmem_limit_bytes=VMEM_LIMIT),
        name="dsa_prompt",
    )(qg, iq, iwt, ik, k, vt, tri)


def _dsa_sample_kernel(qn_ref, iq_ref, iwc_ref, ikn_ref, ktn_ref, vtn_ref, ikp_ref, ktp_ref, vtp_ref, tri_ref, o_ref,
                       sc_ref, *, t, n_past, cb, topk):
    nb = n_past // cb
    hpg = B_HEADS // B_KV_HEADS
    lanes = lax.broadcasted_iota(I32, (t, cb), 1)
    iq = iq_ref[...]
    iq_rows = jnp.concatenate([iq[:, h * IDX_DIM:(h + 1) * IDX_DIM] for h in range(IDX_HEADS)], axis=0)
    iwc = iwc_ref[...]

    def scores(ik_blk):
        w = jnp.maximum(_dot_t(iq_rows, ik_blk), 0.0) * iwc
        s = w[0:t]
        for h in range(1, IDX_HEADS):
            s = s + w[h * t:(h + 1) * t]
        return s

    def fold_lanes(x, op):
        part = x[:, 0:LANES]
        for c in range(1, cb // LANES):
            part = op(part, x[:, c * LANES:(c + 1) * LANES])
        return part

    def past_scores(kb, carry):
        mx, mn = carry
        off = pl.multiple_of(kb * cb, cb)
        sc = scores(ikp_ref[pl.ds(off, cb), :].astype(BF16))
        sc_ref[kb] = sc
        return jnp.maximum(mx, fold_lanes(sc, jnp.maximum)), jnp.minimum(mn, fold_lanes(sc, jnp.minimum))

    mx, mn = lax.fori_loop(0, nb, past_scores,
                           (jnp.full((t, LANES), -jnp.inf, F32), jnp.full((t, LANES), jnp.inf, F32)))
    sc_new = scores(ikn_ref[...])
    valid = lanes < t
    sc_ref[nb] = jnp.where(valid, sc_new, -jnp.inf)
    mx = jnp.maximum(mx, fold_lanes(jnp.where(valid, sc_new, -jnp.inf), jnp.maximum))
    mn = jnp.minimum(mn, fold_lanes(jnp.where(valid, sc_new, jnp.inf), jnp.minimum))
    smax = jnp.max(mx, axis=1, keepdims=True)
    smin = jnp.min(mn, axis=1, keepdims=True)

    def count(pred):
        def body(kb, cnt):
            ind = jnp.where(pred(sc_ref[kb], kb * cb + lanes), 1.0, 0.0)
            return cnt + fold_lanes(ind, jnp.add)
        cnt = lax.fori_loop(0, nb + 1, body, jnp.zeros((t, LANES), F32))
        return jnp.sum(cnt, axis=1, keepdims=True)

    k_f = float(topk)
    steps_per_trip = 4

    c_ge0 = count(lambda blk, pos: blk >= 0.0)
    c_gt0 = count(lambda blk, pos: blk > 0.0)
    zero_tie = (c_ge0 >= k_f) & (c_gt0 < k_f)
    above = c_gt0 >= k_f

    def sel_cond(state):
        tt, _, _, _, pending = state
        return (tt < SELECT_MAX_STEPS) & (pending > 0.0)

    def sel_body(state):
        tt, lo, hi, c_lo, _ = state
        for _ in range(steps_per_trip):
            mid = lo + (hi - lo) * 0.5
            c = count(lambda blk, pos: blk >= mid)
            take = c >= k_f
            lo = jnp.where(take, mid, lo)
            hi = jnp.where(take, hi, mid)
            c_lo = jnp.where(take, c, c_lo)
        pending = jnp.max(jnp.where(zero_tie | (c_lo == k_f), 0.0, 1.0))
        return tt + steps_per_trip, lo, hi, c_lo, pending

    _, lo, _, c_lo, pending = lax.while_loop(
        sel_cond, sel_body,
        (jnp.int32(0), jnp.where(above, 0.0, smin), jnp.where(above, smax, 0.0),
         jnp.where(above, c_ge0, float(n_past + t)), jnp.float32(1.0)))

    def resolve(_):
        def body(kb, acc):
            blk = sc_ref[kb]
            return jnp.minimum(acc, fold_lanes(jnp.where(blk >= lo, blk, jnp.inf), jnp.minimum))
        acc = lax.fori_loop(0, nb + 1, body, jnp.full((t, LANES), jnp.inf, F32))
        v = jnp.min(acc, axis=1, keepdims=True)
        return v, k_f - count(lambda blk, pos: blk > v)

    v_cap, need_cap = lax.cond(pending > 0.0, resolve, lambda _: (lo, jnp.full((t, 1), jnp.inf, F32)), 0)
    unsettled = jnp.logical_not(zero_tie | (c_lo == k_f))
    thr = jnp.where(zero_tie, 0.0, jnp.where(unsettled, v_cap, lo))
    need = jnp.where(zero_tie, k_f - c_gt0, jnp.where(unsettled, need_cap, jnp.inf))
    any_tie = jnp.max(jnp.where(zero_tie | unsettled, 1.0, 0.0))

    @pl.when(any_tie > 0.0)
    def _():
        tri = tri_ref[...]

        def body(kb, seen):
            sc = sc_ref[kb]
            eq = sc == thr
            rank = _dot(jnp.where(eq, 1.0, 0.0).astype(BF16), tri) + seen
            sc_ref[kb] = jnp.where(eq & (rank > need), -jnp.inf, sc)
            return rank[:, cb - 1:cb]

        lax.fori_loop(0, nb + 1, body, jnp.zeros((t, 1), F32))

    qn = qn_ref[...]
    q_groups = [jnp.concatenate([qn[:, h * B_HEAD_DIM:(h + 1) * B_HEAD_DIM]
                                 for h in range(g * hpg, (g + 1) * hpg)], axis=0)
                for g in range(B_KV_HEADS)]

    def attend(carry, kb, kt, vt):
        ms, ls, accs = carry
        sc_blk = sc_ref[kb]
        sel = (sc_blk >= thr) & (sc_blk > -jnp.inf)
        bias = jnp.where(sel, 0.0, -jnp.inf)
        bias_rows = jnp.concatenate([bias] * hpg, axis=0)
        new_m, new_l, new_a = [], [], []
        for g in range(B_KV_HEADS):
            s = _dot(q_groups[g], kt(g)) + bias_rows
            m_new = jnp.maximum(ms[g], jnp.max(s, axis=1, keepdims=True))
            m_safe = jnp.where(m_new == -jnp.inf, 0.0, m_new)
            p = jnp.exp2(s - m_safe)
            alpha = jnp.exp2(ms[g] - m_safe)
            new_m.append(m_new)
            new_l.append(alpha * ls[g] + jnp.sum(p, axis=1, keepdims=True))
            new_a.append(alpha * accs[g] + _dot_t(p.astype(BF16), vt(g)))
        return tuple(new_m), tuple(new_l), tuple(new_a)

    def past_attend(kb, carry):
        off = pl.multiple_of(kb * cb, cb)
        return attend(carry, kb, lambda g: ktp_ref[g, :, pl.ds(off, cb)].astype(BF16),
                      lambda g: vtp_ref[g, :, pl.ds(off, cb)].astype(BF16))

    init = (tuple(jnp.full((hpg * t, 1), -jnp.inf, F32) for _ in range(B_KV_HEADS)),
            tuple(jnp.zeros((hpg * t, 1), F32) for _ in range(B_KV_HEADS)),
            tuple(jnp.zeros((hpg * t, B_HEAD_DIM), F32) for _ in range(B_KV_HEADS)))
    carry = lax.fori_loop(0, nb, past_attend, init)
    _, ls, accs = attend(carry, nb, lambda g: ktn_ref[g], lambda g: vtn_ref[g])
    for g in range(B_KV_HEADS):
        out = accs[g] / ls[g]
        for hh in range(hpg):
            h = g * hpg + hh
            o_ref[:, h * B_HEAD_DIM:(h + 1) * B_HEAD_DIM] = out[hh * t:(hh + 1) * t].astype(BF16)


def _dsa_sample_call(qn, iq, iwc, ikn, ktn, vtn, ikp, ktp, vtp, tri, layer, *, cb):
    b, t, _ = qn.shape
    n_past = ikp.shape[2]
    topk = min(B_TOPK_MAX, (n_past + t) // 4)
    assert n_past % cb == 0 and n_past + t > topk

    def bspec(*shape):
        zeros = (0,) * len(shape)
        return pl.BlockSpec((None,) + shape, lambda bi: (bi,) + zeros)

    def cspec(*shape):
        zeros = (0,) * len(shape)
        return pl.BlockSpec((None, None) + shape, lambda bi: (layer, bi) + zeros)

    return pl.pallas_call(
        functools.partial(_dsa_sample_kernel, t=t, n_past=n_past, cb=cb, topk=topk),
        grid=(b,),
        in_specs=[bspec(t, 512), bspec(t, 256), bspec(IDX_HEADS * t, 1),
                  bspec(cb, IDX_DIM), bspec(B_KV_HEADS, B_HEAD_DIM, cb), bspec(B_KV_HEADS, B_HEAD_DIM, cb),
                  cspec(n_past, IDX_DIM), cspec(B_KV_HEADS, B_HEAD_DIM, n_past),
                  cspec(B_KV_HEADS, B_HEAD_DIM, n_past), _const_spec(tri.shape)],
        out_specs=bspec(t, 512),
        out_shape=jax.ShapeDtypeStruct((b, t, 512), BF16),
        scratch_shapes=[pltpu.VMEM((n_past // cb + 1, t, cb), F32)],
        compiler_params=pltpu.CompilerParams(dimension_semantics=("arbitrary",), vmem_limit_bytes=VMEM_LIMIT),
        name="dsa_sample",
    )(qn, iq, iwc, ikn, ktn, vtn, ikp, ktp, vtp, tri)


SB_WIDE = 256
SB_GROUP = 4
SB_SAMPLE_WINDOW = 1024


def _sb_tri2(n):
    r = jnp.arange(2 * n)[:, None] % n
    c = jnp.arange(n)[None, :]
    return jnp.where(r > c, 1.0, 0.0).astype(BF16)


def _sb_stage(z_fn, pv, tri2_ref, carries, accs, vis):
    tri2 = tri2_ref[...]
    new_c, new_a = list(carries), list(accs)
    for h0 in range(0, C_HEADS, SB_GROUP):
        heads = range(h0, h0 + SB_GROUP)
        zs = [z_fn(h) for h in heads]
        sps = [jnp.maximum(z, 0.0) + jnp.log(1.0 + jnp.exp(-jnp.abs(z))) for z in zs]
        stays = [-sp if vis is None else jnp.where(vis, -sp, 0.0) for sp in sps]
        afters = []
        for ls in stays:
            hi = ls.astype(BF16)
            lo = (ls - hi.astype(F32)).astype(BF16)
            afters.append(_dot(jnp.concatenate([hi, lo], axis=1), tri2))
        for i, h in enumerate(heads):
            w = jnp.exp((zs[i] - sps[i]) + (afters[i] + carries[h]))
            if vis is not None:
                w = jnp.where(vis, w, 0.0)
            new_a[h] = accs[h] + pv(h, w.astype(BF16))
            new_c[h] = carries[h] + jnp.sum(stays[i], axis=1, keepdims=True)
    return tuple(new_c), tuple(new_a)


def _sb_max_carry(carries):
    m = carries[0]
    for c in carries[1:]:
        m = jnp.maximum(m, c)
    return jnp.max(m)


def _sb_init(qb, d):
    return (tuple(jnp.zeros((qb, 1), F32) for _ in range(C_HEADS)),
            tuple(jnp.zeros((qb, d), F32) for _ in range(C_HEADS)))


def _sb_prompt_kernel(q_ref, k_ref, v_ref, tri_ref, o_ref, *, qb):
    i = pl.program_id(1)
    q = q_ref[...]
    lane = lax.broadcasted_iota(I32, (qb, LANES), 1)
    q_heads = []
    for h in range(C_HEADS):
        pair = q[:, (h // 2) * LANES:(h // 2 + 1) * LANES]
        q_heads.append(jnp.where(lane // C_HEAD_DIM == h % 2, pair, jnp.zeros_like(pair)))

    def block(off, carries, accs, vis):
        k_blk = k_ref[pl.ds(off, qb), :]
        v_blk = v_ref[pl.ds(off, qb), :]
        z_fn = lambda h: _dot_t(q_heads[h], k_blk[:, (h // 2) * LANES:(h // 2 + 1) * LANES])
        pv = lambda h, w: _dot(w, v_blk[:, (h // 2) * LANES:(h // 2 + 1) * LANES])
        return _sb_stage(z_fn, pv, tri_ref, carries, accs, vis)

    rows = lax.broadcasted_iota(I32, (qb, qb), 0)
    cols = lax.broadcasted_iota(I32, (qb, qb), 1)
    carries, accs = _sb_init(qb, LANES)
    carries, accs = block(pl.multiple_of(i * qb, qb), carries, accs, cols < rows)

    def cond(state):
        j, mx, _, _ = state
        return (j < i) & (mx > SB_SKIP)

    def body(state):
        j, _, carries, accs = state
        off = pl.multiple_of((i - j - 1) * qb, qb)
        carries, accs = block(off, carries, accs, None)
        return j + 1, _sb_max_carry(carries), carries, accs

    _, _, _, accs = lax.while_loop(cond, body, (jnp.int32(0), _sb_max_carry(carries), carries, accs))
    for p in range(C_HEADS // 2):
        o_ref[:, p * LANES:(p + 1) * LANES] = jnp.where(lane < C_HEAD_DIM, accs[2 * p], accs[2 * p + 1]).astype(BF16)


def _sb_prompt_call(cq, ckb, cvb, lw, *, qb):
    b, s, _ = cq.shape
    qspec = pl.BlockSpec((None, qb, 512), lambda bi, i: (bi, i, 0))
    kspec = pl.BlockSpec((None, s, 512), lambda bi, i: (bi, 0, 0))
    return pl.pallas_call(
        functools.partial(_sb_prompt_kernel, qb=qb),
        grid=(b, s // qb),
        in_specs=[qspec, kspec, kspec, _const_spec(lw['tri2'].shape)],
        out_specs=qspec,
        out_shape=jax.ShapeDtypeStruct((b, s, 512), BF16),
        compiler_params=pltpu.CompilerParams(dimension_semantics=("arbitrary", "arbitrary"),
                                             vmem_limit_bytes=VMEM_LIMIT),
        name="stick_prompt",
    )(cq, ckb, cvb, lw['tri2'])


def _sb_sample_kernel(q_ref, knt_ref, vnt_ref, kpt_ref, vpt_ref, tri1_ref, tri2_ref, o_ref, done_ref, *, qb, n_hist):
    q = q_ref[...]
    q_heads = [q[:, h * C_HEAD_DIM:(h + 1) * C_HEAD_DIM] for h in range(C_HEADS)]
    lane = lax.broadcasted_iota(I32, (qb, LANES), 1)
    rows = lax.broadcasted_iota(I32, (qb, LANES), 0)

    def block(kt, vt, tri_ref, carries, accs, vis):
        z_fn = lambda h: _dot(q_heads[h], kt(h))
        pv = lambda h, w: _dot_t(w, vt(h))
        return _sb_stage(z_fn, pv, tri_ref, carries, accs, vis)

    carries, accs = _sb_init(qb, C_HEAD_DIM)
    carries, accs = block(lambda h: knt_ref[h], lambda h: vnt_ref[h], tri1_ref, carries, accs, lane < rows)

    def cond(state):
        j, mx, _, _ = state
        return (j < n_hist // SB_WIDE) & (mx > SB_SKIP)

    def body(state):
        j, _, carries, accs = state
        off = pl.multiple_of(n_hist - (j + 1) * SB_WIDE, SB_WIDE)
        carries, accs = block(lambda h: kpt_ref[h, :, pl.ds(off, SB_WIDE)].astype(BF16),
                              lambda h: vpt_ref[h, :, pl.ds(off, SB_WIDE)].astype(BF16),
                              tri2_ref, carries, accs, None)
        return j + 1, _sb_max_carry(carries), carries, accs

    _, mx, _, accs = lax.while_loop(cond, body, (jnp.int32(0), _sb_max_carry(carries), carries, accs))
    for h in range(C_HEADS):
        o_ref[:, h * C_HEAD_DIM:(h + 1) * C_HEAD_DIM] = accs[h].astype(BF16)
    done_ref[...] = jnp.full(done_ref.shape, jnp.where(mx > SB_SKIP, 0.0, 1.0), F32)


def _sb_sample_call(cq, knt, vnt, kpt, vpt, lw, layer, n_hist):
    b, t, _ = cq.shape
    n_past = kpt.shape[-1]
    qspec = pl.BlockSpec((None, t, 512), lambda bi: (bi, 0, 0))
    nspec = pl.BlockSpec((None, C_HEADS, C_HEAD_DIM, LANES), lambda bi: (bi, 0, 0, 0))
    pspec = pl.BlockSpec((None, None, C_HEADS, C_HEAD_DIM, n_hist), lambda bi: (layer, bi, 0, 0, n_past // n_hist - 1))
    return pl.pallas_call(
        functools.partial(_sb_sample_kernel, qb=t, n_hist=n_hist),
        grid=(b,),
        in_specs=[qspec, nspec, nspec, pspec, pspec, _const_spec(lw['tri1'].shape), _const_spec(lw['tri2'].shape)],
        out_specs=[qspec, pl.BlockSpec((None, 8, LANES), lambda bi: (bi, 0, 0))],
        out_shape=[jax.ShapeDtypeStruct((b, t, 512), BF16), jax.ShapeDtypeStruct((b, 8, LANES), F32)],
        compiler_params=pltpu.CompilerParams(dimension_semantics=("arbitrary",), vmem_limit_bytes=VMEM_LIMIT),
        name="stick_sample",
    )(cq, knt, vnt, kpt, vpt, lw['tri1'], lw['tri2'])


def _sb_sample(cq, knt, vnt, kpt, vpt, lw, layer):
    n_past = kpt.shape[-1]
    n_hist = min(SB_SAMPLE_WINDOW, n_past)
    oc, done = _sb_sample_call(cq, knt, vnt, kpt, vpt, lw, layer, n_hist)
    if n_hist == n_past:
        return oc
    return lax.cond(jnp.all(done > 0.0), lambda: oc,
                    lambda: _sb_sample_call(cq, knt, vnt, kpt, vpt, lw, layer, n_past)[0])


def _merge_kernel(x_ref, nmix_ref, wg_ref, gb_ref, oa_ref, ob_ref, oc_ref, wa_ref, wb_ref, wc_ref, wo_ref, o_ref):
    x = x_ref[...]
    hb = _rms(x, nmix_ref[...]).astype(BF16)
    merged = None
    for idx, (br_ref, w_ref) in enumerate(((oa_ref, wa_ref), (ob_ref, wb_ref), (oc_ref, wc_ref))):
        cs = slice(idx * D_MODEL, (idx + 1) * D_MODEL)
        gate = jax.nn.sigmoid(_dot(hb, wg_ref[:, cs]) + gb_ref[:, cs])
        term = gate * _dot(br_ref[...], w_ref[...])
        merged = term if merged is None else merged + term
    o_ref[...] = x + _dot(merged.astype(BF16), wo_ref[...])


def _merge_call(x, oa, ob, oc, lw, *, tm, name):
    n = x.shape[0]

    def tile(c):
        return pl.BlockSpec((tm, c), lambda i: (i, 0))

    consts_a = [lw['norm_mix'], lw['w_gate'], lw['gate_bias']]
    consts_b = [lw['w_br_a'], lw['w_br_b'], lw['w_br_c'], lw['w_out']]
    return pl.pallas_call(
        _merge_kernel,
        grid=(n // tm,),
        in_specs=([tile(D_MODEL)] + [_const_spec(c.shape) for c in consts_a] + [tile(512)] * 3
                  + [_const_spec(c.shape) for c in consts_b]),
        out_specs=tile(D_MODEL),
        out_shape=jax.ShapeDtypeStruct((n, D_MODEL), F32),
        compiler_params=pltpu.CompilerParams(dimension_semantics=("arbitrary",), vmem_limit_bytes=VMEM_LIMIT),
        name=name,
    )(x, *consts_a, oa, ob, oc, *consts_b)


def _ffn_kernel(x_ref, p_ref, nffn_ref, wfi_ref, wfo_ref, nple_ref, wpg_ref, wpp_ref, o_ref):
    x = x_ref[...]
    hf = _rms(x, nffn_ref[...]).astype(BF16)
    g = _dot(hf, wfi_ref[:, 0:D_FF])
    up = _dot(hf, wfi_ref[:, D_FF:2 * D_FF])
    x = x + _dot((jax.nn.silu(g) * up).astype(BF16), wfo_ref[...])
    hp = _rms(x, nple_ref[...]).astype(BF16)
    gate = jax.nn.sigmoid(_dot(hp, wpg_ref[...]))
    o_ref[...] = x + gate * _dot(p_ref[...].astype(BF16), wpp_ref[...])


def _ffn_call(x, p, layer, lw, *, tm, name):
    n = x.shape[0]
    consts = [lw['norm_ffn'], lw['w_ffn_in'], lw['w_ffn_out'], lw['norm_ple'], lw['w_ple_gate'], lw['w_ple_proj']]
    return pl.pallas_call(
        _ffn_kernel,
        grid=(n // tm,),
        in_specs=[pl.BlockSpec((tm, D_MODEL), lambda i: (i, 0)),
                  pl.BlockSpec((None, tm, PLE_DIM), lambda i: (layer, i, 0))]
        + [_const_spec(c.shape) for c in consts],
        out_specs=pl.BlockSpec((tm, D_MODEL), lambda i: (i, 0)),
        out_shape=jax.ShapeDtypeStruct((n, D_MODEL), F32),
        compiler_params=pltpu.CompilerParams(dimension_semantics=("arbitrary",), vmem_limit_bytes=VMEM_LIMIT),
        name=name,
    )(x, p, *consts)


def _block_diag_mean(n):
    r = jnp.arange(n)
    return jnp.where((r[:, None] // B_HEAD_DIM) == (r[None, :] // B_HEAD_DIM), 1.0 / B_HEAD_DIM, 0.0).astype(BF16)


def _rank_tri(n):
    r = jnp.arange(n)
    return jnp.where(r[None, :] <= r[:, None], 1.0, 0.0).astype(BF16)


def _prep_layer(l, t_sample, w):
    w_in = w['w_in'][l]
    small = jnp.pad(w_in[:, 2048:2088], ((0, 0), (0, LANES - IDX_DIM - IDX_HEADS)))
    w1 = jnp.concatenate([w_in[:, :2048], small, w_in[:, 2088:3624]], axis=1).astype(BF16)
    a_ws = w['a_ws'][l]
    a_bias_full = jnp.repeat(w['a_bias'][l].T, A_GROUP_DIM, axis=1)
    row = lambda a: a.reshape(1, -1)
    return {
        'norm_mix': row(w['norm_mix'][l]), 'w1': w1,
        'w_vt': w_in[:, 1664:1792].T.astype(BF16), 'w_iwt': w_in[:, 2080:2088].T.astype(BF16),
        'a_vnorm': row(w['a_vnorm'][l]),
        'qn': row(jnp.tile(w['b_qnorm'][l], B_HEADS)), 'kn': row(jnp.tile(w['b_knorm'][l], B_KV_HEADS)),
        'g512': _block_diag_mean(512), 'g128': _block_diag_mean(128),
        'tri1': _sb_tri2(LANES), 'tri2': _sb_tri2(SB_WIDE),
        'tri_rank': _rank_tri(DSA_KB), 'tri_rank_t': _rank_tri(2 * DSA_KB).T,
        'ma_prompt': a_ws, 'abias_prompt': a_bias_full,
        'ma_sample': jnp.tile(a_ws[:, :t_sample, :t_sample], (1, 16, 16)),
        'abias_sample': jnp.tile(a_bias_full[:t_sample], (16, 1)),
        'w_gate': w_in[:, 3624:].astype(BF16), 'gate_bias': row(w['gate_bias'][l]),
        'w_br_a': w['w_br_a'][l].astype(BF16), 'w_br_b': w['w_br_b'][l].astype(BF16),
        'w_br_c': w['w_br_c'][l].astype(BF16), 'w_out': w['w_out'][l].astype(BF16),
        'norm_ffn': row(w['norm_ffn'][l]), 'w_ffn_in': w['w_ffn_in'][l].astype(BF16),
        'w_ffn_out': w['w_ffn_out'][l].astype(BF16), 'norm_ple': row(w['norm_ple'][l]),
        'w_ple_gate': w['w_ple_gate'][l].astype(BF16), 'w_ple_proj': w['w_ple_proj'][l].astype(BF16),
    }


def _iwt_blocks(iwt, qb):
    return iwt.reshape(IDX_HEADS, -1, qb).transpose(1, 0, 2)


def _prompt_layer(x, p, lw, b, s, layer, depth, prev):
    pr = _proj_call(x, lw, tm=512, ca=A_CHUNK, sample=False, layer=layer, depth=depth, prev=prev)
    r3 = lambda a: a.reshape(b, s, a.shape[-1])
    ob = _dsa_call(r3(pr['qg']), r3(pr['iq']), _iwt_blocks(pr['iwt'], 256), r3(pr['ikb']), r3(pr['bkb']), pr['vt'],
                   lw['tri_rank'], qb=256, kb_size=DSA_KB)
    oc = _sb_prompt_call(r3(pr['cq']), r3(pr['ckb']), r3(pr['cvb']), lw, qb=SB_WIDE)
    x = _merge_call(x, pr['oa'], ob.reshape(b * s, 512), oc.reshape(b * s, 512), lw, tm=512, name="merge_prompt")
    x = _ffn_call(x, p, layer, lw, tm=512, name="ffn_prompt")
    return x, pr


def _sample_layer(x, p, lw, b, t, caches, layer):
    pik, pbk_t, pbv_t, pck_t, pcv_t = caches
    pr = _proj_call(x, lw, tm=b * t, ca=b * t, sample=True)
    r3 = lambda a: a.reshape(b, t, a.shape[-1])
    cb = 2 * DSA_KB

    def new_t(a, heads, width):
        at = a.reshape(b, t, heads, -1).transpose(0, 2, 3, 1).astype(BF16)
        return jnp.pad(at, ((0, 0), (0, 0), (0, 0), (0, width - t)))

    ikn = jnp.pad(r3(pr['ikb']), ((0, 0), (0, cb - t), (0, 0)))
    iwc = _iwt_blocks(pr['iwt'], t).reshape(b, IDX_HEADS * t, 1)
    ob = _dsa_sample_call(r3(pr['qh']), r3(pr['iq']), iwc, ikn, new_t(pr['bk'], B_KV_HEADS, cb),
                          new_t(pr['bv'], B_KV_HEADS, cb), pik, pbk_t, pbv_t, lw['tri_rank_t'], layer, cb=cb)
    oc = _sb_sample(r3(pr['cq']), new_t(pr['ck'], C_HEADS, LANES), new_t(pr['cv'], C_HEADS, LANES),
                    pck_t, pcv_t, lw, layer)
    x = _merge_call(x, pr['oa'], ob.reshape(b * t, 512), oc.reshape(b * t, 512), lw, tm=b * t, name="merge_sample")
    x = _ffn_call(x, p, layer, lw, tm=b * t, name="ffn_sample")
    return x, pr


def kernel(x_prompt, x_sample, cache_b_k, cache_b_v, cache_b_kidx, cache_c_k, cache_c_v, p_prompt, p_sample,
           norm_mix, w_in, gate_bias, a_vnorm, a_ws, a_bias, b_qnorm, b_knorm, w_br_a, w_br_b, w_br_c, w_out,
           norm_ffn, w_ffn_in, w_ffn_out, norm_ple, w_ple_gate, w_ple_proj):
    weights = dict(norm_mix=norm_mix, w_in=w_in, gate_bias=gate_bias, a_vnorm=a_vnorm, a_ws=a_ws, a_bias=a_bias,
                   b_qnorm=b_qnorm, b_knorm=b_knorm, w_br_a=w_br_a, w_br_b=w_br_b, w_br_c=w_br_c, w_out=w_out,
                   norm_ffn=norm_ffn, w_ffn_in=w_ffn_in, w_ffn_out=w_ffn_out, norm_ple=norm_ple,
                   w_ple_gate=w_ple_gate, w_ple_proj=w_ple_proj)
    depth = w_in.shape[0]
    bp, sp, _ = x_prompt.shape
    bs, ts, _ = x_sample.shape
    yp = x_prompt.reshape(bp * sp, D_MODEL)
    ys = x_sample.reshape(bs * ts, D_MODEL)
    caches = (cache_b_kidx,) + tuple(c.transpose(0, 1, 3, 4, 2) for c in (cache_b_k, cache_b_v, cache_c_k, cache_c_v))
    new_p, new_s = [], []
    for l in range(depth):
        lw = _prep_layer(l, ts, weights)
        yp, pr_p = _prompt_layer(yp, p_prompt.reshape(depth, bp * sp, PLE_DIM), lw, bp, sp, l, depth,
                                 new_p[-1] if new_p else None)
        ys, pr_s = _sample_layer(ys, p_sample.reshape(depth, bs * ts, PLE_DIM), lw, bs, ts, caches, l)
        new_p.append(pr_p)
        new_s.append(pr_s)

    def stack(states, name, lead, tail):
        return jnp.stack([s[name].reshape(*lead, *tail) for s in states])

    def shared(name, tail):
        return new_p[-1][name].reshape(depth, bp, sp, *tail)

    ls = (bs, ts)
    return (yp.reshape(bp, sp, D_MODEL), ys.reshape(bs, ts, D_MODEL),
            shared('bk', (B_KV_HEADS, B_HEAD_DIM)), shared('bv', (B_KV_HEADS, B_HEAD_DIM)), shared('ik', (IDX_DIM,)),
            shared('ck', (C_HEADS, C_HEAD_DIM)), shared('cv', (C_HEADS, C_HEAD_DIM)),
            stack(new_s, 'bk', ls, (B_KV_HEADS, B_HEAD_DIM)), stack(new_s, 'bv', ls, (B_KV_HEADS, B_HEAD_DIM)),
            stack(new_s, 'ik', ls, (IDX_DIM,)),
            stack(new_s, 'ck', ls, (C_HEADS, C_HEAD_DIM)), stack(new_s, 'cv', ls, (C_HEADS, C_HEAD_DIM)),
            stack(new_s, 'av', ls, (A_HALF,)))
```

```python
import functools

import jax
import jax.numpy as jnp
from jax import lax
from jax.experimental import pallas as pl
from jax.experimental.pallas import tpu as pltpu

F32 = jnp.float32
BF16 = jnp.bfloat16
I32 = jnp.int32

D_MODEL = 1024
CHUNK = 64
EPS = 1e-6
A_CHUNK = 128
A_GROUPS = 4
A_GROUP_DIM = 128
A_HALF = 512
B_HEADS = 8
B_KV_HEADS = 2
B_HEAD_DIM = 64
B_TOPK_MAX = 256
IDX_HEADS = 8
IDX_DIM = 32
C_HEADS = 8
C_HEAD_DIM = 64
N_BRANCH = 3
D_FF = 2816
PLE_DIM = 256

LANES = 128
VMEM_LIMIT = 56 * 1024 * 1024

_C_AU, _C_AV, _C_BQ, _C_BK, _C_BV, _C_IQ = 0, 512, 1024, 1536, 1664, 1792
_C_SM, _C_CQ, _C_CK, _C_CV, _C_END = 2048, 2176, 2688, 3200, 3712
DSA_KB = 512

SELECT_MAX_STEPS = 48
SB_SKIP = -104.0
LOG2E = 1.4426950408889634


def _const_spec(shape):
    zeros = (0,) * len(shape)
    return pl.BlockSpec(shape, lambda *_: zeros, pipeline_mode=pl.Buffered(1))


def _rms(x, gain):
    return x * lax.rsqrt(jnp.mean(x * x, axis=-1, keepdims=True) + EPS) * gain


def _dot(a, b):
    return jnp.dot(a, b, preferred_element_type=F32)


def _dot_t(a, b):
    return lax.dot_general(a, b, (((1,), (1,)), ((), ())), preferred_element_type=F32)


def _split_dot(x, m_ref):
    hi = x.astype(BF16)
    lo = (x - hi.astype(F32)).astype(BF16)
    m = m_ref[...]
    return _dot(hi, m) + _dot(lo, m)


def _group_rms(z, g_ref, gain):
    return z * lax.rsqrt(_split_dot(z * z, g_ref) + EPS) * gain


def _proj_kernel(x_ref, nmix_ref, w_ref, wvt_ref, wiwt_ref, avn_ref, qn_ref, kn_ref, g512_ref, g128_ref,
                 ma_ref, abias_ref, *rest, tm, ca, sample, names, n_alias):
    out = dict(zip(names, rest[n_alias:]))
    hb = _rms(x_ref[...], nmix_ref[...]).astype(BF16)

    def proj(c0, c1):
        return _dot(hb, w_ref[:, c0:c1])

    qn = _group_rms(proj(_C_BQ, _C_BK), g512_ref, qn_ref[...]) * (B_HEAD_DIM ** -0.5 * LOG2E)
    if 'qh' in out:
        out['qh'][...] = qn.astype(BF16)
    lane = lax.broadcasted_iota(I32, (tm, LANES), 1)
    for h in range(B_HEADS if 'qg' in out else 0):
        g = h // (B_HEADS // B_KV_HEADS)
        pair = qn[:, (h // 2) * LANES:(h // 2 + 1) * LANES]
        if h % 2 != g:
            pair = pltpu.roll(pair, B_HEAD_DIM, axis=1)
        out['qg'][:, h * LANES:(h + 1) * LANES] = jnp.where(lane // B_HEAD_DIM == g, pair, 0.0).astype(BF16)
    bk = _group_rms(proj(_C_BK, _C_BV), g128_ref, kn_ref[...])
    out['bk'][...] = bk
    if 'bkb' in out:
        out['bkb'][...] = bk.astype(BF16)
    out['bv'][...] = proj(_C_BV, _C_IQ)
    if 'vt' in out:
        vt = _dot_t(wvt_ref[...], hb).astype(BF16)
        for c in range(tm // DSA_KB):
            out['vt'][c] = vt[:, c * DSA_KB:(c + 1) * DSA_KB]
    out['iq'][...] = proj(_C_IQ, _C_SM).astype(BF16)
    ik = proj(_C_SM, _C_CQ)[:, 0:IDX_DIM]
    out['ik'][...] = ik
    out['ikb'][...] = ik.astype(BF16)
    out['iwt'][...] = _dot_t(wiwt_ref[...], hb) * (1.0 / 16.0)
    out['cq'][...] = (proj(_C_CQ, _C_CK) * (C_HEAD_DIM ** -0.5)).astype(BF16)
    ck = proj(_C_CK, _C_CV)
    out['ck'][...] = ck
    cv = proj(_C_CV, _C_END)
    out['cv'][...] = cv
    if 'ckb' in out:
        out['ckb'][...] = ck.astype(BF16)
        out['cvb'][...] = cv.astype(BF16)

    au = jax.nn.gelu(proj(_C_AU, _C_AV))
    av = _rms(jax.nn.gelu(proj(_C_AV, _C_BQ)), avn_ref[...])
    if 'av' in out:
        out['av'][...] = av
    avb = av.astype(BF16)
    row = lax.broadcasted_iota(I32, (ca, ca), 0)
    col = lax.broadcasted_iota(I32, (ca, ca), 1)
    if sample:
        vis = (col // 16) == (row // 16)
    else:
        vis = (col // CHUNK) <= (row // CHUNK)
    for g in range(A_GROUPS):
        wm = jnp.where(vis, ma_ref[g], 0.0).astype(BF16)
        cs = slice(g * A_GROUP_DIM, (g + 1) * A_GROUP_DIM)
        for c in range(tm // ca):
            rs = slice(c * ca, (c + 1) * ca)
            mixed = _dot(wm, avb[rs, cs]) + abias_ref[:, cs]
            out['oa'][rs, cs] = (au[rs, cs] * mixed).astype(BF16)


_PROMPT_OUTS = ('oa', 'qg', 'bk', 'bkb', 'bv', 'vt', 'iq', 'ik', 'ikb', 'iwt', 'cq', 'ck', 'cv', 'ckb', 'cvb')
_SAMPLE_OUTS = ('oa', 'qh', 'bk', 'bv', 'iq', 'ik', 'ikb', 'iwt', 'cq', 'ck', 'cv', 'av')


_LEAF_NAMES = ('bk', 'bv', 'ik', 'ck', 'cv')


def _proj_call(x, lw, *, tm, ca, sample, layer=0, depth=1, prev=None):
    n = x.shape[0]
    grid = (n // tm,)

    def tile(c):
        return pl.BlockSpec((tm, c), lambda i: (i, 0))

    row_out = {'oa': (A_HALF, BF16), 'qg': (B_HEADS * LANES, BF16), 'qh': (512, BF16), 'bk': (128, F32), 'bkb': (128, BF16),
               'bv': (128, F32), 'iq': (256, BF16), 'ik': (IDX_DIM, F32), 'ikb': (IDX_DIM, BF16),
               'cq': (512, BF16), 'ck': (512, F32), 'cv': (512, F32), 'ckb': (512, BF16), 'cvb': (512, BF16),
               'av': (A_HALF, F32)}
    names = _SAMPLE_OUTS if sample else _PROMPT_OUTS
    out_specs, out_shape = [], []
    for name in names:
        if name == 'vt':
            out_specs.append(pl.BlockSpec((tm // DSA_KB, 128, DSA_KB), lambda i: (i, 0, 0)))
            out_shape.append(jax.ShapeDtypeStruct((n // DSA_KB, 128, DSA_KB), BF16))
        elif name == 'iwt':
            out_specs.append(pl.BlockSpec((IDX_HEADS, tm), lambda i: (0, i)))
            out_shape.append(jax.ShapeDtypeStruct((IDX_HEADS, n), F32))
        elif depth > 1 and name in _LEAF_NAMES:
            c, dt = row_out[name]
            out_specs.append(pl.BlockSpec((None, tm, c), lambda i: (layer, i, 0)))
            out_shape.append(jax.ShapeDtypeStruct((depth, n, c), dt))
        else:
            c, dt = row_out[name]
            out_specs.append(tile(c))
            out_shape.append(jax.ShapeDtypeStruct((n, c), dt))
    ma = lw['ma_sample'] if sample else lw['ma_prompt']
    abias = lw['abias_sample'] if sample else lw['abias_prompt']
    consts = [lw['norm_mix'], lw['w1'], lw['w_vt'], lw['w_iwt'], lw['a_vnorm'], lw['qn'], lw['kn'],
              lw['g512'], lw['g128'], ma, abias]
    shared = [prev[name] for name in _LEAF_NAMES] if prev is not None else []
    aliases = {1 + len(consts) + j: names.index(name) for j, name in enumerate(_LEAF_NAMES)} if shared else {}
    outs = pl.pallas_call(
        functools.partial(_proj_kernel, tm=tm, ca=ca, sample=sample, names=names, n_alias=len(shared)),
        grid=grid,
        in_specs=[tile(D_MODEL)] + [_const_spec(c.shape) for c in consts]
        + [pl.BlockSpec(memory_space=pl.ANY)] * len(shared),
        out_specs=out_specs,
        out_shape=out_shape,
        input_output_aliases=aliases,
        compiler_params=pltpu.CompilerParams(dimension_semantics=("arbitrary",), vmem_limit_bytes=VMEM_LIMIT),
        name="proj_sample" if sample else "proj_prompt",
    )(x, *consts, *shared)
    return dict(zip(names, outs))


def _dsa_kernel(qg_ref, iq_ref, iwt_ref, ik_ref, k_ref, vt_ref, tri_ref, o_ref, sc_ref,
                *, qb, kb_size, n_valid, topk):
    q0 = pl.program_id(1) * qb
    nkb = (q0 + qb + kb_size - 1) // kb_size

    cb_size = kb_size
    ncb = nkb
    cpos0 = lax.broadcasted_iota(I32, (cb_size, qb), 0)
    qpos_row = q0 + lax.broadcasted_iota(I32, (1, qb), 1)
    qchunk_row = qpos_row // CHUNK
    n_adm = jnp.minimum((qchunk_row + 1) * CHUNK, n_valid)

    def fold(x, op):
        return op(op(x.reshape(kb_size // 8, 8, qb), axis=0), axis=0, keepdims=True)

    iq = iq_ref[...]
    iq_rows = jnp.concatenate([iq[:, h * IDX_DIM:(h + 1) * IDX_DIM] for h in range(IDX_HEADS)], axis=0)
    iwt = iwt_ref[...]

    def score_body(cb, carry):
        mx, mn = carry
        off = pl.multiple_of(cb * cb_size, cb_size)
        d = _dot_t(ik_ref[pl.ds(off, cb_size), :], iq_rows)
        acc = jnp.zeros((cb_size, qb), F32)
        for h in range(IDX_HEADS):
            acc = acc + jnp.maximum(d[:, h * qb:(h + 1) * qb], 0.0) * iwt[h:h + 1, :]
        kpos = off + cpos0
        adm = (kpos // CHUNK <= qchunk_row) & (kpos < n_valid)
        sc = jnp.where(adm, acc, -jnp.inf)
        sc_ref[pl.ds(off, cb_size), :] = sc
        mx = jnp.maximum(mx, jnp.max(sc.reshape(cb_size // 8, 8, qb), axis=0))
        mn = jnp.minimum(mn, jnp.min(jnp.where(adm, acc, jnp.inf).reshape(cb_size // 8, 8, qb), axis=0))
        return mx, mn

    mx, mn = lax.fori_loop(0, ncb, score_body,
                           (jnp.full((8, qb), -jnp.inf, F32), jnp.full((8, qb), jnp.inf, F32)))
    smax = jnp.max(mx, axis=0, keepdims=True)
    smin = jnp.min(mn, axis=0, keepdims=True)

    acc_rows = 32

    def count(pred):
        def body(cb, cnt):
            off = pl.multiple_of(cb * cb_size, cb_size)
            ind = jnp.where(pred(sc_ref[pl.ds(off, cb_size), :], off), 1.0, 0.0)
            return cnt + jnp.sum(ind.reshape(cb_size // acc_rows, acc_rows, qb), axis=0)
        cnt = lax.fori_loop(0, ncb, body, jnp.zeros((acc_rows, qb), F32))
        return jnp.sum(jnp.sum(cnt.reshape(acc_rows // 8, 8, qb), axis=0), axis=0, keepdims=True)

    def reduce_min(fn):
        def body(cb, acc):
            off = pl.multiple_of(cb * cb_size, cb_size)
            v = fn(sc_ref[pl.ds(off, cb_size), :])
            return jnp.minimum(acc, jnp.min(v.reshape(cb_size // 8, 8, qb), axis=0))
        acc = lax.fori_loop(0, ncb, body, jnp.full((8, qb), jnp.inf, F32))
        return jnp.min(acc, axis=0, keepdims=True)

    k_f = float(topk)
    few = n_adm <= topk
    steps_per_trip = 4

    c_ge0 = count(lambda blk, off: blk >= 0.0)
    c_gt0 = count(lambda blk, off: blk > 0.0)
    zero_tie = (c_ge0 >= k_f) & (c_gt0 < k_f)
    above = c_gt0 >= k_f

    def sel_cond(state):
        t, _, _, _, pending = state
        return (t < SELECT_MAX_STEPS) & (pending > 0.0)

    def sel_body(state):
        t, lo, hi, c_lo, _ = state
        for _ in range(steps_per_trip):
            mid = lo + (hi - lo) * 0.5
            c = count(lambda blk, off: blk >= mid)
            take = c >= k_f
            lo = jnp.where(take, mid, lo)
            hi = jnp.where(take, hi, mid)
            c_lo = jnp.where(take, c, c_lo)
        pending = jnp.max(jnp.where(few | zero_tie | (c_lo == k_f), 0.0, 1.0))
        return t + steps_per_trip, lo, hi, c_lo, pending

    _, lo, _, c_lo, pending = lax.while_loop(
        sel_cond, sel_body,
        (jnp.int32(0), jnp.where(above, 0.0, smin), jnp.where(above, smax, 0.0),
         jnp.where(above, c_ge0, n_adm.astype(F32)), jnp.float32(1.0)))

    def resolve(_):
        v = reduce_min(lambda blk: jnp.where(blk >= lo, blk, jnp.inf))
        return v, k_f - count(lambda blk, off: blk > v)

    v_cap, need_cap = lax.cond(pending > 0.0, resolve, lambda _: (lo, jnp.full((1, qb), jnp.inf, F32)), 0)
    unsettled = jnp.logical_not(few | zero_tie | (c_lo == k_f))
    thr = jnp.where(few, -jnp.inf, jnp.where(zero_tie, 0.0, jnp.where(unsettled, v_cap, lo)))
    need = jnp.where(few, jnp.inf, jnp.where(zero_tie, k_f - c_gt0, jnp.where(unsettled, need_cap, jnp.inf)))
    any_tie = jnp.max(jnp.where((zero_tie | unsettled) & jnp.logical_not(few), 1.0, 0.0))

    @pl.when(any_tie > 0.0)
    def _():
        tri = tri_ref[...]

        def body(cb, seen):
            off = pl.multiple_of(cb * cb_size, cb_size)
            sc = sc_ref[pl.ds(off, cb_size), :]
            eq = sc == thr
            rank = _dot(tri, jnp.where(eq, 1.0, 0.0).astype(BF16)) + seen
            sc_ref[pl.ds(off, cb_size), :] = jnp.where(eq & (rank > need), -jnp.inf, sc)
            return rank[cb_size - 1:cb_size, :]

        lax.fori_loop(0, ncb, body, jnp.zeros((1, qb), F32))

    qg = qg_ref[...]
    q_pairs = [jnp.concatenate([qg[:, (2 * p) * LANES:(2 * p + 1) * LANES],
                                qg[:, (2 * p + 1) * LANES:(2 * p + 2) * LANES]], axis=0)
               for p in range(B_HEADS // 2)]
    hpg = B_HEADS // B_KV_HEADS

    def att_body(kb, carry):
        ms, ls, accs = carry
        off = pl.multiple_of(kb * kb_size, kb_size)
        sc_blk = sc_ref[pl.ds(off, kb_size), :]
        sel = (sc_blk >= thr) & (sc_blk > -jnp.inf)
        bias = jnp.where(sel, 0.0, -jnp.inf)
        k_blk = k_ref[pl.ds(off, kb_size), :]
        vt_blk = vt_ref[kb]
        new_m, new_l, new_a = list(ms), list(ls), list(accs)
        sts = [_dot_t(k_blk, q_pairs[p]) for p in range(B_HEADS // 2)]
        pts, alphas = [], []
        for h in range(B_HEADS):
            s = sts[h // 2][:, (h % 2) * qb:(h % 2 + 1) * qb] + bias
            m_new = jnp.maximum(ms[h], fold(s, jnp.max))
            m_safe = jnp.where(m_new == -jnp.inf, 0.0, m_new)
            pt = jnp.exp2(s - m_safe)
            alpha = jnp.exp2(ms[h] - m_safe)
            new_m[h] = m_new
            new_l[h] = alpha * ls[h] + fold(pt, jnp.sum)
            pts.append(pt.astype(BF16))
            alphas.append(alpha)
        for h in range(B_HEADS):
            g = h // hpg
            new_a[h] = alphas[h] * accs[h] + _dot(vt_blk[g * B_HEAD_DIM:(g + 1) * B_HEAD_DIM, :], pts[h])
        return tuple(new_m), tuple(new_l), tuple(new_a)

    init = (tuple(jnp.full((1, qb), -jnp.inf, F32) for _ in range(B_HEADS)),
            tuple(jnp.zeros((1, qb), F32) for _ in range(B_HEADS)),
            tuple(jnp.zeros((B_HEAD_DIM, qb), F32) for _ in range(B_HEADS)))
    ms, ls, accs = lax.fori_loop(0, nkb, att_body, init)
    out_t = jnp.concatenate([accs[h] / ls[h] for h in range(B_HEADS)], axis=0)
    o_ref[...] = out_t.T.astype(BF16)


def _dsa_call(qg, iq, iwt, ik, k, vt, tri, *, qb, kb_size):
    b, sq, _ = qg.shape
    lp = ik.shape[1]
    nq = sq // qb
    nkb = lp // kb_size
    topk = min(B_TOPK_MAX, lp // 4)

    def qspec(c):
        return pl.BlockSpec((None, qb, c), lambda bi, i: (bi, i, 0))

    def kspec(c):
        return pl.BlockSpec((None, lp, c), lambda bi, i: (bi, 0, 0))

    return pl.pallas_call(
        functools.partial(_dsa_kernel, qb=qb, kb_size=kb_size, n_valid=lp, topk=topk),
        grid=(b, nq),
        in_specs=[qspec(B_HEADS * LANES), qspec(256),
                  pl.BlockSpec((None, IDX_HEADS, qb), lambda bi, i: (bi * nq + i, 0, 0)),
                  kspec(IDX_DIM), kspec(128),
                  pl.BlockSpec((nkb, 128, kb_size), lambda bi, i: (bi, 0, 0)), _const_spec(tri.shape)],
        out_specs=qspec(512),
        out_shape=jax.ShapeDtypeStruct((b, sq, 512), BF16),
        scratch_shapes=[pltpu.VMEM((lp, qb), F32)],
        compiler_params=pltpu.CompilerParams(dimension_semantics=("arbitrary", "arbitrary"),
                                             vmem_limit_bytes=VMEM_LIMIT),
        name="dsa_prompt",
    )(qg, iq, iwt, ik, k, vt, tri)


def _dsa_sample_kernel(qn_ref, iq_ref, iwc_ref, ikn_ref, ktn_ref, vtn_ref, ikp_ref, ktp_ref, vtp_ref, tri_ref, o_ref,
                       sc_ref, *, t, n_past, cb, topk):
    nb = n_past // cb
    hpg = B_HEADS // B_KV_HEADS
    lanes = lax.broadcasted_iota(I32, (t, cb), 1)
    iq = iq_ref[...]
    iq_rows = jnp.concatenate([iq[:, h * IDX_DIM:(h + 1) * IDX_DIM] for h in range(IDX_HEADS)], axis=0)
    iwc = iwc_ref[...]

    def scores(ik_blk):
        w = jnp.maximum(_dot_t(iq_rows, ik_blk), 0.0) * iwc
        s = w[0:t]
        for h in range(1, IDX_HEADS):
            s = s + w[h * t:(h + 1) * t]
        return s

    def fold_lanes(x, op):
        part = x[:, 0:LANES]
        for c in range(1, cb // LANES):
            part = op(part, x[:, c * LANES:(c + 1) * LANES])
        return part

    def past_scores(kb, carry):
        mx, mn = carry
        off = pl.multiple_of(kb * cb, cb)
        sc = scores(ikp_ref[pl.ds(off, cb), :].astype(BF16))
        sc_ref[kb] = sc
        return jnp.maximum(mx, fold_lanes(sc, jnp.maximum)), jnp.minimum(mn, fold_lanes(sc, jnp.minimum))

    mx, mn = lax.fori_loop(0, nb, past_scores,
                           (jnp.full((t, LANES), -jnp.inf, F32), jnp.full((t, LANES), jnp.inf, F32)))
    sc_new = scores(ikn_ref[...])
    valid = lanes < t
    sc_ref[nb] = jnp.where(valid, sc_new, -jnp.inf)
    mx = jnp.maximum(mx, fold_lanes(jnp.where(valid, sc_new, -jnp.inf), jnp.maximum))
    mn = jnp.minimum(mn, fold_lanes(jnp.where(valid, sc_new, jnp.inf), jnp.minimum))
    smax = jnp.max(mx, axis=1, keepdims=True)
    smin = jnp.min(mn, axis=1, keepdims=True)

    def count(pred):
        def body(kb, cnt):
            ind = jnp.where(pred(sc_ref[kb], kb * cb + lanes), 1.0, 0.0)
            return cnt + fold_lanes(ind, jnp.add)
        cnt = lax.fori_loop(0, nb + 1, body, jnp.zeros((t, LANES), F32))
        return jnp.sum(cnt, axis=1, keepdims=True)

    k_f = float(topk)
    steps_per_trip = 4

    c_ge0 = count(lambda blk, pos: blk >= 0.0)
    c_gt0 = count(lambda blk, pos: blk > 0.0)
    zero_tie = (c_ge0 >= k_f) & (c_gt0 < k_f)
    above = c_gt0 >= k_f

    def sel_cond(state):
        tt, _, _, _, pending = state
        return (tt < SELECT_MAX_STEPS) & (pending > 0.0)

    def sel_body(state):
        tt, lo, hi, c_lo, _ = state
        for _ in range(steps_per_trip):
            mid = lo + (hi - lo) * 0.5
            c = count(lambda blk, pos: blk >= mid)
            take = c >= k_f
            lo = jnp.where(take, mid, lo)
            hi = jnp.where(take, hi, mid)
            c_lo = jnp.where(take, c, c_lo)
        pending = jnp.max(jnp.where(zero_tie | (c_lo == k_f), 0.0, 1.0))
        return tt + steps_per_trip, lo, hi, c_lo, pending

    _, lo, _, c_lo, pending = lax.while_loop(
        sel_cond, sel_body,
        (jnp.int32(0), jnp.where(above, 0.0, smin), jnp.where(above, smax, 0.0),
         jnp.where(above, c_ge0, float(n_past + t)), jnp.float32(1.0)))

    def resolve(_):
        def body(kb, acc):
            blk = sc_ref[kb]
            return jnp.minimum(acc, fold_lanes(jnp.where(blk >= lo, blk, jnp.inf), jnp.minimum))
        acc = lax.fori_loop(0, nb + 1, body, jnp.full((t, LANES), jnp.inf, F32))
        v = jnp.min(acc, axis=1, keepdims=True)
        return v, k_f - count(lambda blk, pos: blk > v)

    v_cap, need_cap = lax.cond(pending > 0.0, resolve, lambda _: (lo, jnp.full((t, 1), jnp.inf, F32)), 0)
    unsettled = jnp.logical_not(zero_tie | (c_lo == k_f))
    thr = jnp.where(zero_tie, 0.0, jnp.where(unsettled, v_cap, lo))
    need = jnp.where(zero_tie, k_f - c_gt0, jnp.where(unsettled, need_cap, jnp.inf))
    any_tie = jnp.max(jnp.where(zero_tie | unsettled, 1.0, 0.0))

    @pl.when(any_tie > 0.0)
    def _():
        tri = tri_ref[...]

        def body(kb, seen):
            sc = sc_ref[kb]
            eq = sc == thr
            rank = _dot(jnp.where(eq, 1.0, 0.0).astype(BF16), tri) + seen
            sc_ref[kb] = jnp.where(eq & (rank > need), -jnp.inf, sc)
            return rank[:, cb - 1:cb]

        lax.fori_loop(0, nb + 1, body, jnp.zeros((t, 1), F32))

    qn = qn_ref[...]
    q_groups = [jnp.concatenate([qn[:, h * B_HEAD_DIM:(h + 1) * B_HEAD_DIM]
                                 for h in range(g * hpg, (g + 1) * hpg)], axis=0)
                for g in range(B_KV_HEADS)]

    def attend(carry, kb, kt, vt):
        ms, ls, accs = carry
        sc_blk = sc_ref[kb]
        sel = (sc_blk >= thr) & (sc_blk > -jnp.inf)
        bias = jnp.where(sel, 0.0, -jnp.inf)
        bias_rows = jnp.concatenate([bias] * hpg, axis=0)
        new_m, new_l, new_a = [], [], []
        for g in range(B_KV_HEADS):
            s = _dot(q_groups[g], kt(g)) + bias_rows
            m_new = jnp.maximum(ms[g], jnp.max(s, axis=1, keepdims=True))
            m_safe = jnp.where(m_new == -jnp.inf, 0.0, m_new)
            p = jnp.exp2(s - m_safe)
            alpha = jnp.exp2(ms[g] - m_safe)
            new_m.append(m_new)
            new_l.append(alpha * ls[g] + jnp.sum(p, axis=1, keepdims=True))
            new_a.append(alpha * accs[g] + _dot_t(p.astype(BF16), vt(g)))
        return tuple(new_m), tuple(new_l), tuple(new_a)

    def past_attend(kb, carry):
        off = pl.multiple_of(kb * cb, cb)
        return attend(carry, kb, lambda g: ktp_ref[g, :, pl.ds(off, cb)].astype(BF16),
                      lambda g: vtp_ref[g, :, pl.ds(off, cb)].astype(BF16))

    init = (tuple(jnp.full((hpg * t, 1), -jnp.inf, F32) for _ in range(B_KV_HEADS)),
            tuple(jnp.zeros((hpg * t, 1), F32) for _ in range(B_KV_HEADS)),
            tuple(jnp.zeros((hpg * t, B_HEAD_DIM), F32) for _ in range(B_KV_HEADS)))
    carry = lax.fori_loop(0, nb, past_attend, init)
    _, ls, accs = attend(carry, nb, lambda g: ktn_ref[g], lambda g: vtn_ref[g])
    for g in range(B_KV_HEADS):
        out = accs[g] / ls[g]
        for hh in range(hpg):
            h = g * hpg + hh
            o_ref[:, h * B_HEAD_DIM:(h + 1) * B_HEAD_DIM] = out[hh * t:(hh + 1) * t].astype(BF16)


def _dsa_sample_call(qn, iq, iwc, ikn, ktn, vtn, ikp, ktp, vtp, tri, layer, *, cb):
    b, t, _ = qn.shape
    n_past = ikp.shape[2]
    topk = min(B_TOPK_MAX, (n_past + t) // 4)
    assert n_past % cb == 0 and n_past + t > topk

    def bspec(*shape):
        zeros = (0,) * len(shape)
        return pl.BlockSpec((None,) + shape, lambda bi: (bi,) + zeros)

    def cspec(*shape):
        zeros = (0,) * len(shape)
        return pl.BlockSpec((None, None) + shape, lambda bi: (layer, bi) + zeros)

    return pl.pallas_call(
        functools.partial(_dsa_sample_kernel, t=t, n_past=n_past, cb=cb, topk=topk),
        grid=(b,),
        in_specs=[bspec(t, 512), bspec(t, 256), bspec(IDX_HEADS * t, 1),
                  bspec(cb, IDX_DIM), bspec(B_KV_HEADS, B_HEAD_DIM, cb), bspec(B_KV_HEADS, B_HEAD_DIM, cb),
                  cspec(n_past, IDX_DIM), cspec(B_KV_HEADS, B_HEAD_DIM, n_past),
                  cspec(B_KV_HEADS, B_HEAD_DIM, n_past), _const_spec(tri.shape)],
        out_specs=bspec(t, 512),
        out_shape=jax.ShapeDtypeStruct((b, t, 512), BF16),
        scratch_shapes=[pltpu.VMEM((n_past // cb + 1, t, cb), F32)],
        compiler_params=pltpu.CompilerParams(dimension_semantics=("arbitrary",), vmem_limit_bytes=VMEM_LIMIT),
        name="dsa_sample",
    )(qn, iq, iwc, ikn, ktn, vtn, ikp, ktp, vtp, tri)


SB_WIDE = 256
SB_GROUP = 4
SB_SAMPLE_WINDOW = 1024


def _sb_tri2(n):
    r = jnp.arange(2 * n)[:, None] % n
    c = jnp.arange(n)[None, :]
    return jnp.where(r > c, 1.0, 0.0).astype(BF16)


def _sb_stage(z_fn, pv, tri2_ref, carries, accs, vis):
    tri2 = tri2_ref[...]
    new_c, new_a = list(carries), list(accs)
    for h0 in range(0, C_HEADS, SB_GROUP):
        heads = range(h0, h0 + SB_GROUP)
        zs = [z_fn(h) for h in heads]
        sps = [jnp.maximum(z, 0.0) + jnp.log(1.0 + jnp.exp(-jnp.abs(z))) for z in zs]
        stays = [-sp if vis is None else jnp.where(vis, -sp, 0.0) for sp in sps]
        afters = []
        for ls in stays:
            hi = ls.astype(BF16)
            lo = (ls - hi.astype(F32)).astype(BF16)
            afters.append(_dot(jnp.concatenate([hi, lo], axis=1), tri2))
        for i, h in enumerate(heads):
            w = jnp.exp((zs[i] - sps[i]) + (afters[i] + carries[h]))
            if vis is not None:
                w = jnp.where(vis, w, 0.0)
            new_a[h] = accs[h] + pv(h, w.astype(BF16))
            new_c[h] = carries[h] + jnp.sum(stays[i], axis=1, keepdims=True)
    return tuple(new_c), tuple(new_a)


def _sb_max_carry(carries):
    m = carries[0]
    for c in carries[1:]:
        m = jnp.maximum(m, c)
    return jnp.max(m)


def _sb_init(qb, d):
    return (tuple(jnp.zeros((qb, 1), F32) for _ in range(C_HEADS)),
            tuple(jnp.zeros((qb, d), F32) for _ in range(C_HEADS)))


def _sb_prompt_kernel(q_ref, k_ref, v_ref, tri_ref, o_ref, *, qb):
    i = pl.program_id(1)
    q = q_ref[...]
    lane = lax.broadcasted_iota(I32, (qb, LANES), 1)
    q_heads = []
    for h in range(C_HEADS):
        pair = q[:, (h // 2) * LANES:(h // 2 + 1) * LANES]
        q_heads.append(jnp.where(lane // C_HEAD_DIM == h % 2, pair, jnp.zeros_like(pair)))

    def block(off, carries, accs, vis):
        k_blk = k_ref[pl.ds(off, qb), :]
        v_blk = v_ref[pl.ds(off, qb), :]
        z_fn = lambda h: _dot_t(q_heads[h], k_blk[:, (h // 2) * LANES:(h // 2 + 1) * LANES])
        pv = lambda h, w: _dot(w, v_blk[:, (h // 2) * LANES:(h // 2 + 1) * LANES])
        return _sb_stage(z_fn, pv, tri_ref, carries, accs, vis)

    rows = lax.broadcasted_iota(I32, (qb, qb), 0)
    cols = lax.broadcasted_iota(I32, (qb, qb), 1)
    carries, accs = _sb_init(qb, LANES)
    carries, accs = block(pl.multiple_of(i * qb, qb), carries, accs, cols < rows)

    def cond(state):
        j, mx, _, _ = state
        return (j < i) & (mx > SB_SKIP)

    def body(state):
        j, _, carries, accs = state
        off = pl.multiple_of((i - j - 1) * qb, qb)
        carries, accs = block(off, carries, accs, None)
        return j + 1, _sb_max_carry(carries), carries, accs

    _, _, _, accs = lax.while_loop(cond, body, (jnp.int32(0), _sb_max_carry(carries), carries, accs))
    for p in range(C_HEADS // 2):
        o_ref[:, p * LANES:(p + 1) * LANES] = jnp.where(lane < C_HEAD_DIM, accs[2 * p], accs[2 * p + 1]).astype(BF16)


def _sb_prompt_call(cq, ckb, cvb, lw, *, qb):
    b, s, _ = cq.shape
    qspec = pl.BlockSpec((None, qb, 512), lambda bi, i: (bi, i, 0))
    kspec = pl.BlockSpec((None, s, 512), lambda bi, i: (bi, 0, 0))
    return pl.pallas_call(
        functools.partial(_sb_prompt_kernel, qb=qb),
        grid=(b, s // qb),
        in_specs=[qspec, kspec, kspec, _const_spec(lw['tri2'].shape)],
        out_specs=qspec,
        out_shape=jax.ShapeDtypeStruct((b, s, 512), BF16),
        compiler_params=pltpu.CompilerParams(dimension_semantics=("arbitrary", "arbitrary"),
                                             vmem_limit_bytes=VMEM_LIMIT),
        name="stick_prompt",
    )(cq, ckb, cvb, lw['tri2'])


def _sb_sample_kernel(q_ref, knt_ref, vnt_ref, kpt_ref, vpt_ref, tri1_ref, tri2_ref, o_ref, done_ref, *, qb, n_hist):
    q = q_ref[...]
    q_heads = [q[:, h * C_HEAD_DIM:(h + 1) * C_HEAD_DIM] for h in range(C_HEADS)]
    lane = lax.broadcasted_iota(I32, (qb, LANES), 1)
    rows = lax.broadcasted_iota(I32, (qb, LANES), 0)

    def block(kt, vt, tri_ref, carries, accs, vis):
        z_fn = lambda h: _dot(q_heads[h], kt(h))
        pv = lambda h, w: _dot_t(w, vt(h))
        return _sb_stage(z_fn, pv, tri_ref, carries, accs, vis)

    carries, accs = _sb_init(qb, C_HEAD_DIM)
    carries, accs = block(lambda h: knt_ref[h], lambda h: vnt_ref[h], tri1_ref, carries, accs, lane < rows)

    def cond(state):
        j, mx, _, _ = state
        return (j < n_hist // SB_WIDE) & (mx > SB_SKIP)

    def body(state):
        j, _, carries, accs = state
        off = pl.multiple_of(n_hist - (j + 1) * SB_WIDE, SB_WIDE)
        carries, accs = block(lambda h: kpt_ref[h, :, pl.ds(off, SB_WIDE)].astype(BF16),
                              lambda h: vpt_ref[h, :, pl.ds(off, SB_WIDE)].astype(BF16),
                              tri2_ref, carries, accs, None)
        return j + 1, _sb_max_carry(carries), carries, accs

    _, mx, _, accs = lax.while_loop(cond, body, (jnp.int32(0), _sb_max_carry(carries), carries, accs))
    for h in range(C_HEADS):
        o_ref[:, h * C_HEAD_DIM:(h + 1) * C_HEAD_DIM] = accs[h].astype(BF16)
    done_ref[...] = jnp.full(done_ref.shape, jnp.where(mx > SB_SKIP, 0.0, 1.0), F32)


def _sb_sample_call(cq, knt, vnt, kpt, vpt, lw, layer, n_hist):
    b, t, _ = cq.shape
    n_past = kpt.shape[-1]
    qspec = pl.BlockSpec((None, t, 512), lambda bi: (bi, 0, 0))
    nspec = pl.BlockSpec((None, C_HEADS, C_HEAD_DIM, LANES), lambda bi: (bi, 0, 0, 0))
    pspec = pl.BlockSpec((None, None, C_HEADS, C_HEAD_DIM, n_hist), lambda bi: (layer, bi, 0, 0, n_past // n_hist - 1))
    return pl.pallas_call(
        functools.partial(_sb_sample_kernel, qb=t, n_hist=n_hist),
        grid=(b,),
        in_specs=[qspec, nspec, nspec, pspec, pspec, _const_spec(lw['tri1'].shape), _const_spec(lw['tri2'].shape)],
        out_specs=[qspec, pl.BlockSpec((None, 8, LANES), lambda bi: (bi, 0, 0))],
        out_shape=[jax.ShapeDtypeStruct((b, t, 512), BF16), jax.ShapeDtypeStruct((b, 8, LANES), F32)],
        compiler_params=pltpu.CompilerParams(dimension_semantics=("arbitrary",), vmem_limit_bytes=VMEM_LIMIT),
        name="stick_sample",
    )(cq, knt, vnt, kpt, vpt, lw['tri1'], lw['tri2'])


def _sb_sample(cq, knt, vnt, kpt, vpt, lw, layer):
    n_past = kpt.shape[-1]
    n_hist = min(SB_SAMPLE_WINDOW, n_past)
    oc, done = _sb_sample_call(cq, knt, vnt, kpt, vpt, lw, layer, n_hist)
    if n_hist == n_past:
        return oc
    return lax.cond(jnp.all(done > 0.0), lambda: oc,
                    lambda: _sb_sample_call(cq, knt, vnt, kpt, vpt, lw, layer, n_past)[0])


def _merge_kernel(x_ref, nmix_ref, wg_ref, gb_ref, oa_ref, ob_ref, oc_ref, wa_ref, wb_ref, wc_ref, wo_ref, o_ref):
    x = x_ref[...]
    hb = _rms(x, nmix_ref[...]).astype(BF16)
    merged = None
    for idx, (br_ref, w_ref) in enumerate(((oa_ref, wa_ref), (ob_ref, wb_ref), (oc_ref, wc_ref))):
        cs = slice(idx * D_MODEL, (idx + 1) * D_MODEL)
        gate = jax.nn.sigmoid(_dot(hb, wg_ref[:, cs]) + gb_ref[:, cs])
        term = gate * _dot(br_ref[...], w_ref[...])
        merged = term if merged is None else merged + term
    o_ref[...] = x + _dot(merged.astype(BF16), wo_ref[...])


def _merge_call(x, oa, ob, oc, lw, *, tm, name):
    n = x.shape[0]

    def tile(c):
        return pl.BlockSpec((tm, c), lambda i: (i, 0))

    consts_a = [lw['norm_mix'], lw['w_gate'], lw['gate_bias']]
    consts_b = [lw['w_br_a'], lw['w_br_b'], lw['w_br_c'], lw['w_out']]
    return pl.pallas_call(
        _merge_kernel,
        grid=(n // tm,),
        in_specs=([tile(D_MODEL)] + [_const_spec(c.shape) for c in consts_a] + [tile(512)] * 3
                  + [_const_spec(c.shape) for c in consts_b]),
        out_specs=tile(D_MODEL),
        out_shape=jax.ShapeDtypeStruct((n, D_MODEL), F32),
        compiler_params=pltpu.CompilerParams(dimension_semantics=("arbitrary",), vmem_limit_bytes=VMEM_LIMIT),
        name=name,
    )(x, *consts_a, oa, ob, oc, *consts_b)


def _ffn_kernel(x_ref, p_ref, nffn_ref, wfi_ref, wfo_ref, nple_ref, wpg_ref, wpp_ref, o_ref):
    x = x_ref[...]
    hf = _rms(x, nffn_ref[...]).astype(BF16)
    g = _dot(hf, wfi_ref[:, 0:D_FF])
    up = _dot(hf, wfi_ref[:, D_FF:2 * D_FF])
    x = x + _dot((jax.nn.silu(g) * up).astype(BF16), wfo_ref[...])
    hp = _rms(x, nple_ref[...]).astype(BF16)
    gate = jax.nn.sigmoid(_dot(hp, wpg_ref[...]))
    o_ref[...] = x + gate * _dot(p_ref[...].astype(BF16), wpp_ref[...])


def _ffn_call(x, p, layer, lw, *, tm, name):
    n = x.shape[0]
    consts = [lw['norm_ffn'], lw['w_ffn_in'], lw['w_ffn_out'], lw['norm_ple'], lw['w_ple_gate'], lw['w_ple_proj']]
    return pl.pallas_call(
        _ffn_kernel,
        grid=(n // tm,),
        in_specs=[pl.BlockSpec((tm, D_MODEL), lambda i: (i, 0)),
                  pl.BlockSpec((None, tm, PLE_DIM), lambda i: (layer, i, 0))]
        + [_const_spec(c.shape) for c in consts],
        out_specs=pl.BlockSpec((tm, D_MODEL), lambda i: (i, 0)),
        out_shape=jax.ShapeDtypeStruct((n, D_MODEL), F32),
        compiler_params=pltpu.CompilerParams(dimension_semantics=("arbitrary",), vmem_limit_bytes=VMEM_LIMIT),
        name=name,
    )(x, p, *consts)


def _post_kernel(x_ref, p_ref, nmix_ref, wg_ref, gb_ref, oa_ref, ob_ref, oc_ref, wa_ref, wb_ref, wc_ref, wo_ref,
                 nffn_ref, wfi_ref, wfo_ref, nple_ref, wpg_ref, wpp_ref, o_ref):
    _merge_kernel(x_ref, nmix_ref, wg_ref, gb_ref, oa_ref, ob_ref, oc_ref, wa_ref, wb_ref, wc_ref, wo_ref, o_ref)
    _ffn_kernel(o_ref, p_ref, nffn_ref, wfi_ref, wfo_ref, nple_ref, wpg_ref, wpp_ref, o_ref)


def _post_call(x, oa, ob, oc, p, layer, lw, *, tm, name):
    n = x.shape[0]

    def tile(c):
        return pl.BlockSpec((tm, c), lambda i: (i, 0))

    consts_a = [lw['norm_mix'], lw['w_gate'], lw['gate_bias']]
    consts_b = [lw['w_br_a'], lw['w_br_b'], lw['w_br_c'], lw['w_out'],
                lw['norm_ffn'], lw['w_ffn_in'], lw['w_ffn_out'], lw['norm_ple'], lw['w_ple_gate'], lw['w_ple_proj']]
    return pl.pallas_call(
        _post_kernel,
        grid=(n // tm,),
        in_specs=([tile(D_MODEL), pl.BlockSpec((None, tm, PLE_DIM), lambda i: (layer, i, 0))]
                  + [_const_spec(c.shape) for c in consts_a] + [tile(512)] * 3
                  + [_const_spec(c.shape) for c in consts_b]),
        out_specs=tile(D_MODEL),
        out_shape=jax.ShapeDtypeStruct((n, D_MODEL), F32),
        compiler_params=pltpu.CompilerParams(dimension_semantics=("arbitrary",), vmem_limit_bytes=VMEM_LIMIT),
        name=name,
    )(x, p, *consts_a, oa, ob, oc, *consts_b)


def _block_diag_mean(n):
    r = jnp.arange(n)
    return jnp.where((r[:, None] // B_HEAD_DIM) == (r[None, :] // B_HEAD_DIM), 1.0 / B_HEAD_DIM, 0.0).astype(BF16)


def _rank_tri(n):
    r = jnp.arange(n)
    return jnp.where(r[None, :] <= r[:, None], 1.0, 0.0).astype(BF16)


def _prep_layer(l, t_sample, w):
    w_in = w['w_in'][l]
    small = jnp.pad(w_in[:, 2048:2088], ((0, 0), (0, LANES - IDX_DIM - IDX_HEADS)))
    w1 = jnp.concatenate([w_in[:, :2048], small, w_in[:, 2088:3624]], axis=1).astype(BF16)
    a_ws = w['a_ws'][l]
    a_bias_full = jnp.repeat(w['a_bias'][l].T, A_GROUP_DIM, axis=1)
    row = lambda a: a.reshape(1, -1)
    return {
        'norm_mix': row(w['norm_mix'][l]), 'w1': w1,
        'w_vt': w_in[:, 1664:1792].T.astype(BF16), 'w_iwt': w_in[:, 2080:2088].T.astype(BF16),
        'a_vnorm': row(w['a_vnorm'][l]),
        'qn': row(jnp.tile(w['b_qnorm'][l], B_HEADS)), 'kn': row(jnp.tile(w['b_knorm'][l], B_KV_HEADS)),
        'g512': _block_diag_mean(512), 'g128': _block_diag_mean(128),
        'tri1': _sb_tri2(LANES), 'tri2': _sb_tri2(SB_WIDE),
        'tri_rank': _rank_tri(DSA_KB), 'tri_rank_t': _rank_tri(2 * DSA_KB).T,
        'ma_prompt': a_ws, 'abias_prompt': a_bias_full,
        'ma_sample': jnp.tile(a_ws[:, :t_sample, :t_sample], (1, 16, 16)),
        'abias_sample': jnp.tile(a_bias_full[:t_sample], (16, 1)),
        'w_gate': w_in[:, 3624:].astype(BF16), 'gate_bias': row(w['gate_bias'][l]),
        'w_br_a': w['w_br_a'][l].astype(BF16), 'w_br_b': w['w_br_b'][l].astype(BF16),
        'w_br_c': w['w_br_c'][l].astype(BF16), 'w_out': w['w_out'][l].astype(BF16),
        'norm_ffn': row(w['norm_ffn'][l]), 'w_ffn_in': w['w_ffn_in'][l].astype(BF16),
        'w_ffn_out': w['w_ffn_out'][l].astype(BF16), 'norm_ple': row(w['norm_ple'][l]),
        'w_ple_gate': w['w_ple_gate'][l].astype(BF16), 'w_ple_proj': w['w_ple_proj'][l].astype(BF16),
    }


def _iwt_blocks(iwt, qb):
    return iwt.reshape(IDX_HEADS, -1, qb).transpose(1, 0, 2)


def _prompt_layer(x, p, lw, b, s, layer, depth, prev):
    pr = _proj_call(x, lw, tm=512, ca=A_CHUNK, sample=False, layer=layer, depth=depth, prev=prev)
    r3 = lambda a: a.reshape(b, s, a.shape[-1])
    ob = _dsa_call(r3(pr['qg']), r3(pr['iq']), _iwt_blocks(pr['iwt'], 256), r3(pr['ikb']), r3(pr['bkb']), pr['vt'],
                   lw['tri_rank'], qb=256, kb_size=DSA_KB)
    oc = _sb_prompt_call(r3(pr['cq']), r3(pr['ckb']), r3(pr['cvb']), lw, qb=SB_WIDE)
    x = _post_call(x, pr['oa'], ob.reshape(b * s, 512), oc.reshape(b * s, 512), p, layer, lw, tm=256, name="post_prompt")
    return x, pr


def _sample_layer(x, p, lw, b, t, caches, layer):
    pik, pbk_t, pbv_t, pck_t, pcv_t = caches
    pr = _proj_call(x, lw, tm=b * t, ca=b * t, sample=True)
    r3 = lambda a: a.reshape(b, t, a.shape[-1])
    cb = 2 * DSA_KB

    def new_t(a, heads, width):
        at = a.reshape(b, t, heads, -1).transpose(0, 2, 3, 1).astype(BF16)
        return jnp.pad(at, ((0, 0), (0, 0), (0, 0), (0, width - t)))

    ikn = jnp.pad(r3(pr['ikb']), ((0, 0), (0, cb - t), (0, 0)))
    iwc = _iwt_blocks(pr['iwt'], t).reshape(b, IDX_HEADS * t, 1)
    ob = _dsa_sample_call(r3(pr['qh']), r3(pr['iq']), iwc, ikn, new_t(pr['bk'], B_KV_HEADS, cb),
                          new_t(pr['bv'], B_KV_HEADS, cb), pik, pbk_t, pbv_t, lw['tri_rank_t'], layer, cb=cb)
    oc = _sb_sample(r3(pr['cq']), new_t(pr['ck'], C_HEADS, LANES), new_t(pr['cv'], C_HEADS, LANES),
                    pck_t, pcv_t, lw, layer)
    x = _merge_call(x, pr['oa'], ob.reshape(b * t, 512), oc.reshape(b * t, 512), lw, tm=b * t, name="merge_sample")
    x = _ffn_call(x, p, layer, lw, tm=b * t, name="ffn_sample")
    return x, pr


def kernel(x_prompt, x_sample, cache_b_k, cache_b_v, cache_b_kidx, cache_c_k, cache_c_v, p_prompt, p_sample,
           norm_mix, w_in, gate_bias, a_vnorm, a_ws, a_bias, b_qnorm, b_knorm, w_br_a, w_br_b, w_br_c, w_out,
           norm_ffn, w_ffn_in, w_ffn_out, norm_ple, w_ple_gate, w_ple_proj):
    weights = dict(norm_mix=norm_mix, w_in=w_in, gate_bias=gate_bias, a_vnorm=a_vnorm, a_ws=a_ws, a_bias=a_bias,
                   b_qnorm=b_qnorm, b_knorm=b_knorm, w_br_a=w_br_a, w_br_b=w_br_b, w_br_c=w_br_c, w_out=w_out,
                   norm_ffn=norm_ffn, w_ffn_in=w_ffn_in, w_ffn_out=w_ffn_out, norm_ple=norm_ple,
                   w_ple_gate=w_ple_gate, w_ple_proj=w_ple_proj)
    depth = w_in.shape[0]
    bp, sp, _ = x_prompt.shape
    bs, ts, _ = x_sample.shape
    yp = x_prompt.reshape(bp * sp, D_MODEL)
    ys = x_sample.reshape(bs * ts, D_MODEL)
    caches = (cache_b_kidx,) + tuple(c.transpose(0, 1, 3, 4, 2) for c in (cache_b_k, cache_b_v, cache_c_k, cache_c_v))
    new_p, new_s = [], []
    for l in range(depth):
        lw = _prep_layer(l, ts, weights)
        yp, pr_p = _prompt_layer(yp, p_prompt.reshape(depth, bp * sp, PLE_DIM), lw, bp, sp, l, depth,
                                 new_p[-1] if new_p else None)
        ys, pr_s = _sample_layer(ys, p_sample.reshape(depth, bs * ts, PLE_DIM), lw, bs, ts, caches, l)
        new_p.append(pr_p)
        new_s.append(pr_s)

    def stack(states, name, lead, tail):
        return jnp.stack([s[name].reshape(*lead, *tail) for s in states])

    def shared(name, tail):
        return new_p[-1][name].reshape(depth, bp, sp, *tail)

    ls = (bs, ts)
    return (yp.reshape(bp, sp, D_MODEL), ys.reshape(bs, ts, D_MODEL),
            shared('bk', (B_KV_HEADS, B_HEAD_DIM)), shared('bv', (B_KV_HEADS, B_HEAD_DIM)), shared('ik', (IDX_DIM,)),
            shared('ck', (C_HEADS, C_HEAD_DIM)), shared('cv', (C_HEADS, C_HEAD_DIM)),
            stack(new_s, 'bk', ls, (B_KV_HEADS, B_HEAD_DIM)), stack(new_s, 'bv', ls, (B_KV_HEADS, B_HEAD_DIM)),
            stack(new_s, 'ik', ls, (IDX_DIM,)),
            stack(new_s, 'ck', ls, (C_HEADS, C_HEAD_DIM)), stack(new_s, 'cv', ls, (C_HEADS, C_HEAD_DIM)),
            stack(new_s, 'av', ls, (A_HALF,)))
```
